```python
import math
import jax, jax.numpy as jnp
from jax import lax
import numpy as np

D_MODEL = 1024
BATCH = 1
SEQ = 16384
DEPTH = 1
DEC_BATCH = 2
DEC_SEQ = 16384
PAST_LEN = 128

ATT_WIDTH = D_MODEL // 2
SSM_WIDTH = D_MODEL - ATT_WIDTH
HEAD_DIM = 64
N_HEADS = ATT_WIDTH // HEAD_DIM
DILATED_CONFIGS = ((128, 1), (512, 4), (2048, 16))
SSM_GROUP = 16
N_SSM_GROUPS = SSM_WIDTH // SSM_GROUP
SSM_STATE = 64
D_FF = 4 * D_MODEL
IN_WIDTH = 3 * ATT_WIDTH + SSM_WIDTH
ROPE_THETA = 10000.0
NORM_EPS = 1e-6
MASK_VALUE = -1e30

kernel_name = 'hybrid_dilated_attn_s5_encoder'


def rms_norm(x, gain):
    xf = x.astype(jnp.float32)
    y = xf * lax.rsqrt(jnp.mean(xf * xf, axis=-1, keepdims=True) + NORM_EPS)
    return (y * gain.astype(jnp.float32)).astype(x.dtype)


def apply_rotary(t):
    seq = t.shape[1]
    half = HEAD_DIM // 2
    inv_freq = 1.0 / (ROPE_THETA ** (jnp.arange(half, dtype=jnp.float32) / half))
    ang = jnp.arange(seq, dtype=jnp.float32)[:, None] * inv_freq[None, :]
    cos = jnp.cos(ang)[None, :, None, :]
    sin = jnp.sin(ang)[None, :, None, :]
    tf = t.astype(jnp.float32)
    t1, t2 = tf[..., :half], tf[..., half:]
    return jnp.concatenate([t1 * cos - t2 * sin, t2 * cos + t1 * sin], axis=-1)


def dilated_window_attention(q, k, v, window, dilation):
    bsz, seq, n_heads, head_dim = q.shape
    radius = window // 2 // dilation
    blk = radius
    unit = dilation * blk
    s_pad = -(-seq // unit) * unit
    sub_len = s_pad // dilation
    n_blk = sub_len // blk
    pad = s_pad - seq

    def to_sub(t):
        t = jnp.pad(t, ((0, 0), (0, pad), (0, 0), (0, 0)))
        t = t.reshape(bsz, sub_len, dilation, n_heads, head_dim).transpose(0, 2, 1, 3, 4)
        return t.reshape(bsz * dilation, n_blk, blk, n_heads, head_dim)

    def neighbours(t):
        tp = jnp.pad(t, ((0, 0), (1, 1), (0, 0), (0, 0), (0, 0)))
        return jnp.concatenate([tp[:, :-2], tp[:, 1:-1], tp[:, 2:]], axis=2)

    qs = to_sub(q)
    kn = neighbours(to_sub(k))
    vn = neighbours(to_sub(v))

    q_idx = jnp.arange(sub_len).reshape(n_blk, blk)
    k_idx = (jnp.arange(n_blk)[:, None] - 1) * blk + jnp.arange(3 * blk)[None, :]
    k_orig = k_idx[None] * dilation + jnp.arange(dilation)[:, None, None]
    k_valid = (k_idx[None] >= 0) & (k_orig < seq)
    near = jnp.abs(q_idx[:, :, None] - k_idx[:, None, :]) <= radius
    mask = near[None] & k_valid[:, :, None, :]
    mask = jnp.tile(mask, (bsz, 1, 1, 1))[:, :, None]

    scores = jnp.einsum('bnqhd,bnkhd->bnhqk', qs, kn) * (HEAD_DIM ** -0.5)
    scores = jnp.where(mask, scores, MASK_VALUE)
    m = jnp.max(scores, axis=-1, keepdims=True)
    p = jnp.exp(scores - m)
    denom = jnp.sum(p, axis=-1)
    out = jnp.einsum('bnhqk,bnkhd->bnqhd', p, vn) / jnp.swapaxes(denom, 2, 3)[..., None]
    lse = jnp.swapaxes(m[..., 0] + jnp.log(denom), 2, 3)

    def from_sub(t):
        rest = t.shape[3:]
        t = t.reshape((bsz, dilation, sub_len) + rest)
        t = jnp.moveaxis(t, 1, 2).reshape((bsz, s_pad) + rest)
        return t[:, :seq]

    return from_sub(out), from_sub(lse)


def dilated_attention_mixture(q, k, v):
    outs, lses = [], []
    for window, dilation in DILATED_CONFIGS:
        o, l = dilated_window_attention(q, k, v, window, dilation)
        outs.append(o)
        lses.append(l)
    weights = jax.nn.softmax(jnp.stack(lses, axis=0), axis=0)
    return jnp.einsum('gbsh,gbshd->bshd', weights, jnp.stack(outs, axis=0))


def _scan_combine(left, right):
    a_l, b_l = left
    a_r, b_r = right
    return a_r * a_l, a_r * b_l + b_r


def s5_direction(u_grp, a_re, a_im, log_step, b_re, b_im, c_re, c_im, reverse):
    f32 = jnp.float32
    lam = lax.complex(a_re.astype(f32), a_im.astype(f32))
    step = jnp.exp(log_step.astype(f32))[:, None]
    a_bar = jnp.exp(lam * step)
    b_bar = ((a_bar - 1.0) / lam)[:, :, None] * lax.complex(b_re.astype(f32), b_im.astype(f32))
    bu = jnp.einsum('bsgc,gpc->bsgp', u_grp.astype(jnp.complex64), b_bar)
    a_seq = jnp.broadcast_to(a_bar, bu.shape)
    _, states = lax.associative_scan(_scan_combine, (a_seq, bu), reverse=reverse, axis=1)
    c_mat = lax.complex(c_re.astype(f32), c_im.astype(f32))
    return jnp.real(jnp.einsum('bsgp,gcp->bsgc', states, c_mat))


def s5_mixer(u, a_re, a_im, log_step, b_re, b_im, c_re, c_im, d_skip, w_glu, b_glu):
    bsz, seq, _ = u.shape
    uf = u.astype(jnp.float32)
    ug = uf.reshape(bsz, seq, N_SSM_GROUPS, SSM_GROUP)
    y = (s5_direction(ug, a_re[0], a_im[0], log_step[0], b_re[0], b_im[0], c_re[0], c_im[0], False)
         + s5_direction(ug, a_re[1], a_im[1], log_step[1], b_re[1], b_im[1], c_re[1], c_im[1], True))
    y = y.reshape(bsz, seq, SSM_WIDTH) + d_skip.astype(jnp.float32) * uf
    y = jax.nn.gelu(y)
    return y * jax.nn.sigmoid(y @ w_glu.astype(jnp.float32) + b_glu.astype(jnp.float32))


def encoder_layer(x, g_mix_pre, w_in, a_re, a_im, log_step, b_re, b_im, c_re, c_im, d_skip,
                  w_glu, b_glu, w_out, g_mix_post, g_mlp_pre, w_up, w_down, g_mlp_post):
    bsz, seq, _ = x.shape
    h = rms_norm(x, g_mix_pre)
    proj = h @ w_in
    q = proj[..., :ATT_WIDTH].reshape(bsz, seq, N_HEADS, HEAD_DIM)
    k = proj[..., ATT_WIDTH:2 * ATT_WIDTH].reshape(bsz, seq, N_HEADS, HEAD_DIM)
    v = proj[..., 2 * ATT_WIDTH:3 * ATT_WIDTH].reshape(bsz, seq, N_HEADS, HEAD_DIM)
    u = proj[..., 3 * ATT_WIDTH:]
    att = dilated_attention_mixture(apply_rotary(q), apply_rotary(k), v.astype(jnp.float32))
    att = att.reshape(bsz, seq, ATT_WIDTH)
    ssm = s5_mixer(u, a_re, a_im, log_step, b_re, b_im, c_re, c_im, d_skip, w_glu, b_glu)
    mixed = jnp.concatenate([att, ssm], axis=-1).astype(x.dtype) @ w_out
    x = x + rms_norm(mixed, g_mix_post)
    h = rms_norm(x, g_mlp_pre)
    ff = jnp.square(jax.nn.relu(h @ w_up)) @ w_down
    return x + rms_norm(ff, g_mlp_post)


def setup_inputs(seed: int = 0) -> dict:
    key = jax.random.key(seed)
    ks = jax.random.split(key, 24)
    f32 = jnp.float32

    def nrm(k, shape, scale):
        return jax.random.normal(k, shape, f32) * scale

    G, P, C = N_SSM_GROUPS, SSM_STATE, SSM_GROUP
    n_idx = jnp.arange(P, dtype=f32)
    return {
        'x_prompt': nrm(ks[0], (BATCH, SEQ, D_MODEL), 1.0),
        'x_sample': nrm(ks[1], (DEC_BATCH, DEC_SEQ, D_MODEL), 1.0),
        'norm_mix_pre': 1.0 + nrm(ks[2], (DEPTH, D_MODEL), 0.02),
        'w_in': nrm(ks[3], (DEPTH, D_MODEL, IN_WIDTH), D_MODEL ** -0.5),
        'ssm_a_re': -0.5 + nrm(ks[4], (DEPTH, 2, G, P), 0.01),
        'ssm_a_im': math.pi * n_idx + nrm(ks[5], (DEPTH, 2, G, P), 0.01),
        'ssm_log_step': jax.random.uniform(ks[6], (DEPTH, 2, G), f32, math.log(1e-3), math.log(1e-1)),
        'ssm_b_re': nrm(ks[7], (DEPTH, 2, G, P, C), (2 * C) ** -0.5),
        'ssm_b_im': nrm(ks[8], (DEPTH, 2, G, P, C), (2 * C) ** -0.5),
        'ssm_c_re': nrm(ks[9], (DEPTH, 2, G, C, P), (2 * P) ** -0.5),
        'ssm_c_im': nrm(ks[10], (DEPTH, 2, G, C, P), (2 * P) ** -0.5),
        'ssm_d': nrm(ks[11], (DEPTH, SSM_WIDTH), 1.0),
        'w_glu': nrm(ks[12], (DEPTH, SSM_WIDTH, SSM_WIDTH), SSM_WIDTH ** -0.5),
        'b_glu': nrm(ks[13], (DEPTH, SSM_WIDTH), 0.02),
        'w_out': nrm(ks[14], (DEPTH, D_MODEL, D_MODEL), D_MODEL ** -0.5),
        'norm_mix_post': 1.0 + nrm(ks[15], (DEPTH, D_MODEL), 0.02),
        'norm_mlp_pre': 1.0 + nrm(ks[16], (DEPTH, D_MODEL), 0.02),
        'w_up': nrm(ks[17], (DEPTH, D_MODEL, D_FF), D_MODEL ** -0.5),
        'w_down': nrm(ks[18], (DEPTH, D_FF, D_MODEL), D_FF ** -0.5),
        'norm_mlp_post': 1.0 + nrm(ks[19], (DEPTH, D_MODEL), 0.02),
    }


def reference(x_prompt, x_sample, norm_mix_pre, w_in, ssm_a_re, ssm_a_im, ssm_log_step,
              ssm_b_re, ssm_b_im, ssm_c_re, ssm_c_im, ssm_d, w_glu, b_glu, w_out,
              norm_mix_post, norm_mlp_pre, w_up, w_down, norm_mlp_post):
    def run_trunk(x):
        for l in range(DEPTH):
            x = encoder_layer(x, norm_mix_pre[l], w_in[l], ssm_a_re[l], ssm_a_im[l], ssm_log_step[l],
                              ssm_b_re[l], ssm_b_im[l], ssm_c_re[l], ssm_c_im[l], ssm_d[l],
                              w_glu[l], b_glu[l], w_out[l], norm_mix_post[l], norm_mlp_pre[l],
                              w_up[l], w_down[l], norm_mlp_post[l])
        return x

    y_prompt = run_trunk(x_prompt)
    y_sample = run_trunk(x_sample)
    return (y_prompt, y_sample)
```

```python
import functools
import math

import jax
import jax.numpy as jnp
from jax import lax
from jax.experimental import pallas as pl
from jax.experimental.pallas import tpu as pltpu

F32 = jnp.float32
BF16 = jnp.bfloat16

D_MODEL = 1024
ATT_WIDTH = 512
SSM_WIDTH = 512
HEAD_DIM = 64
N_HEADS = 8
DILATIONS = ((128, 1), (512, 4), (2048, 16))
RADIUS = 64
SSM_GROUP = 16
N_GROUPS = 32
SSM_STATE = 64
D_FF = 4096
IN_WIDTH = 2048
ROPE_THETA = 10000.0
NORM_EPS = 1e-6
MASK_VALUE = -1e30

CHUNK = 16
N_SEG = 8
LANES = 128
VMEM_LIMIT = 56 * 1024 * 1024


def _cparams(sem):
    return pltpu.CompilerParams(dimension_semantics=sem, vmem_limit_bytes=VMEM_LIMIT)


def _inproj_kernel(x_ref, g_ref, w_ref, cos_ref, sin_ref, q_ref, k_ref, v_ref, u_ref, ub_ref):
    x = x_ref[...]
    ms = jnp.mean(x * x, axis=-1, keepdims=True)
    h = (x * lax.rsqrt(ms + NORM_EPS)) * g_ref[...]
    proj = jnp.dot(h.astype(BF16), w_ref[...], preferred_element_type=F32)
    cos = cos_ref[...]
    sin = sin_ref[...]
    lane = lax.broadcasted_iota(jnp.int32, cos.shape, 1)
    first_half = (lane & (HEAD_DIM // 2)) == 0

    def rot(t):
        partner = jnp.where(first_half, pltpu.roll(t, LANES - HEAD_DIM // 2, 1),
                            pltpu.roll(t, HEAD_DIM // 2, 1))
        return t * cos + partner * sin

    for c in range(ATT_WIDTH // LANES):
        sl = slice(c * LANES, (c + 1) * LANES)
        q_ref[:, sl] = (rot(proj[:, sl]) * (HEAD_DIM ** -0.5)).astype(BF16)
        k_ref[:, sl] = rot(proj[:, ATT_WIDTH + c * LANES:ATT_WIDTH + (c + 1) * LANES]).astype(BF16)
    v_ref[...] = proj[:, 2 * ATT_WIDTH:3 * ATT_WIDTH].astype(BF16)
    u = proj[:, 3 * ATT_WIDTH:]
    u_ref[...] = u
    ub_ref[...] = u.astype(BF16)


def _inproj(x2, g, w_bf, cos_t, sin_t, seq, tile):
    n = x2.shape[0]
    n_pos = seq // tile
    tok = lambda i: (i, 0)
    pos = lambda i: (i % n_pos, 0)
    const = lambda i: (0, 0)
    return pl.pallas_call(
        _inproj_kernel,
        grid=(n // tile,),
        in_specs=[
            pl.BlockSpec((tile, D_MODEL), tok),
            pl.BlockSpec((1, D_MODEL), const),
            pl.BlockSpec((D_MODEL, IN_WIDTH), const),
            pl.BlockSpec((tile, LANES), pos),
            pl.BlockSpec((tile, LANES), pos),
        ],
        out_specs=[pl.BlockSpec((tile, ATT_WIDTH), tok)] * 5,
        out_shape=[jax.ShapeDtypeStruct((n, ATT_WIDTH), BF16)] * 3
        + [jax.ShapeDtypeStruct((n, SSM_WIDTH), F32), jax.ShapeDtypeStruct((n, SSM_WIDTH), BF16)],
        compiler_params=_cparams(("arbitrary",)),
        name="inproj",
    )(x2, g, w_bf, cos_t, sin_t)


def _attn_kernel(q_ref, kp_ref, kc_ref, kn_ref, vp_ref, vc_ref, vn_ref, o_ref, lse_ref,
                 kx_ref, vx_ref, *, tm, sub_len):
    j = pl.program_id(2)
    kx_ref[0:RADIUS] = kp_ref[...]
    kx_ref[RADIUS:RADIUS + tm] = kc_ref[...]
    kx_ref[RADIUS + tm:] = kn_ref[...]
    vx_ref[0:RADIUS] = vp_ref[...]
    vx_ref[RADIUS:RADIUS + tm] = vc_ref[...]
    vx_ref[RADIUS + tm:] = vn_ref[...]

    nk = 3 * RADIUS
    rows = 2 * RADIUS
    qi = lax.broadcasted_iota(jnp.int32, (rows, nk), 0) & (RADIUS - 1)
    kc = lax.broadcasted_iota(jnp.int32, (rows, nk), 1)
    band = (kc >= qi) & (kc <= qi + 2 * RADIUS)
    lane = lax.broadcasted_iota(jnp.int32, (RADIUS, LANES), 1)
    head_a = lane < HEAD_DIM
    ones = jnp.ones((nk, LANES), BF16)

    for sb in range(tm // RADIUS):
        o = sb * RADIUS
        kidx = kc + (j * tm + o - RADIUS)
        mask = band & (kidx >= 0) & (kidx < sub_len)
        for p in range(N_HEADS // 2):
            sl = slice(p * LANES, (p + 1) * LANES)
            q2 = q_ref[o:o + RADIUS, sl]
            zero = jnp.zeros_like(q2)
            qs = jnp.concatenate([jnp.where(head_a, q2, zero), jnp.where(head_a, zero, q2)], axis=0)
            kw = kx_ref[o:o + nk, sl]
            s = lax.dot_general(qs, kw, (((1,), (1,)), ((), ())), preferred_element_type=F32)
            s = jnp.where(mask, s, MASK_VALUE)
            m = jnp.max(s, axis=-1, keepdims=True)
            pe = jnp.exp(s - m).astype(BF16)
            vaug = jnp.concatenate([vx_ref[o:o + nk, sl], ones], axis=1)
            r = jnp.dot(pe, vaug, preferred_element_type=F32)
            den = r[:, LANES:]
            outn = r[:, :LANES] / den
            lse = m + jnp.log(den)
            o_ref[o:o + RADIUS, sl] = jnp.where(head_a, outn[:RADIUS], outn[RADIUS:]).astype(o_ref.dtype)
            lse_ref[o:o + RADIUS, sl] = jnp.where(head_a, lse[:RADIUS], lse[RADIUS:])


def _banded_attention(q, k, v, dil, tm):
    bsz, seq, _ = q.shape
    sub_len = seq // dil
    tm = min(tm, sub_len)
    assert sub_len % tm == 0 and tm % RADIUS == 0
    hb = tm // RADIUS
    n_halo = sub_len // RADIUS
    view = lambda t: t.reshape(bsz, sub_len, dil * ATT_WIDTH)
    cur = pl.BlockSpec((None, tm, ATT_WIDTH), lambda b, r, j: (b, j, r))
    prev = pl.BlockSpec((None, RADIUS, ATT_WIDTH), lambda b, r, j: (b, jnp.maximum(j * hb - 1, 0), r))
    nxt = pl.BlockSpec((None, RADIUS, ATT_WIDTH),
                       lambda b, r, j: (b, jnp.minimum((j + 1) * hb, n_halo - 1), r))
    qv, kv, vv = view(q), view(k), view(v)
    out, lse = pl.pallas_call(
        functools.partial(_attn_kernel, tm=tm, sub_len=sub_len),
        grid=(bsz, dil, sub_len // tm),
        in_specs=[cur, prev, cur, nxt, prev, cur, nxt],
        out_specs=[cur, cur],
        out_shape=[jax.ShapeDtypeStruct(qv.shape, BF16), jax.ShapeDtypeStruct(qv.shape, F32)],
        scratch_shapes=[pltpu.VMEM((tm + 2 * RADIUS, ATT_WIDTH), BF16)] * 2,
        compiler_params=_cparams(("arbitrary",) * 3),
        name=f"attn_d{dil}",
    )(qv, kv, kv, kv, vv, vv, vv)
    return out.reshape(bsz, seq, ATT_WIDTH), lse.reshape(bsz, seq, ATT_WIDTH)


def _s5_kernel(ug_ref, w1_ref, cs_ref, pw_ref, dec_ref, y_ref, x_ref, h_ref, *, n_i):
    half = LANES // 2
    z = jnp.dot(ug_ref[...], w1_ref[...], preferred_element_type=F32)
    y_ref[...] = z[:, :2 * LANES]
    x_ref[...] = z[:, 2 * LANES:]
    lane = lax.broadcasted_iota(jnp.int32, (N_SEG, LANES), 1)
    fwd = lane < half
    a_re = jnp.broadcast_to(dec_ref[0:1, :LANES], (N_SEG, LANES))
    a_im = jnp.broadcast_to(dec_ref[0:1, LANES:], (N_SEG, LANES))

    def scan_body(s, carry):
        hre, him = carry
        rf = pl.multiple_of(s * N_SEG, N_SEG)
        rb = pl.multiple_of((n_i - 1 - s) * N_SEG, N_SEG)
        h_ref[pl.ds(rf, N_SEG), 0:half] = hre[:, :half]
        h_ref[pl.ds(rb, N_SEG), half:LANES] = hre[:, half:]
        h_ref[pl.ds(rf, N_SEG), LANES:LANES + half] = him[:, :half]
        h_ref[pl.ds(rb, N_SEG), LANES + half:] = him[:, half:]
        xf = x_ref[pl.ds(rf, N_SEG), :]
        xb = x_ref[pl.ds(rb, N_SEG), :]
        xre = jnp.where(fwd, xf[:, :LANES], xb[:, :LANES])
        xim = jnp.where(fwd, xf[:, LANES:], xb[:, LANES:])
        return a_re * hre - a_im * him + xre, a_re * him + a_im * hre + xim

    zero = jnp.zeros((N_SEG, LANES), F32)
    l_re, l_im = lax.fori_loop(0, n_i, scan_body, (zero, zero))

    lane1 = lax.broadcasted_iota(jnp.int32, (1, LANES), 1)
    fwd1 = lane1 < half
    s_re = dec_ref[1:2, :LANES]
    s_im = dec_ref[1:2, LANES:]
    e_re = jnp.zeros((1, LANES), F32)
    e_im = jnp.zeros((1, LANES), F32)
    ins = []
    for s in range(N_SEG):
        ins.append((e_re, e_im))
        lre = jnp.where(fwd1, l_re[s:s + 1], l_re[N_SEG - 1 - s:N_SEG - s])
        lim = jnp.where(fwd1, l_im[s:s + 1], l_im[N_SEG - 1 - s:N_SEG - s])
        e_re, e_im = s_re * e_re - s_im * e_im + lre, s_re * e_im + s_im * e_re + lim
    ein_re = jnp.concatenate(
        [jnp.where(fwd1, ins[s][0], ins[N_SEG - 1 - s][0]) for s in range(N_SEG)], axis=0)
    ein_im = jnp.concatenate(
        [jnp.where(fwd1, ins[s][1], ins[N_SEG - 1 - s][1]) for s in range(N_SEG)], axis=0)

    def fix_body(ib, c):
        pw = pw_ref[pl.ds(pl.multiple_of(ib * 8, 8), 8), :]
        for r in range(8):
            rows = pl.ds(pl.multiple_of((ib * 8 + r) * N_SEG, N_SEG), N_SEG)
            p_re = pw[r:r + 1, :LANES]
            p_im = pw[r:r + 1, LANES:]
            h_ref[rows, :LANES] = h_ref[rows, :LANES] + (p_re * ein_re - p_im * ein_im)
            h_ref[rows, LANES:] = h_ref[rows, LANES:] + (p_re * ein_im + p_im * ein_re)
        return c

    lax.fori_loop(0, n_i // 8, fix_body, 0)
    y_ref[...] += jnp.dot(h_ref[...].astype(BF16), cs_ref[...], preferred_element_type=F32)


def _s5_scan(ug, w1, cs, pw, dec):
    bsz, n_g, rows, width = ug.shape
    n_i = rows // N_SEG
    grp = lambda b, g: (g, 0, 0)
    return pl.pallas_call(
        functools.partial(_s5_kernel, n_i=n_i),
        grid=(bsz, n_g),
        in_specs=[
            pl.BlockSpec((None, None, rows, width), lambda b, g: (b, g, 0, 0)),
            pl.BlockSpec((None, width, 2 * width), grp),
            pl.BlockSpec((None, width, width), grp),
            pl.BlockSpec((None, n_i, width), grp),
            pl.BlockSpec((None, 2, width), grp),
        ],
        out_specs=pl.BlockSpec((None, None, rows, width), lambda b, g: (b, g, 0, 0)),
        out_shape=jax.ShapeDtypeStruct((bsz, n_g, rows, width), F32),
        scratch_shapes=[pltpu.VMEM((rows, width), F32)] * 2,
        compiler_params=_cparams(("arbitrary",) * 2),
        name="s5_scan",
    )(ug, w1, cs, pw, dec)


def _s5_operators(a_re, a_im, log_step, b_re, b_im, c_re, c_im, n_i):
    lam = lax.complex(a_re.astype(F32), a_im.astype(F32))
    step = jnp.exp(log_step.astype(F32))[..., None]
    a_bar = jnp.exp(lam * step)
    b_bar = ((a_bar - 1.0) / lam)[..., None] * lax.complex(b_re.astype(F32), b_im.astype(F32))
    c_mat = lax.complex(c_re.astype(F32), c_im.astype(F32))

    def powers(base, n):
        reps = jnp.broadcast_to(base, (n - 1,) + base.shape)
        return jnp.concatenate([jnp.ones_like(base)[None], jnp.cumprod(reps, axis=0)], axis=0)

    ap = powers(a_bar, CHUNK + 1)
    a_chunk = ap[CHUNK]
    kern = jnp.real(jnp.einsum('dgcp,tdgp,dgpe->dgtce', c_mat, ap[:CHUNK], b_bar))
    t_idx = jnp.arange(CHUNK)
    lag = t_idx[None, :] - t_idx[:, None]
    kf = kern[0][:, jnp.clip(lag, 0, CHUNK - 1)] * (lag >= 0)[None, :, :, None, None]
    kb = kern[1][:, jnp.clip(-lag, 0, CHUNK - 1)] * (lag <= 0)[None, :, :, None, None]
    m = jnp.transpose(kf + kb, (0, 1, 4, 2, 3)).reshape(N_GROUPS, CHUNK * SSM_GROUP, CHUNK * SSM_GROUP)

    bs_f = jnp.einsum('sgp,gpe->gsep', ap[:CHUNK][::-1, 0], b_bar[0])
    bs_b = jnp.einsum('sgp,gpe->gsep', ap[:CHUNK][:, 1], b_bar[1])
    flat = lambda t: t.reshape(N_GROUPS, CHUNK * SSM_GROUP, SSM_STATE)
    bs = jnp.concatenate([flat(jnp.real(bs_f)), flat(jnp.real(bs_b)),
                          flat(jnp.imag(bs_f)), flat(jnp.imag(bs_b))], axis=-1)
    w1 = jnp.concatenate([m, bs], axis=-1).astype(BF16)

    co_f = jnp.einsum('gcp,tgp->gptc', c_mat[0], ap[1:CHUNK + 1, 0])
    co_b = jnp.einsum('gcp,tgp->gptc', c_mat[1], ap[1:CHUNK + 1][::-1, 1])
    flat2 = lambda t: t.reshape(N_GROUPS, SSM_STATE, CHUNK * SSM_GROUP)
    cs = jnp.concatenate([flat2(jnp.real(co_f)), flat2(jnp.real(co_b)),
                          -flat2(jnp.imag(co_f)), -flat2(jnp.imag(co_b))], axis=1).astype(BF16)

    pc = powers(a_chunk, n_i + 1)
    pf = jnp.transpose(pc[:n_i, 0], (1, 0, 2))
    pb = jnp.transpose(pc[:n_i][::-1, 1], (1, 0, 2))
    pw = jnp.concatenate([jnp.real(pf), jnp.real(pb), jnp.imag(pf), jnp.imag(pb)], axis=-1)
    a_seg = pc[n_i]
    row = lambda t: jnp.concatenate([jnp.real(t[0]), jnp.real(t[1]), jnp.imag(t[0]), jnp.imag(t[1])], axis=-1)
    dec = jnp.stack([row(a_chunk), row(a_seg)], axis=1)
    return w1, cs, pw.astype(F32), dec.astype(F32)


def _rms(x, g):
    return (x * lax.rsqrt(jnp.mean(x * x, axis=-1, keepdims=True) + NORM_EPS)) * g


def _mix_kernel(o1_ref, o4_ref, o16_ref, l1_ref, l4_ref, l16_ref, ys_ref, u_ref, x_ref,
                d_ref, wg_ref, bg_ref, wo_ref, g_ref, out_ref):
    l1, l4, l16 = l1_ref[...], l4_ref[...], l16_ref[...]
    mx = jnp.maximum(jnp.maximum(l1, l4), l16)
    e1, e4, e16 = jnp.exp(l1 - mx), jnp.exp(l4 - mx), jnp.exp(l16 - mx)
    att = (e1 * o1_ref[...].astype(F32) + e4 * o4_ref[...].astype(F32)
           + e16 * o16_ref[...].astype(F32)) / (e1 + e4 + e16)
    y = ys_ref[...] + d_ref[...] * u_ref[...]
    y = 0.5 * y * (1.0 + jnp.tanh(math.sqrt(2.0 / math.pi) * (y + 0.044715 * (y * y * y))))
    gate = jnp.dot(y.astype(BF16), wg_ref[...], preferred_element_type=F32) + bg_ref[...]
    ssm = y * (1.0 / (1.0 + jnp.exp(-gate)))
    mixed = (jnp.dot(att.astype(BF16), wo_ref[:ATT_WIDTH], preferred_element_type=F32)
             + jnp.dot(ssm.astype(BF16), wo_ref[ATT_WIDTH:], preferred_element_type=F32))
    out_ref[...] = x_ref[...] + _rms(mixed, g_ref[...])


def _mix(o1, o4, o16, l1, l4, l16, ys, u, x2, d, wg, bg, wo, g, tile):
    n = x2.shape[0]
    tok = lambda i: (i, 0)
    const = lambda i: (0, 0)
    half = pl.BlockSpec((tile, ATT_WIDTH), tok)
    return pl.pallas_call(
        _mix_kernel,
        grid=(n // tile,),
        in_specs=[half] * 8 + [
            pl.BlockSpec((tile, D_MODEL), tok),
            pl.BlockSpec((1, SSM_WIDTH), const),
            pl.BlockSpec((SSM_WIDTH, SSM_WIDTH), const),
            pl.BlockSpec((1, SSM_WIDTH), const),
            pl.BlockSpec((D_MODEL, D_MODEL), const),
            pl.BlockSpec((1, D_MODEL), const),
        ],
        out_specs=pl.BlockSpec((tile, D_MODEL), tok),
        out_shape=jax.ShapeDtypeStruct((n, D_MODEL), F32),
        compiler_params=_cparams(("arbitrary",)),
        name="mix",
    )(o1, o4, o16, l1, l4, l16, ys, u, x2, d, wg, bg, wo, g)


def _mlp_kernel(x_ref, gpre_ref, wu_ref, wd_ref, gpost_ref, out_ref, *, ff_chunk):
    x = x_ref[...]
    h = _rms(x, gpre_ref[...]).astype(BF16)
    acc = jnp.zeros(x.shape, F32)
    for f in range(D_FF // ff_chunk):
        sl = slice(f * ff_chunk, (f + 1) * ff_chunk)
        a = jnp.maximum(jnp.dot(h, wu_ref[:, sl], preferred_element_type=F32), 0.0)
        acc = acc + jnp.dot((a * a).astype(BF16), wd_ref[sl, :], preferred_element_type=F32)
    out_ref[...] = x + _rms(acc, gpost_ref[...])


def _mlp(x2, gpre, wu, wd, gpost, tile, ff_chunk):
    n = x2.shape[0]
    tok = lambda i: (i, 0)
    const = lambda i: (0, 0)
    return pl.pallas_call(
        functools.partial(_mlp_kernel, ff_chunk=ff_chunk),
        grid=(n // tile,),
        in_specs=[
            pl.BlockSpec((tile, D_MODEL), tok),
            pl.BlockSpec((1, D_MODEL), const),
            pl.BlockSpec((D_MODEL, D_FF), const),
            pl.BlockSpec((D_FF, D_MODEL), const),
            pl.BlockSpec((1, D_MODEL), const),
        ],
        out_specs=pl.BlockSpec((tile, D_MODEL), tok),
        out_shape=jax.ShapeDtypeStruct((n, D_MODEL), F32),
        compiler_params=_cparams(("arbitrary",)),
        name="mlp",
    )(x2, gpre, wu, wd, gpost)


def _rotary_tables(seq):
    half = HEAD_DIM // 2
    inv_freq = 1.0 / (ROPE_THETA ** (jnp.arange(half, dtype=F32) / half))
    ang = jnp.arange(seq, dtype=F32)[:, None] * inv_freq[None, :]
    cos, sin = jnp.cos(ang), jnp.sin(ang)
    return jnp.tile(cos, (1, 4)), jnp.tile(jnp.concatenate([-sin, sin], axis=-1), (1, 2))


def _layer(x, tables, s5_ops, g_mix_pre, w_in, d_skip, w_glu, b_glu, w_out, g_mix_post,
           g_mlp_pre, w_up, w_down, g_mlp_post, tok_tile=512, attn_tile=256):
    bsz, seq, _ = x.shape
    n = bsz * seq
    x2 = x.reshape(n, D_MODEL)
    cos_t, sin_t = tables
    w1, cs, pw, dec = s5_ops
    row = lambda t: t.reshape(1, -1).astype(F32)

    q, k, v, u, ub = _inproj(x2, row(g_mix_pre), w_in.astype(BF16), cos_t, sin_t, seq, tok_tile)
    as3 = lambda t: t.reshape(bsz, seq, ATT_WIDTH)
    outs = [_banded_attention(as3(q), as3(k), as3(v), dil, attn_tile) for _, dil in DILATIONS]

    n_i = seq // (CHUNK * N_SEG)
    ug = ub.reshape(bsz, N_SEG, n_i, CHUNK, N_GROUPS, SSM_GROUP)
    ug = jnp.transpose(ug, (0, 4, 2, 1, 3, 5)).reshape(bsz, N_GROUPS, n_i * N_SEG, CHUNK * SSM_GROUP)
    yg = _s5_scan(ug, w1, cs, pw, dec)
    ys = yg.reshape(bsz, N_GROUPS, n_i, N_SEG, CHUNK, SSM_GROUP)
    ys = jnp.transpose(ys, (0, 3, 2, 4, 1, 5)).reshape(n, SSM_WIDTH)

    flat = lambda t: t.reshape(n, ATT_WIDTH)
    x1 = _mix(flat(outs[0][0]), flat(outs[1][0]), flat(outs[2][0]),
              flat(outs[0][1]), flat(outs[1][1]), flat(outs[2][1]),
              ys, u, x2, row(d_skip), w_glu.astype(BF16), row(b_glu), w_out.astype(BF16),
              row(g_mix_post), tok_tile)
    y = _mlp(x1, row(g_mlp_pre), w_up.astype(BF16), w_down.astype(BF16), row(g_mlp_post),
             tok_tile, 1024)
    return y.reshape(bsz, seq, D_MODEL)


def kernel(x_prompt, x_sample, norm_mix_pre, w_in, ssm_a_re, ssm_a_im, ssm_log_step, ssm_b_re, ssm_b_im,
           ssm_c_re, ssm_c_im, ssm_d, w_glu, b_glu, w_out, norm_mix_post, norm_mlp_pre, w_up, w_down,
           norm_mlp_post):
    def run(x):
        seq = x.shape[1]
        for l in range(norm_mix_pre.shape[0]):
            ops = _s5_operators(ssm_a_re[l], ssm_a_im[l], ssm_log_step[l], ssm_b_re[l], ssm_b_im[l],
                                ssm_c_re[l], ssm_c_im[l], seq // (CHUNK * N_SEG))
            x = _layer(x, _rotary_tables(seq), ops, norm_mix_pre[l], w_in[l], ssm_d[l], w_glu[l],
                       b_glu[l], w_out[l], norm_mix_post[l], norm_mlp_pre[l], w_up[l], w_down[l],
                       norm_mlp_post[l])
        return x

    return run(x_prompt), run(x_sample)
```

```python
import functools
import math

import numpy as np
import jax
import jax.numpy as jnp
from jax import lax
from jax.experimental import pallas as pl
from jax.experimental.pallas import tpu as pltpu

F32 = jnp.float32
BF16 = jnp.bfloat16

D_MODEL = 1024
ATT_WIDTH = 512
SSM_WIDTH = 512
HEAD_DIM = 64
N_HEADS = 8
DILATIONS = ((128, 1), (512, 4), (2048, 16))
RADIUS = 64
SSM_GROUP = 16
N_GROUPS = 32
SSM_STATE = 64
D_FF = 4096
IN_WIDTH = 2048
ROPE_THETA = 10000.0
NORM_EPS = 1e-6
MASK_VALUE = -1e30

CHUNK = 16
N_SEG = 8
LANES = 128
SLOTS = LANES // SSM_GROUP
CW = CHUNK * SSM_GROUP
VMEM_LIMIT = 56 * 1024 * 1024
HI = lax.Precision.HIGHEST


def _cparams(sem):
    return pltpu.CompilerParams(dimension_semantics=sem, vmem_limit_bytes=VMEM_LIMIT)


def _slot_masks():
    lane = lax.broadcasted_iota(jnp.int32, (1, LANES), 1)
    return [(lane // SSM_GROUP) == s for s in range(SLOTS)]


def _step_of_slot():
    gam = np.arange(SLOTS)[:, None, None]
    hh = np.arange(CHUNK // SLOTS)[None, :, None]
    l = np.arange(SLOTS)[None, None, :]
    return (SLOTS * hh + (l - gam) % SLOTS).reshape(SLOTS, CHUNK)


def _inproj_kernel(x_ref, g_ref, w_ref, cos_ref, sin_ref, q_ref, k_ref, v_ref, u_ref, ug_ref, us_ref):
    x = x_ref[...]
    ms = jnp.mean(x * x, axis=-1, keepdims=True)
    h = (x * lax.rsqrt(ms + NORM_EPS)) * g_ref[...]
    proj = jnp.dot(h.astype(BF16), w_ref[...], preferred_element_type=F32)
    cos = cos_ref[...]
    sin = sin_ref[...]
    lane = lax.broadcasted_iota(jnp.int32, cos.shape, 1)
    first_half = (lane & (HEAD_DIM // 2)) == 0

    def rot(t):
        partner = jnp.where(first_half, pltpu.roll(t, LANES - HEAD_DIM // 2, 1),
                            pltpu.roll(t, HEAD_DIM // 2, 1))
        return t * cos + partner * sin

    for c in range(ATT_WIDTH // LANES):
        sl = slice(c * LANES, (c + 1) * LANES)
        q_ref[:, sl] = (rot(proj[:, sl]) * (HEAD_DIM ** -0.5)).astype(BF16)
        k_ref[:, sl] = rot(proj[:, ATT_WIDTH + c * LANES:ATT_WIDTH + (c + 1) * LANES]).astype(BF16)
    v_ref[...] = proj[:, 2 * ATT_WIDTH:3 * ATT_WIDTH].astype(BF16)
    u = proj[:, 3 * ATT_WIDTH:]
    u_ref[...] = u

    n_chunk = x.shape[0] // CHUNK
    masks = _slot_masks()
    for qt in range(SSM_WIDTH // LANES):
        us_ref[qt] = u[:, qt * LANES:(qt + 1) * LANES]
    for qt in range(SSM_WIDTH // LANES):
        for hh in range(CHUNK // SLOTS):
            rolled = []
            for t8 in range(SLOTS):
                step_rows = us_ref[qt, pl.ds(SLOTS * hh + t8, n_chunk, stride=CHUNK), :]
                rolled.append(step_rows if t8 == 0 else pltpu.roll(step_rows, SSM_GROUP * t8, 1))
            for gam in range(SLOTS):
                res = rolled[SLOTS - 1]
                for t8 in range(SLOTS - 2, -1, -1):
                    res = jnp.where(masks[(gam + t8) % SLOTS], rolled[t8], res)
                ug_ref[SLOTS * qt + gam, :, hh * LANES:(hh + 1) * LANES] = res.astype(BF16)


def _inproj(x2, g, w_bf, cos_t, sin_t, seq, tile):
    n = x2.shape[0]
    n_pos = seq // tile
    tok = lambda i: (i, 0)
    pos = lambda i: (i % n_pos, 0)
    const = lambda i: (0, 0)
    return pl.pallas_call(
        _inproj_kernel,
        grid=(n // tile,),
        in_specs=[
            pl.BlockSpec((tile, D_MODEL), tok),
            pl.BlockSpec((1, D_MODEL), const),
            pl.BlockSpec((D_MODEL, IN_WIDTH), const),
            pl.BlockSpec((tile, LANES), pos),
            pl.BlockSpec((tile, LANES), pos),
        ],
        out_specs=[pl.BlockSpec((tile, ATT_WIDTH), tok)] * 4
        + [pl.BlockSpec((N_GROUPS, tile // CHUNK, CW), lambda i: (0, i, 0))],
        out_shape=[jax.ShapeDtypeStruct((n, ATT_WIDTH), BF16)] * 3
        + [jax.ShapeDtypeStruct((n, SSM_WIDTH), F32),
           jax.ShapeDtypeStruct((N_GROUPS, n // CHUNK, CW), BF16)],
        scratch_shapes=[pltpu.VMEM((SSM_WIDTH // LANES, tile, LANES), F32)],
        compiler_params=_cparams(("arbitrary",)),
        name="inproj",
    )(x2, g, w_bf, cos_t, sin_t)


def _attn_kernel(q_ref, kp_ref, kc_ref, kn_ref, vp_ref, vc_ref, vn_ref, o_ref, lse_ref,
                 kx_ref, vx_ref, *, tm, sub_len):
    j = pl.program_id(2)
    kx_ref[0:RADIUS] = kp_ref[...]
    kx_ref[RADIUS:RADIUS + tm] = kc_ref[...]
    kx_ref[RADIUS + tm:] = kn_ref[...]
    vx_ref[0:RADIUS] = vp_ref[...]
    vx_ref[RADIUS:RADIUS + tm] = vc_ref[...]
    vx_ref[RADIUS + tm:] = vn_ref[...]

    nk = 3 * RADIUS
    rows = 2 * RADIUS
    qi = lax.broadcasted_iota(jnp.int32, (rows, nk), 0) & (RADIUS - 1)
    kc = lax.broadcasted_iota(jnp.int32, (rows, nk), 1)
    band = (kc >= qi) & (kc <= qi + 2 * RADIUS)
    lane = lax.broadcasted_iota(jnp.int32, (RADIUS, LANES), 1)
    head_a = lane < HEAD_DIM
    ones = jnp.ones((nk, LANES), BF16)

    for sb in range(tm // RADIUS):
        o = sb * RADIUS
        kidx = kc + (j * tm + o - RADIUS)
        mask = band & (kidx >= 0) & (kidx < sub_len)
        for p in range(N_HEADS // 2):
            sl = slice(p * LANES, (p + 1) * LANES)
            q2 = q_ref[o:o + RADIUS, sl]
            zero = jnp.zeros_like(q2)
            qs = jnp.concatenate([jnp.where(head_a, q2, zero), jnp.where(head_a, zero, q2)], axis=0)
            kw = kx_ref[o:o + nk, sl]
            s = lax.dot_general(qs, kw, (((1,), (1,)), ((), ())), preferred_element_type=F32)
            s = jnp.where(mask, s, MASK_VALUE)
            m = jnp.max(s, axis=-1, keepdims=True)
            pe = jnp.exp(s - m).astype(BF16)
            vaug = jnp.concatenate([vx_ref[o:o + nk, sl], ones], axis=1)
            r = jnp.dot(pe, vaug, preferred_element_type=F32)
            den = r[:, LANES:]
            outn = r[:, :LANES] / den
            lse = m + jnp.log(den)
            o_ref[o:o + RADIUS, sl] = jnp.where(head_a, outn[:RADIUS], outn[RADIUS:]).astype(o_ref.dtype)
            lse_ref[o:o + RADIUS, sl] = jnp.where(head_a, lse[:RADIUS], lse[RADIUS:])


def _banded_attention(q, k, v, dil, tm):
    bsz, seq, _ = q.shape
    sub_len = seq // dil
    tm = min(tm, sub_len)
    assert sub_len % tm == 0 and tm % RADIUS == 0
    hb = tm // RADIUS
    n_halo = sub_len // RADIUS
    view = lambda t: t.reshape(bsz, sub_len, dil * ATT_WIDTH)
    cur = pl.BlockSpec((None, tm, ATT_WIDTH), lambda b, r, j: (b, j, r))
    prev = pl.BlockSpec((None, RADIUS, ATT_WIDTH), lambda b, r, j: (b, jnp.maximum(j * hb - 1, 0), r))
    nxt = pl.BlockSpec((None, RADIUS, ATT_WIDTH),
                       lambda b, r, j: (b, jnp.minimum((j + 1) * hb, n_halo - 1), r))
    qv, kv, vv = view(q), view(k), view(v)
    out, lse = pl.pallas_call(
        functools.partial(_attn_kernel, tm=tm, sub_len=sub_len),
        grid=(bsz, dil, sub_len // tm),
        in_specs=[cur, prev, cur, nxt, prev, cur, nxt],
        out_specs=[cur, cur],
        out_shape=[jax.ShapeDtypeStruct(qv.shape, BF16), jax.ShapeDtypeStruct(qv.shape, F32)],
        scratch_shapes=[pltpu.VMEM((tm + 2 * RADIUS, ATT_WIDTH), BF16)] * 2,
        compiler_params=_cparams(("arbitrary",) * 3),
        name=f"attn_d{dil}",
    )(qv, kv, kv, kv, vv, vv, vv)
    return out.reshape(bsz, seq, ATT_WIDTH), lse.reshape(bsz, seq, ATT_WIDTH)


def _s5_kernel(ug_ref, w1_ref, cs_ref, pw_ref, dec_ref, y_ref, x_ref, h_ref, *, n_i):
    half = LANES // 2
    z = jnp.dot(ug_ref[...], w1_ref[...], preferred_element_type=F32)
    y_ref[...] = z[:, :CW]
    for seg in range(N_SEG):
        rows = slice(seg * n_i, (seg + 1) * n_i)
        x_ref[0, pl.ds(seg, n_i, stride=N_SEG), :] = z[rows, CW:CW + LANES]
        x_ref[1, pl.ds(seg, n_i, stride=N_SEG), :] = z[rows, CW + LANES:]
    lane = lax.broadcasted_iota(jnp.int32, (N_SEG, LANES), 1)
    fwd = lane < half
    a_re = jnp.broadcast_to(dec_ref[0:1, :LANES], (N_SEG, LANES))
    a_im = jnp.broadcast_to(dec_ref[0:1, LANES:], (N_SEG, LANES))

    def scan_body(s, carry):
        hre, him = carry
        rf = pl.ds(pl.multiple_of(s * N_SEG, N_SEG), N_SEG)
        rb = pl.ds(pl.multiple_of((n_i - 1 - s) * N_SEG, N_SEG), N_SEG)
        h_ref[0, rf, 0:half] = hre[:, :half]
        h_ref[0, rb, half:] = hre[:, half:]
        h_ref[1, rf, 0:half] = him[:, :half]
        h_ref[1, rb, half:] = him[:, half:]
        xre = jnp.where(fwd, x_ref[0, rf, :], x_ref[0, rb, :])
        xim = jnp.where(fwd, x_ref[1, rf, :], x_ref[1, rb, :])
        return a_re * hre - a_im * him + xre, a_re * him + a_im * hre + xim

    zero = jnp.zeros((N_SEG, LANES), F32)
    l_re, l_im = lax.fori_loop(0, n_i, scan_body, (zero, zero))

    lane1 = lax.broadcasted_iota(jnp.int32, (1, LANES), 1)
    fwd1 = lane1 < half
    s_re = dec_ref[1:2, :LANES]
    s_im = dec_ref[1:2, LANES:]
    e_re = jnp.zeros((1, LANES), F32)
    e_im = jnp.zeros((1, LANES), F32)
    ins = []
    for s in range(N_SEG):
        ins.append((e_re, e_im))
        lre = jnp.where(fwd1, l_re[s:s + 1], l_re[N_SEG - 1 - s:N_SEG - s])
        lim = jnp.where(fwd1, l_im[s:s + 1], l_im[N_SEG - 1 - s:N_SEG - s])
        e_re, e_im = s_re * e_re - s_im * e_im + lre, s_re * e_im + s_im * e_re + lim
    ein_re = jnp.concatenate(
        [jnp.where(fwd1, ins[s][0], ins[N_SEG - 1 - s][0]) for s in range(N_SEG)], axis=0)
    ein_im = jnp.concatenate(
        [jnp.where(fwd1, ins[s][1], ins[N_SEG - 1 - s][1]) for s in range(N_SEG)], axis=0)

    def fix_body(ib, c):
        pw = pw_ref[pl.ds(pl.multiple_of(ib * 8, 8), 8), :]
        for r in range(8):
            rows = pl.ds(pl.multiple_of((ib * 8 + r) * N_SEG, N_SEG), N_SEG)
            p_re = pw[r:r + 1, :LANES]
            p_im = pw[r:r + 1, LANES:]
            h_ref[0, rows, :] = h_ref[0, rows, :] + (p_re * ein_re - p_im * ein_im)
            h_ref[1, rows, :] = h_ref[1, rows, :] + (p_re * ein_im + p_im * ein_re)
        return c

    lax.fori_loop(0, n_i // 8, fix_body, 0)
    ungroup = lambda c: jnp.concatenate(
        [h_ref[c, pl.ds(seg, n_i, stride=N_SEG), :] for seg in range(N_SEG)], axis=0)
    hcat = jnp.concatenate([ungroup(0), ungroup(1)], axis=1).astype(BF16)
    y_ref[...] += jnp.dot(hcat, cs_ref[...], preferred_element_type=F32)


def _s5_scan(ug, w1, cs, pw, dec, bsz):
    n_g, total_rows, width = ug.shape
    rows = total_rows // bsz
    n_i = rows // N_SEG
    grp = lambda b, g: (g, 0, 0)
    seq_rows = pl.BlockSpec((None, rows, width), lambda b, g: (g, b, 0))
    return pl.pallas_call(
        functools.partial(_s5_kernel, n_i=n_i),
        grid=(bsz, n_g),
        in_specs=[
            seq_rows,
            pl.BlockSpec((None, width, 2 * width), grp),
            pl.BlockSpec((None, width, width), grp),
            pl.BlockSpec((None, n_i, width), grp),
            pl.BlockSpec((None, 2, width), grp),
        ],
        out_specs=seq_rows,
        out_shape=jax.ShapeDtypeStruct((n_g, total_rows, width), F32),
        scratch_shapes=[pltpu.VMEM((2, rows, LANES), F32)] * 2,
        compiler_params=_cparams(("arbitrary",) * 2),
        name="s5_scan",
    )(ug, w1, cs, pw, dec)


def _s5_operators(a_re, a_im, log_step, b_re, b_im, c_re, c_im, n_i):
    a_re, a_im, b_re, b_im, c_re, c_im = (t.astype(F32) for t in (a_re, a_im, b_re, b_im, c_re, c_im))
    step = jnp.exp(log_step.astype(F32))[..., None]
    zr, zi = a_re * step, a_im * step

    def power(n):
        n = jnp.asarray(n, F32).reshape(n.shape + (1, 1, 1))
        mag = jnp.exp(n * zr)
        return mag * jnp.cos(n * zi), mag * jnp.sin(n * zi)

    abr, abi = power(np.array(1))
    den = a_re * a_re + a_im * a_im
    fr = ((abr - 1.0) * a_re + abi * a_im) / den
    fi = (abi * a_re - (abr - 1.0) * a_im) / den
    bbr = fr[..., None] * b_re - fi[..., None] * b_im
    bbi = fr[..., None] * b_im + fi[..., None] * b_re

    pr, pi = power(np.arange(CHUNK + 1))
    wr = pr[:CHUNK, ..., None] * bbr - pi[:CHUNK, ..., None] * bbi
    wi = pr[:CHUNK, ..., None] * bbi + pi[:CHUNK, ..., None] * bbr
    kern = (jnp.einsum('dgcp,tdgpe->dgtce', c_re, wr, precision=HI)
            - jnp.einsum('dgcp,tdgpe->dgtce', c_im, wi, precision=HI))

    step_of = _step_of_slot()
    n_q = N_GROUPS // SLOTS
    s_step = step_of[:, :, None]
    t_step = step_of[:, None, :]
    tau = np.arange(CHUNK)[None, :, None, None]
    sel_f = (t_step - s_step)[:, None] == tau
    sel_b = (s_step - t_step)[:, None] == tau
    sel = jnp.asarray(np.concatenate([sel_f, sel_b], axis=1), F32)
    kx = jnp.transpose(kern, (1, 0, 2, 3, 4)).reshape(n_q, SLOTS, 2 * CHUNK, SSM_GROUP, SSM_GROUP)
    m = jnp.einsum('yxst,qyxce->qysetc', sel, kx, precision=HI).reshape(N_GROUPS, CW, CW)

    onehot = lambda idx: jnp.asarray(idx[..., None] == np.arange(CHUNK + 1), F32)
    grouped = lambda t: t.reshape((t.shape[0], n_q, SLOTS) + t.shape[2:])
    take = lambda oh, t: jnp.einsum('ypn,nqyk->qypk', oh, grouped(t), precision=HI)

    def state_in(d, idx):
        ar, ai = take(onehot(idx), pr[:, d]), take(onehot(idx), pi[:, d])
        br = jnp.swapaxes(grouped(bbr[d][None])[0], -1, -2)
        bi = jnp.swapaxes(grouped(bbi[d][None])[0], -1, -2)
        re = ar[:, :, :, None, :] * br[:, :, None] - ai[:, :, :, None, :] * bi[:, :, None]
        im = ar[:, :, :, None, :] * bi[:, :, None] + ai[:, :, :, None, :] * br[:, :, None]
        return re.reshape(N_GROUPS, CW, SSM_STATE), im.reshape(N_GROUPS, CW, SSM_STATE)

    bf_re, bf_im = state_in(0, CHUNK - 1 - step_of)
    bb_re, bb_im = state_in(1, step_of)
    w1 = jnp.concatenate([m, bf_re, bb_re, bf_im, bb_im], axis=-1).astype(BF16)

    def state_out(d, idx):
        ar, ai = take(onehot(idx), pr[:, d]), take(onehot(idx), pi[:, d])
        cr = jnp.swapaxes(grouped(c_re[d][None])[0], -1, -2)
        ci = jnp.swapaxes(grouped(c_im[d][None])[0], -1, -2)
        ar, ai = jnp.swapaxes(ar, -1, -2)[..., None], jnp.swapaxes(ai, -1, -2)[..., None]
        re = cr[:, :, :, None, :] * ar - ci[:, :, :, None, :] * ai
        im = cr[:, :, :, None, :] * ai + ci[:, :, :, None, :] * ar
        return re.reshape(N_GROUPS, SSM_STATE, CW), im.reshape(N_GROUPS, SSM_STATE, CW)

    cf_re, cf_im = state_out(0, step_of + 1)
    cb_re, cb_im = state_out(1, CHUNK - step_of)
    cs = jnp.concatenate([cf_re, cb_re, -cf_im, -cb_im], axis=1).astype(BF16)

    i_idx = np.arange(n_i)
    pf_re, pf_im = power(CHUNK * i_idx)
    pb_re, pb_im = power(CHUNK * (n_i - 1 - i_idx))
    pw = jnp.concatenate([pf_re[:, 0], pb_re[:, 1], pf_im[:, 0], pb_im[:, 1]], axis=-1)
    pw = jnp.transpose(pw, (1, 0, 2))
    dr, di = power(np.array([CHUNK, CHUNK * n_i]))
    dec = jnp.concatenate([dr[:, 0], dr[:, 1], di[:, 0], di[:, 1]], axis=-1)
    dec = jnp.transpose(dec, (1, 0, 2))
    return w1, cs, pw, dec


def _rms(x, g):
    return (x * lax.rsqrt(jnp.mean(x * x, axis=-1, keepdims=True) + NORM_EPS)) * g


def _mix_kernel(o1_ref, o4_ref, o16_ref, l1_ref, l4_ref, l16_ref, yg_ref, u_ref, x_ref,
                d_ref, wg_ref, bg_ref, wo_ref, g_ref, out_ref, ys_ref):
    n_chunk = x_ref.shape[0] // CHUNK
    masks = _slot_masks()
    for qt in range(SSM_WIDTH // LANES):
        for hh in range(CHUNK // SLOTS):
            src = [yg_ref[SLOTS * qt + gam, :, hh * LANES:(hh + 1) * LANES] for gam in range(SLOTS)]
            for t8 in range(SLOTS):
                pre = src[SLOTS - 1]
                for gam in range(SLOTS - 2, -1, -1):
                    pre = jnp.where(masks[(gam + t8) % SLOTS], src[gam], pre)
                nat = pre if t8 == 0 else pltpu.roll(pre, LANES - SSM_GROUP * t8, 1)
                ys_ref[qt, pl.ds(SLOTS * hh + t8, n_chunk, stride=CHUNK), :] = nat
    ys = jnp.concatenate([ys_ref[qt] for qt in range(SSM_WIDTH // LANES)], axis=1)

    l1, l4, l16 = l1_ref[...], l4_ref[...], l16_ref[...]
    mx = jnp.maximum(jnp.maximum(l1, l4), l16)
    e1, e4, e16 = jnp.exp(l1 - mx), jnp.exp(l4 - mx), jnp.exp(l16 - mx)
    att = (e1 * o1_ref[...].astype(F32) + e4 * o4_ref[...].astype(F32)
           + e16 * o16_ref[...].astype(F32)) / (e1 + e4 + e16)
    y = ys + d_ref[...] * u_ref[...]
    y = 0.5 * y * (1.0 + jnp.tanh(math.sqrt(2.0 / math.pi) * (y + 0.044715 * (y * y * y))))
    gate = jnp.dot(y.astype(BF16), wg_ref[...], preferred_element_type=F32) + bg_ref[...]
    ssm = y * (1.0 / (1.0 + jnp.exp(-gate)))
    mixed = (jnp.dot(att.astype(BF16), wo_ref[:ATT_WIDTH], preferred_element_type=F32)
             + jnp.dot(ssm.astype(BF16), wo_ref[ATT_WIDTH:], preferred_element_type=F32))
    out_ref[...] = x_ref[...] + _rms(mixed, g_ref[...])


def _mix(o1, o4, o16, l1, l4, l16, yg, u, x2, d, wg, bg, wo, g, tile):
    n = x2.shape[0]
    tok = lambda i: (i, 0)
    const = lambda i: (0, 0)
    half = pl.BlockSpec((tile, ATT_WIDTH), tok)
    return pl.pallas_call(
        _mix_kernel,
        grid=(n // tile,),
        in_specs=[half] * 6 + [
            pl.BlockSpec((N_GROUPS, tile // CHUNK, CW), lambda i: (0, i, 0)),
            half,
            pl.BlockSpec((tile, D_MODEL), tok),
            pl.BlockSpec((1, SSM_WIDTH), const),
            pl.BlockSpec((SSM_WIDTH, SSM_WIDTH), const),
            pl.BlockSpec((1, SSM_WIDTH), const),
            pl.BlockSpec((D_MODEL, D_MODEL), const),
            pl.BlockSpec((1, D_MODEL), const),
        ],
        out_specs=pl.BlockSpec((tile, D_MODEL), tok),
        out_shape=jax.ShapeDtypeStruct((n, D_MODEL), F32),
        scratch_shapes=[pltpu.VMEM((SSM_WIDTH // LANES, tile, LANES), F32)],
        compiler_params=_cparams(("arbitrary",)),
        name="mix",
    )(o1, o4, o16, l1, l4, l16, yg, u, x2, d, wg, bg, wo, g)


def _mlp_kernel(x_ref, gpre_ref, wu_ref, wd_ref, gpost_ref, out_ref, *, ff_chunk):
    x = x_ref[...]
    h = _rms(x, gpre_ref[...]).astype(BF16)
    acc = jnp.zeros(x.shape, F32)
    for f in range(D_FF // ff_chunk):
        sl = slice(f * ff_chunk, (f + 1) * ff_chunk)
        a = jnp.maximum(jnp.dot(h, wu_ref[:, sl], preferred_element_type=F32), 0.0)
        acc = acc + jnp.dot((a * a).astype(BF16), wd_ref[sl, :], preferred_element_type=F32)
    out_ref[...] = x + _rms(acc, gpost_ref[...])


def _mlp(x2, gpre, wu, wd, gpost, tile, ff_chunk):
    n = x2.shape[0]
    tok = lambda i: (i, 0)
    const = lambda i: (0, 0)
    return pl.pallas_call(
        functools.partial(_mlp_kernel, ff_chunk=ff_chunk),
        grid=(n // tile,),
        in_specs=[
            pl.BlockSpec((tile, D_MODEL), tok),
            pl.BlockSpec((1, D_MODEL), const),
            pl.BlockSpec((D_MODEL, D_FF), const),
            pl.BlockSpec((D_FF, D_MODEL), const),
            pl.BlockSpec((1, D_MODEL), const),
        ],
        out_specs=pl.BlockSpec((tile, D_MODEL), tok),
        out_shape=jax.ShapeDtypeStruct((n, D_MODEL), F32),
        compiler_params=_cparams(("arbitrary",)),
        name="mlp",
    )(x2, gpre, wu, wd, gpost)


def _rotary_tables(seq):
    half = HEAD_DIM // 2
    inv_freq = 1.0 / (ROPE_THETA ** (jnp.arange(half, dtype=F32) / half))
    ang = jnp.arange(seq, dtype=F32)[:, None] * inv_freq[None, :]
    cos, sin = jnp.cos(ang), jnp.sin(ang)
    return jnp.tile(cos, (1, 4)), jnp.tile(jnp.concatenate([-sin, sin], axis=-1), (1, 2))


def _prepare(seq, norm_mix_pre, w_in, a_re, a_im, log_step, b_re, b_im, c_re, c_im, d_skip, w_glu, b_glu,
             w_out, norm_mix_post, norm_mlp_pre, w_up, w_down, norm_mlp_post):
    row = lambda t: t.reshape(1, -1).astype(F32)
    return dict(
        tables=_rotary_tables(seq),
        s5=_s5_operators(a_re, a_im, log_step, b_re, b_im, c_re, c_im, seq // (CHUNK * N_SEG)),
        g_mix_pre=row(norm_mix_pre), w_in=w_in.astype(BF16), d_skip=row(d_skip), w_glu=w_glu.astype(BF16),
        b_glu=row(b_glu), w_out=w_out.astype(BF16), g_mix_post=row(norm_mix_post),
        g_mlp_pre=row(norm_mlp_pre), w_up=w_up.astype(BF16), w_down=w_down.astype(BF16),
        g_mlp_post=row(norm_mlp_post))


def _layer(x, p, tok_tile=512, attn_tile=256):
    bsz, seq, _ = x.shape
    n = bsz * seq
    x2 = x.reshape(n, D_MODEL)
    cos_t, sin_t = p['tables']
    w1, cs, pw, dec = p['s5']

    q, k, v, u, ug = _inproj(x2, p['g_mix_pre'], p['w_in'], cos_t, sin_t, seq, tok_tile)
    as3 = lambda t: t.reshape(bsz, seq, ATT_WIDTH)
    outs = [_banded_attention(as3(q), as3(k), as3(v), dil, attn_tile) for _, dil in DILATIONS]
    yg = _s5_scan(ug, w1, cs, pw, dec, bsz)

    flat = lambda t: t.reshape(n, ATT_WIDTH)
    x1 = _mix(flat(outs[0][0]), flat(outs[1][0]), flat(outs[2][0]),
              flat(outs[0][1]), flat(outs[1][1]), flat(outs[2][1]),
              yg, u, x2, p['d_skip'], p['w_glu'], p['b_glu'], p['w_out'], p['g_mix_post'], tok_tile)
    y = _mlp(x1, p['g_mlp_pre'], p['w_up'], p['w_down'], p['g_mlp_post'], tok_tile, 1024)
    return y.reshape(bsz, seq, D_MODEL)


def kernel(x_prompt, x_sample, norm_mix_pre, w_in, ssm_a_re, ssm_a_im, ssm_log_step, ssm_b_re, ssm_b_im,
           ssm_c_re, ssm_c_im, ssm_d, w_glu, b_glu, w_out, norm_mix_post, norm_mlp_pre, w_up, w_down,
           norm_mlp_post):
    weights = (norm_mix_pre, w_in, ssm_a_re, ssm_a_im, ssm_log_step, ssm_b_re, ssm_b_im, ssm_c_re, ssm_c_im,
               ssm_d, w_glu, b_glu, w_out, norm_mix_post, norm_mlp_pre, w_up, w_down, norm_mlp_post)
    depth = norm_mix_pre.shape[0]
    prepared = {}

    def run(x):
        seq = x.shape[1]
        if seq not in prepared:
            prepared[seq] = [_prepare(seq, *(w[l] for w in weights)) for l in range(depth)]
        for l in range(depth):
            x = _layer(x, prepared[seq][l])
        return x

    return run(x_prompt), run(x_sample)
```

```python
import functools
import math

import numpy as np
import jax
import jax.numpy as jnp
from jax import lax
from jax.experimental import pallas as pl
from jax.experimental.pallas import tpu as pltpu

F32 = jnp.float32
BF16 = jnp.bfloat16

D_MODEL = 1024
ATT_WIDTH = 512
SSM_WIDTH = 512
HEAD_DIM = 64
N_HEADS = 8
DILATIONS = ((128, 1), (512, 4), (2048, 16))
RADIUS = 64
SSM_GROUP = 16
N_GROUPS = 32
SSM_STATE = 64
D_FF = 4096
IN_WIDTH = 2048
ROPE_THETA = 10000.0
NORM_EPS = 1e-6
MASK_VALUE = -1e30

CHUNK = 16
N_SEG = 8
LANES = 128
SLOTS = LANES // SSM_GROUP
CW = CHUNK * SSM_GROUP
VMEM_LIMIT = 56 * 1024 * 1024
HI = lax.Precision.HIGHEST
LOG2E = 1.4426950408889634
LN2 = 0.6931471805599453


def _cparams(sem):
    return pltpu.CompilerParams(dimension_semantics=sem, vmem_limit_bytes=VMEM_LIMIT)


def _slot_masks():
    lane = lax.broadcasted_iota(jnp.int32, (1, LANES), 1)
    return [(lane // SSM_GROUP) == s for s in range(SLOTS)]


def _step_of_slot():
    gam = np.arange(SLOTS)[:, None, None]
    hh = np.arange(CHUNK // SLOTS)[None, :, None]
    l = np.arange(SLOTS)[None, None, :]
    return (SLOTS * hh + (l - gam) % SLOTS).reshape(SLOTS, CHUNK)


def _inproj_kernel(x_ref, g_ref, w_ref, cos_ref, sin_ref,
                   q1_ref, q4_ref, q16_ref, k1_ref, k4_ref, k16_ref, v1_ref, v4_ref, v16_ref,
                   u_ref, ug_ref, rs_ref, st_ref):
    x = x_ref[...]
    tile = x.shape[0]
    ms = jnp.mean(x * x, axis=-1, keepdims=True)
    h = (x * lax.rsqrt(ms + NORM_EPS)) * g_ref[...]
    proj = jnp.dot(h.astype(BF16), w_ref[...], preferred_element_type=F32)
    cos = cos_ref[...]
    sin = sin_ref[...]
    lane = lax.broadcasted_iota(jnp.int32, cos.shape, 1)
    first_half = (lane & (HEAD_DIM // 2)) == 0
    n_lt = ATT_WIDTH // LANES

    def rot(t):
        partner = jnp.where(first_half, pltpu.roll(t, LANES - HEAD_DIM // 2, 1),
                            pltpu.roll(t, HEAD_DIM // 2, 1))
        return t * cos + partner * sin

    def spread(lane_tile, nat_ref, d4_ref, d16_ref):
        for c in range(n_lt):
            rs_ref[c] = lane_tile(c)
            nat_ref[:, c * LANES:(c + 1) * LANES] = rs_ref[c].astype(BF16)
        for c in range(n_lt):
            for r4 in range(4):
                blk = rs_ref[c, pl.ds(r4, tile // 4, stride=4), :]
                d4_ref[r4, :, c * LANES:(c + 1) * LANES] = blk.astype(BF16)
                st_ref[c * 4 + r4] = blk
        for c in range(n_lt):
            for r4 in range(4):
                for r2 in range(4):
                    blk = st_ref[c * 4 + r4, pl.ds(r2, tile // 16, stride=4), :]
                    d16_ref[r4 + 4 * r2, :, c * LANES:(c + 1) * LANES] = blk.astype(BF16)

    q_scale = (HEAD_DIM ** -0.5) * LOG2E
    spread(lambda c: rot(proj[:, c * LANES:(c + 1) * LANES]) * q_scale, q1_ref, q4_ref, q16_ref)
    spread(lambda c: rot(proj[:, ATT_WIDTH + c * LANES:ATT_WIDTH + (c + 1) * LANES]), k1_ref, k4_ref, k16_ref)
    spread(lambda c: proj[:, 2 * ATT_WIDTH + c * LANES:2 * ATT_WIDTH + (c + 1) * LANES], v1_ref, v4_ref, v16_ref)
    u = proj[:, 3 * ATT_WIDTH:]
    u_ref[...] = u

    n_chunk = tile // CHUNK
    masks = _slot_masks()
    for qt in range(SSM_WIDTH // LANES):
        rs_ref[qt] = u[:, qt * LANES:(qt + 1) * LANES]
    for qt in range(SSM_WIDTH // LANES):
        for hh in range(CHUNK // SLOTS):
            rolled = []
            for t8 in range(SLOTS):
                step_rows = rs_ref[qt, pl.ds(SLOTS * hh + t8, n_chunk, stride=CHUNK), :]
                rolled.append(step_rows if t8 == 0 else pltpu.roll(step_rows, SSM_GROUP * t8, 1))
            for gam in range(SLOTS):
                res = rolled[SLOTS - 1]
                for t8 in range(SLOTS - 2, -1, -1):
                    res = jnp.where(masks[(gam + t8) % SLOTS], rolled[t8], res)
                ug_ref[SLOTS * qt + gam, :, hh * LANES:(hh + 1) * LANES] = res.astype(BF16)


def _inproj(x2, g, w_bf, cos_t, sin_t, seq, tile):
    n = x2.shape[0]
    n_pos = seq // tile
    tok = lambda i: (i, 0)
    pos = lambda i: (i % n_pos, 0)
    const = lambda i: (0, 0)
    split = lambda i: (0, i, 0)
    qkv_specs = [pl.BlockSpec((tile, ATT_WIDTH), tok), pl.BlockSpec((4, tile // 4, ATT_WIDTH), split),
                 pl.BlockSpec((16, tile // 16, ATT_WIDTH), split)]
    qkv_shapes = [jax.ShapeDtypeStruct((n, ATT_WIDTH), BF16), jax.ShapeDtypeStruct((4, n // 4, ATT_WIDTH), BF16),
                  jax.ShapeDtypeStruct((16, n // 16, ATT_WIDTH), BF16)]
    return pl.pallas_call(
        _inproj_kernel,
        grid=(n // tile,),
        in_specs=[
            pl.BlockSpec((tile, D_MODEL), tok),
            pl.BlockSpec((1, D_MODEL), const),
            pl.BlockSpec((D_MODEL, IN_WIDTH), const),
            pl.BlockSpec((tile, LANES), pos),
            pl.BlockSpec((tile, LANES), pos),
        ],
        out_specs=qkv_specs * 3 + [pl.BlockSpec((tile, SSM_WIDTH), tok),
                                   pl.BlockSpec((N_GROUPS, tile // CHUNK, CW), split)],
        out_shape=qkv_shapes * 3 + [jax.ShapeDtypeStruct((n, SSM_WIDTH), F32),
                                    jax.ShapeDtypeStruct((N_GROUPS, n // CHUNK, CW), BF16)],
        scratch_shapes=[pltpu.VMEM((ATT_WIDTH // LANES, tile, LANES), F32),
                        pltpu.VMEM((4 * ATT_WIDTH // LANES, tile // 4, LANES), F32)],
        compiler_params=_cparams(("arbitrary",)),
        name="inproj",
    )(x2, g, w_bf, cos_t, sin_t)


def _attn_kernel(q_ref, kp_ref, kc_ref, kn_ref, vp_ref, vc_ref, vn_ref, o_ref, lse_ref,
                 kx_ref, vx_ref, bias_ref, s_ref, p_ref, m_ref, ost_ref, lst_ref, nat_ref, mid_ref,
                 *, dil, tm, qb, tiles_per_seq):
    tile_in_seq = pl.program_id(0) % tiles_per_seq
    kx_ref[:, 0:RADIUS] = kp_ref[...]
    kx_ref[:, RADIUS:RADIUS + tm] = kc_ref[...]
    kx_ref[:, RADIUS + tm:] = kn_ref[...]
    vx_ref[:, 0:RADIUS] = vp_ref[...]
    vx_ref[:, RADIUS:RADIUS + tm] = vc_ref[...]
    vx_ref[:, RADIUS + tm:] = vn_ref[...]

    win = qb + 2 * RADIUS
    rows = 2 * qb
    qi = lax.broadcasted_iota(jnp.int32, (rows, win), 0) & (qb - 1)
    kcol = lax.broadcasted_iota(jnp.int32, (rows, win), 1)
    band = (kcol >= qi) & (kcol <= qi + 2 * RADIUS)
    after_start = kcol >= RADIUS
    before_end = kcol < qb + RADIUS
    neg = jnp.full((rows, win), MASK_VALUE, F32)
    zero = jnp.zeros((rows, win), F32)
    bias_ref[0] = jnp.where(band, zero, neg)
    bias_ref[1] = jnp.where(band & after_start, zero, neg)
    bias_ref[2] = jnp.where(band & before_end, zero, neg)
    bias_ref[3] = jnp.where(band & after_start & before_end, zero, neg)

    lane = lax.broadcasted_iota(jnp.int32, (qb, LANES), 1)
    head_a = lane < HEAD_DIM
    ones = jnp.ones((win, LANES), BF16)
    n_sb = tm // qb
    n_pair = N_HEADS // 2

    def item(w, carry):
        r = w // n_sb
        sb = w % n_sb
        row0 = pl.multiple_of(sb * qb, qb)
        at_start = jnp.logical_and(tile_in_seq == 0, sb == 0)
        at_end = jnp.logical_and(tile_in_seq == tiles_per_seq - 1, sb == n_sb - 1)
        bias = bias_ref[at_start.astype(jnp.int32) + 2 * at_end.astype(jnp.int32)]
        for p in range(n_pair):
            sl = slice(p * LANES, (p + 1) * LANES)
            q2 = q_ref[r, pl.ds(row0, qb), sl]
            zq = jnp.zeros_like(q2)
            qs = jnp.concatenate([jnp.where(head_a, q2, zq), jnp.where(head_a, zq, q2)], axis=0)
            kw = kx_ref[r, pl.ds(row0, win), sl]
            s_ref[p] = lax.dot_general(qs, kw, (((1,), (1,)), ((), ())), preferred_element_type=F32)
        for p in range(n_pair):
            s = s_ref[p] + bias
            m = jnp.max(s, axis=-1, keepdims=True)
            p_ref[p] = jnp.exp2(s - m).astype(BF16)
            m_ref[p] = jnp.broadcast_to(m, (rows, LANES))
        m8 = jnp.zeros((qb, LANES), F32)
        l8 = jnp.ones((qb, LANES), F32)
        for p in range(n_pair):
            sl = slice(p * LANES, (p + 1) * LANES)
            vaug = jnp.concatenate([vx_ref[r, pl.ds(row0, win), sl], ones], axis=1)
            res = jnp.dot(p_ref[p], vaug, preferred_element_type=F32)
            den = res[:, LANES:]
            mrow = m_ref[p]
            acc = jnp.where(head_a, res[:qb, :LANES], res[qb:, :LANES])
            ost_ref[r, pl.ds(row0, qb), sl] = acc / jnp.where(head_a, den[:qb], den[qb:])
            m8 = jnp.where(lane == 2 * p, mrow[:qb], jnp.where(lane == 2 * p + 1, mrow[qb:], m8))
            l8 = jnp.where(lane == 2 * p, den[:qb], jnp.where(lane == 2 * p + 1, den[qb:], l8))
        lst_ref[r, pl.ds(row0, qb), :] = m8 * LN2 + jnp.log(l8)
        return carry

    lax.fori_loop(0, dil * n_sb, item, 0)

    n_lt = ATT_WIDTH // LANES
    if dil == 1:
        o_ref[...] = ost_ref[0].astype(o_ref.dtype)
        lse_ref[...] = lst_ref[0]
    else:
        def gather(stage_ref, lt):
            sl = slice(lt * LANES, (lt + 1) * LANES)
            if dil == 4:
                for r in range(4):
                    nat_ref[lt, pl.ds(r, tm, stride=4), :] = stage_ref[r, :, sl]
            else:
                for r4 in range(4):
                    for r2 in range(4):
                        mid_ref[lt * 4 + r4, pl.ds(r2, tm, stride=4), :] = stage_ref[r4 + 4 * r2, :, sl]
                for r4 in range(4):
                    nat_ref[lt, pl.ds(r4, 4 * tm, stride=4), :] = mid_ref[lt * 4 + r4]
            return nat_ref[lt]

        for lt in range(n_lt):
            o_ref[:, lt * LANES:(lt + 1) * LANES] = gather(ost_ref, lt).astype(o_ref.dtype)
        lse_ref[...] = gather(lst_ref, 0)


def _banded_attention(q, k, v, dil, seq, nat_tile):
    _, n_rows, _ = q.shape
    n = n_rows * dil
    nat_tile = min(nat_tile, seq)
    tm = nat_tile // dil
    qb = min(2 * RADIUS, tm)
    assert dil in (1, 4, 16) and seq % nat_tile == 0 and tm % qb == 0 and qb % RADIUS == 0
    hb = tm // RADIUS
    n_halo = n_rows // RADIUS
    win = qb + 2 * RADIUS
    cur = pl.BlockSpec((dil, tm, ATT_WIDTH), lambda i: (0, i, 0))
    prev = pl.BlockSpec((dil, RADIUS, ATT_WIDTH), lambda i: (0, jnp.maximum(i * hb - 1, 0), 0))
    nxt = pl.BlockSpec((dil, RADIUS, ATT_WIDTH), lambda i: (0, jnp.minimum((i + 1) * hb, n_halo - 1), 0))
    n_lt = ATT_WIDTH // LANES
    return pl.pallas_call(
        functools.partial(_attn_kernel, dil=dil, tm=tm, qb=qb, tiles_per_seq=seq // nat_tile),
        grid=(n // nat_tile,),
        in_specs=[cur, prev, cur, nxt, prev, cur, nxt],
        out_specs=[pl.BlockSpec((nat_tile, ATT_WIDTH), lambda i: (i, 0)),
                   pl.BlockSpec((nat_tile, LANES), lambda i: (i, 0))],
        out_shape=[jax.ShapeDtypeStruct((n, ATT_WIDTH), BF16), jax.ShapeDtypeStruct((n, LANES), F32)],
        scratch_shapes=[
            pltpu.VMEM((dil, tm + 2 * RADIUS, ATT_WIDTH), BF16),
            pltpu.VMEM((dil, tm + 2 * RADIUS, ATT_WIDTH), BF16),
            pltpu.VMEM((4, 2 * qb, win), F32),
            pltpu.VMEM((N_HEADS // 2, 2 * qb, win), F32),
            pltpu.VMEM((N_HEADS // 2, 2 * qb, win), BF16),
            pltpu.VMEM((N_HEADS // 2, 2 * qb, LANES), F32),
            pltpu.VMEM((dil, tm, ATT_WIDTH), F32),
            pltpu.VMEM((dil, tm, LANES), F32),
            pltpu.VMEM((n_lt, nat_tile, LANES), F32),
            pltpu.VMEM((4 * n_lt, nat_tile // 4, LANES), F32),
        ],
        compiler_params=_cparams(("arbitrary",)),
        name=f"attn_d{dil}",
    )(q, k, k, k, v, v, v)


def _s5_kernel(ug_ref, w1_ref, cs_ref, pw_ref, dec_ref, y_ref, x_ref, h_ref, *, n_i):
    half = LANES // 2
    z = jnp.dot(ug_ref[...], w1_ref[...], preferred_element_type=F32)
    y_ref[...] = z[:, :CW]
    for seg in range(N_SEG):
        rows = slice(seg * n_i, (seg + 1) * n_i)
        x_ref[0, pl.ds(seg, n_i, stride=N_SEG), :] = z[rows, CW:CW + LANES]
        x_ref[1, pl.ds(seg, n_i, stride=N_SEG), :] = z[rows, CW + LANES:]
    lane = lax.broadcasted_iota(jnp.int32, (N_SEG, LANES), 1)
    fwd = lane < half
    a_re = jnp.broadcast_to(dec_ref[0:1, :LANES], (N_SEG, LANES))
    a_im = jnp.broadcast_to(dec_ref[0:1, LANES:], (N_SEG, LANES))

    def scan_body(s, carry):
        hre, him = carry
        rf = pl.ds(pl.multiple_of(s * N_SEG, N_SEG), N_SEG)
        rb = pl.ds(pl.multiple_of((n_i - 1 - s) * N_SEG, N_SEG), N_SEG)
        h_ref[0, rf, 0:half] = hre[:, :half]
        h_ref[0, rb, half:] = hre[:, half:]
        h_ref[1, rf, 0:half] = him[:, :half]
        h_ref[1, rb, half:] = him[:, half:]
        xre = jnp.where(fwd, x_ref[0, rf, :], x_ref[0, rb, :])
        xim = jnp.where(fwd, x_ref[1, rf, :], x_ref[1, rb, :])
        return a_re * hre - a_im * him + xre, a_re * him + a_im * hre + xim

    zero = jnp.zeros((N_SEG, LANES), F32)
    l_re, l_im = lax.fori_loop(0, n_i, scan_body, (zero, zero))

    lane1 = lax.broadcasted_iota(jnp.int32, (1, LANES), 1)
    fwd1 = lane1 < half
    s_re = dec_ref[1:2, :LANES]
    s_im = dec_ref[1:2, LANES:]
    e_re = jnp.zeros((1, LANES), F32)
    e_im = jnp.zeros((1, LANES), F32)
    ins = []
    for s in range(N_SEG):
        ins.append((e_re, e_im))
        lre = jnp.where(fwd1, l_re[s:s + 1], l_re[N_SEG - 1 - s:N_SEG - s])
        lim = jnp.where(fwd1, l_im[s:s + 1], l_im[N_SEG - 1 - s:N_SEG - s])
        e_re, e_im = s_re * e_re - s_im * e_im + lre, s_re * e_im + s_im * e_re + lim
    ein_re = jnp.concatenate(
        [jnp.where(fwd1, ins[s][0], ins[N_SEG - 1 - s][0]) for s in range(N_SEG)], axis=0)
    ein_im = jnp.concatenate(
        [jnp.where(fwd1, ins[s][1], ins[N_SEG - 1 - s][1]) for s in range(N_SEG)], axis=0)

    def fix_body(ib, c):
        pw = pw_ref[pl.ds(pl.multiple_of(ib * 8, 8), 8), :]
        for r in range(8):
            rows = pl.ds(pl.multiple_of((ib * 8 + r) * N_SEG, N_SEG), N_SEG)
            p_re = pw[r:r + 1, :LANES]
            p_im = pw[r:r + 1, LANES:]
            h_ref[0, rows, :] = h_ref[0, rows, :] + (p_re * ein_re - p_im * ein_im)
            h_ref[1, rows, :] = h_ref[1, rows, :] + (p_re * ein_im + p_im * ein_re)
        return c

    lax.fori_loop(0, n_i // 8, fix_body, 0)
    ungroup = lambda c: jnp.concatenate(
        [h_ref[c, pl.ds(seg, n_i, stride=N_SEG), :] for seg in range(N_SEG)], axis=0)
    hcat = jnp.concatenate([ungroup(0), ungroup(1)], axis=1).astype(BF16)
    y_ref[...] += jnp.dot(hcat, cs_ref[...], preferred_element_type=F32)


def _s5_scan(ug, w1, cs, pw, dec, bsz):
    n_g, total_rows, width = ug.shape
    rows = total_rows // bsz
    n_i = rows // N_SEG
    grp = lambda b, g: (g, 0, 0)
    seq_rows = pl.BlockSpec((None, rows, width), lambda b, g: (g, b, 0))
    return pl.pallas_call(
        functools.partial(_s5_kernel, n_i=n_i),
        grid=(bsz, n_g),
        in_specs=[
            seq_rows,
            pl.BlockSpec((None, width, 2 * width), grp),
            pl.BlockSpec((None, width, width), grp),
            pl.BlockSpec((None, n_i, width), grp),
            pl.BlockSpec((None, 2, width), grp),
        ],
        out_specs=seq_rows,
        out_shape=jax.ShapeDtypeStruct((n_g, total_rows, width), F32),
        scratch_shapes=[pltpu.VMEM((2, rows, LANES), F32)] * 2,
        compiler_params=_cparams(("arbitrary",) * 2),
        name="s5_scan",
    )(ug, w1, cs, pw, dec)


def _s5_operators(a_re, a_im, log_step, b_re, b_im, c_re, c_im, n_i):
    a_re, a_im, b_re, b_im, c_re, c_im = (t.astype(F32) for t in (a_re, a_im, b_re, b_im, c_re, c_im))
    step = jnp.exp(log_step.astype(F32))[..., None]
    zr, zi = a_re * step, a_im * step

    def power(n):
        n = jnp.asarray(n, F32).reshape(n.shape + (1, 1, 1))
        mag = jnp.exp(n * zr)
        return mag * jnp.cos(n * zi), mag * jnp.sin(n * zi)

    abr, abi = power(np.array(1))
    den = a_re * a_re + a_im * a_im
    fr = ((abr - 1.0) * a_re + abi * a_im) / den
    fi = (abi * a_re - (abr - 1.0) * a_im) / den
    bbr = fr[..., None] * b_re - fi[..., None] * b_im
    bbi = fr[..., None] * b_im + fi[..., None] * b_re

    pr, pi = power(np.arange(CHUNK + 1))
    wr = pr[:CHUNK, ..., None] * bbr - pi[:CHUNK, ..., None] * bbi
    wi = pr[:CHUNK, ..., None] * bbi + pi[:CHUNK, ..., None] * bbr
    kern = (jnp.einsum('dgcp,tdgpe->dgtce', c_re, wr, precision=HI)
            - jnp.einsum('dgcp,tdgpe->dgtce', c_im, wi, precision=HI))

    step_of = _step_of_slot()
    n_q = N_GROUPS // SLOTS
    s_step = step_of[:, :, None]
    t_step = step_of[:, None, :]
    tau = np.arange(CHUNK)[None, :, None, None]
    sel_f = (t_step - s_step)[:, None] == tau
    sel_b = (s_step - t_step)[:, None] == tau
    sel = jnp.asarray(np.concatenate([sel_f, sel_b], axis=1), F32)
    kx = jnp.transpose(kern, (1, 0, 2, 3, 4)).reshape(n_q, SLOTS, 2 * CHUNK, SSM_GROUP, SSM_GROUP)
    m = jnp.einsum('yxst,qyxce->qysetc', sel, kx, precision=HI).reshape(N_GROUPS, CW, CW)

    onehot = lambda idx: jnp.asarray(idx[..., None] == np.arange(CHUNK + 1), F32)
    grouped = lambda t: t.reshape((t.shape[0], n_q, SLOTS) + t.shape[2:])
    take = lambda oh, t: jnp.einsum('ypn,nqyk->qypk', oh, grouped(t), precision=HI)

    def state_in(d, idx):
        ar, ai = take(onehot(idx), pr[:, d]), take(onehot(idx), pi[:, d])
        br = jnp.swapaxes(grouped(bbr[d][None])[0], -1, -2)
        bi = jnp.swapaxes(grouped(bbi[d][None])[0], -1, -2)
        re = ar[:, :, :, None, :] * br[:, :, None] - ai[:, :, :, None, :] * bi[:, :, None]
        im = ar[:, :, :, None, :] * bi[:, :, None] + ai[:, :, :, None, :] * br[:, :, None]
        return re.reshape(N_GROUPS, CW, SSM_STATE), im.reshape(N_GROUPS, CW, SSM_STATE)

    bf_re, bf_im = state_in(0, CHUNK - 1 - step_of)
    bb_re, bb_im = state_in(1, step_of)
    w1 = jnp.concatenate([m, bf_re, bb_re, bf_im, bb_im], axis=-1).astype(BF16)

    def state_out(d, idx):
        ar, ai = take(onehot(idx), pr[:, d]), take(onehot(idx), pi[:, d])
        cr = jnp.swapaxes(grouped(c_re[d][None])[0], -1, -2)
        ci = jnp.swapaxes(grouped(c_im[d][None])[0], -1, -2)
        ar, ai = jnp.swapaxes(ar, -1, -2)[..., None], jnp.swapaxes(ai, -1, -2)[..., None]
        re = cr[:, :, :, None, :] * ar - ci[:, :, :, None, :] * ai
        im = cr[:, :, :, None, :] * ai + ci[:, :, :, None, :] * ar
        return re.reshape(N_GROUPS, SSM_STATE, CW), im.reshape(N_GROUPS, SSM_STATE, CW)

    cf_re, cf_im = state_out(0, step_of + 1)
    cb_re, cb_im = state_out(1, CHUNK - step_of)
    cs = jnp.concatenate([cf_re, cb_re, -cf_im, -cb_im], axis=1).astype(BF16)

    i_idx = np.arange(n_i)
    pf_re, pf_im = power(CHUNK * i_idx)
    pb_re, pb_im = power(CHUNK * (n_i - 1 - i_idx))
    pw = jnp.concatenate([pf_re[:, 0], pb_re[:, 1], pf_im[:, 0], pb_im[:, 1]], axis=-1)
    pw = jnp.transpose(pw, (1, 0, 2))
    dr, di = power(np.array([CHUNK, CHUNK * n_i]))
    dec = jnp.concatenate([dr[:, 0], dr[:, 1], di[:, 0], di[:, 1]], axis=-1)
    dec = jnp.transpose(dec, (1, 0, 2))
    return w1, cs, pw, dec


def _rms(x, g):
    return (x * lax.rsqrt(jnp.mean(x * x, axis=-1, keepdims=True) + NORM_EPS)) * g


def _mix_kernel(o1_ref, o4_ref, o16_ref, l1_ref, l4_ref, l16_ref, yg_ref, u_ref, x_ref,
                d_ref, wg_ref, bg_ref, wo_ref, g_ref, out_ref, ys_ref):
    n_chunk = x_ref.shape[0] // CHUNK
    masks = _slot_masks()
    for qt in range(SSM_WIDTH // LANES):
        for hh in range(CHUNK // SLOTS):
            src = [yg_ref[SLOTS * qt + gam, :, hh * LANES:(hh + 1) * LANES] for gam in range(SLOTS)]
            for t8 in range(SLOTS):
                pre = src[SLOTS - 1]
                for gam in range(SLOTS - 2, -1, -1):
                    pre = jnp.where(masks[(gam + t8) % SLOTS], src[gam], pre)
                nat = pre if t8 == 0 else pltpu.roll(pre, LANES - SSM_GROUP * t8, 1)
                ys_ref[qt, pl.ds(SLOTS * hh + t8, n_chunk, stride=CHUNK), :] = nat
    ys = jnp.concatenate([ys_ref[qt] for qt in range(SSM_WIDTH // LANES)], axis=1)

    l1, l4, l16 = l1_ref[...], l4_ref[...], l16_ref[...]
    mx = jnp.maximum(jnp.maximum(l1, l4), l16)
    e1, e4, e16 = jnp.exp(l1 - mx), jnp.exp(l4 - mx), jnp.exp(l16 - mx)
    inv = 1.0 / (e1 + e4 + e16)
    head_row = lax.broadcasted_iota(jnp.int32, (LANES, ATT_WIDTH), 0)
    head_col = lax.broadcasted_iota(jnp.int32, (LANES, ATT_WIDTH), 1) // HEAD_DIM
    spread = jnp.where(head_row == head_col, 1.0, 0.0).astype(BF16)

    def per_column(w):
        hi = w.astype(BF16)
        lo = (w - hi.astype(F32)).astype(BF16)
        return (jnp.dot(hi, spread, preferred_element_type=F32)
                + jnp.dot(lo, spread, preferred_element_type=F32))

    att = (per_column(e1 * inv) * o1_ref[...].astype(F32) + per_column(e4 * inv) * o4_ref[...].astype(F32)
           + per_column(e16 * inv) * o16_ref[...].astype(F32))
    y = ys + d_ref[...] * u_ref[...]
    y = 0.5 * y * (1.0 + jnp.tanh(math.sqrt(2.0 / math.pi) * (y + 0.044715 * (y * y * y))))
    gate = jnp.dot(y.astype(BF16), wg_ref[...], preferred_element_type=F32) + bg_ref[...]
    ssm = y * (1.0 / (1.0 + jnp.exp(-gate)))
    mixed = (jnp.dot(att.astype(BF16), wo_ref[:ATT_WIDTH], preferred_element_type=F32)
             + jnp.dot(ssm.astype(BF16), wo_ref[ATT_WIDTH:], preferred_element_type=F32))
    out_ref[...] = x_ref[...] + _rms(mixed, g_ref[...])


def _mix(o1, o4, o16, l1, l4, l16, yg, u, x2, d, wg, bg, wo, g, tile):
    n = x2.shape[0]
    tok = lambda i: (i, 0)
    const = lambda i: (0, 0)
    half = pl.BlockSpec((tile, ATT_WIDTH), tok)
    stat = pl.BlockSpec((tile, LANES), tok)
    return pl.pallas_call(
        _mix_kernel,
        grid=(n // tile,),
        in_specs=[half] * 3 + [stat] * 3 + [
            pl.BlockSpec((N_GROUPS, tile // CHUNK, CW), lambda i: (0, i, 0)),
            half,
            pl.BlockSpec((tile, D_MODEL), tok),
            pl.BlockSpec((1, SSM_WIDTH), const),
            pl.BlockSpec((SSM_WIDTH, SSM_WIDTH), const),
            pl.BlockSpec((1, SSM_WIDTH), const),
            pl.BlockSpec((D_MODEL, D_MODEL), const),
            pl.BlockSpec((1, D_MODEL), const),
        ],
        out_specs=pl.BlockSpec((tile, D_MODEL), tok),
        out_shape=jax.ShapeDtypeStruct((n, D_MODEL), F32),
        scratch_shapes=[pltpu.VMEM((SSM_WIDTH // LANES, tile, LANES), F32)],
        compiler_params=_cparams(("arbitrary",)),
        name="mix",
    )(o1, o4, o16, l1, l4, l16, yg, u, x2, d, wg, bg, wo, g)


def _mlp_kernel(x_ref, gpre_ref, wu_ref, wd_ref, gpost_ref, out_ref, *, ff_chunk):
    x = x_ref[...]
    h = _rms(x, gpre_ref[...]).astype(BF16)
    acc = jnp.zeros(x.shape, F32)
    for f in range(D_FF // ff_chunk):
        sl = slice(f * ff_chunk, (f + 1) * ff_chunk)
        a = jnp.maximum(jnp.dot(h, wu_ref[:, sl], preferred_element_type=F32), 0.0)
        acc = acc + jnp.dot((a * a).astype(BF16), wd_ref[sl, :], preferred_element_type=F32)
    out_ref[...] = x + _rms(acc, gpost_ref[...])


def _mlp(x2, gpre, wu, wd, gpost, tile, ff_chunk):
    n = x2.shape[0]
    tok = lambda i: (i, 0)
    const = lambda i: (0, 0)
    return pl.pallas_call(
        functools.partial(_mlp_kernel, ff_chunk=ff_chunk),
        grid=(n // tile,),
        in_specs=[
            pl.BlockSpec((tile, D_MODEL), tok),
            pl.BlockSpec((1, D_MODEL), const),
            pl.BlockSpec((D_MODEL, D_FF), const),
            pl.BlockSpec((D_FF, D_MODEL), const),
            pl.BlockSpec((1, D_MODEL), const),
        ],
        out_specs=pl.BlockSpec((tile, D_MODEL), tok),
        out_shape=jax.ShapeDtypeStruct((n, D_MODEL), F32),
        compiler_params=_cparams(("arbitrary",)),
        name="mlp",
    )(x2, gpre, wu, wd, gpost)


def _rotary_tables(seq):
    half = HEAD_DIM // 2
    inv_freq = 1.0 / (ROPE_THETA ** (jnp.arange(half, dtype=F32) / half))
    ang = jnp.arange(seq, dtype=F32)[:, None] * inv_freq[None, :]
    cos, sin = jnp.cos(ang), jnp.sin(ang)
    return jnp.tile(cos, (1, 4)), jnp.tile(jnp.concatenate([-sin, sin], axis=-1), (1, 2))


def _prepare(seq, norm_mix_pre, w_in, a_re, a_im, log_step, b_re, b_im, c_re, c_im, d_skip, w_glu, b_glu,
             w_out, norm_mix_post, norm_mlp_pre, w_up, w_down, norm_mlp_post):
    row = lambda t: t.reshape(1, -1).astype(F32)
    return dict(
        tables=_rotary_tables(seq),
        s5=_s5_operators(a_re, a_im, log_step, b_re, b_im, c_re, c_im, seq // (CHUNK * N_SEG)),
        g_mix_pre=row(norm_mix_pre), w_in=w_in.astype(BF16), d_skip=row(d_skip), w_glu=w_glu.astype(BF16),
        b_glu=row(b_glu), w_out=w_out.astype(BF16), g_mix_post=row(norm_mix_post),
        g_mlp_pre=row(norm_mlp_pre), w_up=w_up.astype(BF16), w_down=w_down.astype(BF16),
        g_mlp_post=row(norm_mlp_post))


def _layer(x, p, tok_tile=512, attn_tile=1024):
    bsz, seq, _ = x.shape
    n = bsz * seq
    x2 = x.reshape(n, D_MODEL)
    cos_t, sin_t = p['tables']
    w1, cs, pw, dec = p['s5']

    q1, q4, q16, k1, k4, k16, v1, v4, v16, u, ug = _inproj(
        x2, p['g_mix_pre'], p['w_in'], cos_t, sin_t, seq, tok_tile)
    (o1, l1), (o4, l4), (o16, l16) = (
        _banded_attention(q1[None], k1[None], v1[None], 1, seq, attn_tile),
        _banded_attention(q4, k4, v4, 4, seq, attn_tile),
        _banded_attention(q16, k16, v16, 16, seq, attn_tile))
    yg = _s5_scan(ug, w1, cs, pw, dec, bsz)
    x1 = _mix(o1, o4, o16, l1, l4, l16, yg, u, x2, p['d_skip'], p['w_glu'], p['b_glu'], p['w_out'],
              p['g_mix_post'], tok_tile)
    y = _mlp(x1, p['g_mlp_pre'], p['w_up'], p['w_down'], p['g_mlp_post'], tok_tile, 1024)
    return y.reshape(bsz, seq, D_MODEL)


def kernel(x_prompt, x_sample, norm_mix_pre, w_in, ssm_a_re, ssm_a_im, ssm_log_step, ssm_b_re, ssm_b_im,
           ssm_c_re, ssm_c_im, ssm_d, w_glu, b_glu, w_out, norm_mix_post, norm_mlp_pre, w_up, w_down,
           norm_mlp_post):
    weights = (norm_mix_pre, w_in, ssm_a_re, ssm_a_im, ssm_log_step, ssm_b_re, ssm_b_im, ssm_c_re, ssm_c_im,
               ssm_d, w_glu, b_glu, w_out, norm_mix_post, norm_mlp_pre, w_up, w_down, norm_mlp_post)
    depth = norm_mix_pre.shape[0]
    prepared = {}

    def run(x):
        seq = x.shape[1]
        if seq not in prepared:
            prepared[seq] = [_prepare(seq, *(w[l] for w in weights)) for l in range(depth)]
        for l in range(depth):
            x = _layer(x, prepared[seq][l])
        return x

    return run(x_prompt), run(x_sample)
```

```python
import functools
import math

import numpy as np
import jax
import jax.numpy as jnp
from jax import lax
from jax.experimental import pallas as pl
from jax.experimental.pallas import tpu as pltpu

F32 = jnp.float32
BF16 = jnp.bfloat16

D_MODEL = 1024
ATT_WIDTH = 512
SSM_WIDTH = 512
HEAD_DIM = 64
N_HEADS = 8
DILATIONS = ((128, 1), (512, 4), (2048, 16))
RADIUS = 64
SSM_GROUP = 16
N_GROUPS = 32
SSM_STATE = 64
D_FF = 4096
IN_WIDTH = 2048
ROPE_THETA = 10000.0
NORM_EPS = 1e-6
MASK_VALUE = -1e30

CHUNK = 16
N_SEG = 8
LANES = 128
SLOTS = LANES // SSM_GROUP
CW = CHUNK * SSM_GROUP
VMEM_LIMIT = 56 * 1024 * 1024
HI = lax.Precision.HIGHEST
LOG2E = 1.4426950408889634
LN2 = 0.6931471805599453


def _cparams(sem):
    return pltpu.CompilerParams(dimension_semantics=sem, vmem_limit_bytes=VMEM_LIMIT)


def _slot_masks():
    lane = lax.broadcasted_iota(jnp.int32, (1, LANES), 1)
    return [(lane // SSM_GROUP) == s for s in range(SLOTS)]


def _step_of_slot():
    gam = np.arange(SLOTS)[:, None, None]
    hh = np.arange(CHUNK // SLOTS)[None, :, None]
    l = np.arange(SLOTS)[None, None, :]
    return (SLOTS * hh + (l - gam) % SLOTS).reshape(SLOTS, CHUNK)


def _inproj_kernel(x_ref, g_ref, w_ref, cos_ref, sin_ref,
                   q1_ref, q4_ref, q16_ref, k1_ref, k4_ref, k16_ref, v1_ref, v4_ref, v16_ref,
                   u_ref, ug_ref, rs_ref, st_ref):
    x = x_ref[...]
    tile = x.shape[0]
    ms = jnp.mean(x * x, axis=-1, keepdims=True)
    h = (x * lax.rsqrt(ms + NORM_EPS)) * g_ref[...]
    proj = jnp.dot(h.astype(BF16), w_ref[...], preferred_element_type=F32)
    cos = cos_ref[...]
    sin = sin_ref[...]
    lane = lax.broadcasted_iota(jnp.int32, cos.shape, 1)
    first_half = (lane & (HEAD_DIM // 2)) == 0
    n_lt = ATT_WIDTH // LANES

    def rot(t):
        partner = jnp.where(first_half, pltpu.roll(t, LANES - HEAD_DIM // 2, 1),
                            pltpu.roll(t, HEAD_DIM // 2, 1))
        return t * cos + partner * sin

    def spread(lane_tile, nat_ref, d4_ref, d16_ref):
        for c in range(n_lt):
            rs_ref[c] = lane_tile(c)
            nat_ref[:, c * LANES:(c + 1) * LANES] = rs_ref[c].astype(BF16)
        for c in range(n_lt):
            for r4 in range(4):
                blk = rs_ref[c, pl.ds(r4, tile // 4, stride=4), :]
                d4_ref[r4, :, c * LANES:(c + 1) * LANES] = blk.astype(BF16)
                st_ref[c * 4 + r4] = blk
        for c in range(n_lt):
            for r4 in range(4):
                for r2 in range(4):
                    blk = st_ref[c * 4 + r4, pl.ds(r2, tile // 16, stride=4), :]
                    d16_ref[r4 + 4 * r2, :, c * LANES:(c + 1) * LANES] = blk.astype(BF16)

    q_scale = (HEAD_DIM ** -0.5) * LOG2E
    spread(lambda c: rot(proj[:, c * LANES:(c + 1) * LANES]) * q_scale, q1_ref, q4_ref, q16_ref)
    spread(lambda c: rot(proj[:, ATT_WIDTH + c * LANES:ATT_WIDTH + (c + 1) * LANES]), k1_ref, k4_ref, k16_ref)
    spread(lambda c: proj[:, 2 * ATT_WIDTH + c * LANES:2 * ATT_WIDTH + (c + 1) * LANES], v1_ref, v4_ref, v16_ref)
    u = proj[:, 3 * ATT_WIDTH:]
    u_ref[...] = u

    n_chunk = tile // CHUNK
    masks = _slot_masks()
    for qt in range(SSM_WIDTH // LANES):
        rs_ref[qt] = u[:, qt * LANES:(qt + 1) * LANES]
    for qt in range(SSM_WIDTH // LANES):
        for hh in range(CHUNK // SLOTS):
            rolled = []
            for t8 in range(SLOTS):
                step_rows = rs_ref[qt, pl.ds(SLOTS * hh + t8, n_chunk, stride=CHUNK), :]
                rolled.append(step_rows if t8 == 0 else pltpu.roll(step_rows, SSM_GROUP * t8, 1))
            for gam in range(SLOTS):
                res = rolled[SLOTS - 1]
                for t8 in range(SLOTS - 2, -1, -1):
                    res = jnp.where(masks[(gam + t8) % SLOTS], rolled[t8], res)
                ug_ref[SLOTS * qt + gam, :, hh * LANES:(hh + 1) * LANES] = res.astype(BF16)


def _inproj(x2, g, w_bf, cos_t, sin_t, seq, tile):
    n = x2.shape[0]
    n_pos = seq // tile
    tok = lambda i: (i, 0)
    pos = lambda i: (i % n_pos, 0)
    const = lambda i: (0, 0)
    split = lambda i: (0, i, 0)
    qkv_specs = [pl.BlockSpec((tile, ATT_WIDTH), tok), pl.BlockSpec((4, tile // 4, ATT_WIDTH), split),
                 pl.BlockSpec((16, tile // 16, ATT_WIDTH), split)]
    qkv_shapes = [jax.ShapeDtypeStruct((n, ATT_WIDTH), BF16), jax.ShapeDtypeStruct((4, n // 4, ATT_WIDTH), BF16),
                  jax.ShapeDtypeStruct((16, n // 16, ATT_WIDTH), BF16)]
    return pl.pallas_call(
        _inproj_kernel,
        grid=(n // tile,),
        in_specs=[
            pl.BlockSpec((tile, D_MODEL), tok),
            pl.BlockSpec((1, D_MODEL), const),
            pl.BlockSpec((D_MODEL, IN_WIDTH), const),
            pl.BlockSpec((tile, LANES), pos),
            pl.BlockSpec((tile, LANES), pos),
        ],
        out_specs=qkv_specs * 3 + [pl.BlockSpec((tile, SSM_WIDTH), tok),
                                   pl.BlockSpec((N_GROUPS, tile // CHUNK, CW), split)],
        out_shape=qkv_shapes * 3 + [jax.ShapeDtypeStruct((n, SSM_WIDTH), F32),
                                    jax.ShapeDtypeStruct((N_GROUPS, n // CHUNK, CW), BF16)],
        scratch_shapes=[pltpu.VMEM((ATT_WIDTH // LANES, tile, LANES), F32),
                        pltpu.VMEM((4 * ATT_WIDTH // LANES, tile // 4, LANES), F32)],
        compiler_params=_cparams(("arbitrary",)),
        name="inproj",
    )(x2, g, w_bf, cos_t, sin_t)


def _attn_kernel(q_ref, kp_ref, kc_ref, kn_ref, vp_ref, vc_ref, vn_ref, o_ref, lse_ref,
                 kx_ref, vx_ref, bias_ref, s_ref, p_ref, m_ref, ost_ref, lst_ref, nat_ref, mid_ref,
                 *, dil, tm, qb, tiles_per_seq):
    tile_in_seq = pl.program_id(0) % tiles_per_seq
    kx_ref[:, 0:RADIUS] = kp_ref[...]
    kx_ref[:, RADIUS:RADIUS + tm] = kc_ref[...]
    kx_ref[:, RADIUS + tm:] = kn_ref[...]
    vx_ref[:, 0:RADIUS] = vp_ref[...]
    vx_ref[:, RADIUS:RADIUS + tm] = vc_ref[...]
    vx_ref[:, RADIUS + tm:] = vn_ref[...]

    win = qb + 2 * RADIUS
    rows = 2 * qb
    qi = lax.broadcasted_iota(jnp.int32, (rows, win), 0) & (qb - 1)
    kcol = lax.broadcasted_iota(jnp.int32, (rows, win), 1)
    band = (kcol >= qi) & (kcol <= qi + 2 * RADIUS)
    after_start = kcol >= RADIUS
    before_end = kcol < qb + RADIUS
    neg = jnp.full((rows, win), MASK_VALUE, F32)
    zero = jnp.zeros((rows, win), F32)
    bias_ref[0] = jnp.where(band, zero, neg)
    bias_ref[1] = jnp.where(band & after_start, zero, neg)
    bias_ref[2] = jnp.where(band & before_end, zero, neg)
    bias_ref[3] = jnp.where(band & after_start & before_end, zero, neg)

    lane = lax.broadcasted_iota(jnp.int32, (qb, LANES), 1)
    head_a = lane < HEAD_DIM
    ones = jnp.ones((win, LANES), BF16)
    n_sb = tm // qb
    n_pair = N_HEADS // 2

    def item(w, carry):
        r = w // n_sb
        sb = w % n_sb
        row0 = pl.multiple_of(sb * qb, qb)
        at_start = jnp.logical_and(tile_in_seq == 0, sb == 0)
        at_end = jnp.logical_and(tile_in_seq == tiles_per_seq - 1, sb == n_sb - 1)
        bias = bias_ref[at_start.astype(jnp.int32) + 2 * at_end.astype(jnp.int32)]
        for p in range(n_pair):
            sl = slice(p * LANES, (p + 1) * LANES)
            q2 = q_ref[r, pl.ds(row0, qb), sl]
            zq = jnp.zeros_like(q2)
            qs = jnp.concatenate([jnp.where(head_a, q2, zq), jnp.where(head_a, zq, q2)], axis=0)
            kw = kx_ref[r, pl.ds(row0, win), sl]
            s_ref[p] = lax.dot_general(qs, kw, (((1,), (1,)), ((), ())), preferred_element_type=F32)
        for p in range(n_pair):
            s = s_ref[p] + bias
            m = jnp.max(s, axis=-1, keepdims=True)
            p_ref[p] = jnp.exp2(s - m).astype(BF16)
            m_ref[p] = jnp.broadcast_to(m, (rows, LANES))
        m8 = jnp.zeros((qb, LANES), F32)
        l8 = jnp.ones((qb, LANES), F32)
        for p in range(n_pair):
            sl = slice(p * LANES, (p + 1) * LANES)
            vaug = jnp.concatenate([vx_ref[r, pl.ds(row0, win), sl], ones], axis=1)
            res = jnp.dot(p_ref[p], vaug, preferred_element_type=F32)
            den = res[:, LANES:]
            mrow = m_ref[p]
            acc = jnp.where(head_a, res[:qb, :LANES], res[qb:, :LANES])
            ost_ref[r, pl.ds(row0, qb), sl] = acc / jnp.where(head_a, den[:qb], den[qb:])
            m8 = jnp.where(lane == 2 * p, mrow[:qb], jnp.where(lane == 2 * p + 1, mrow[qb:], m8))
            l8 = jnp.where(lane == 2 * p, den[:qb], jnp.where(lane == 2 * p + 1, den[qb:], l8))
        lst_ref[r, pl.ds(row0, qb), :] = m8 * LN2 + jnp.log(l8)
        return carry

    lax.fori_loop(0, dil * n_sb, item, 0, unroll=2)

    n_lt = ATT_WIDTH // LANES
    if dil == 1:
        o_ref[...] = ost_ref[0].astype(o_ref.dtype)
        lse_ref[...] = lst_ref[0]
    else:
        def gather(stage_ref, lt):
            sl = slice(lt * LANES, (lt + 1) * LANES)
            if dil == 4:
                for r in range(4):
                    nat_ref[pl.ds(r, tm, stride=4), :] = stage_ref[r, :, sl]
            else:
                for r4 in range(4):
                    for r2 in range(4):
                        mid_ref[r4, pl.ds(r2, tm, stride=4), :] = stage_ref[r4 + 4 * r2, :, sl]
                for r4 in range(4):
                    nat_ref[pl.ds(r4, 4 * tm, stride=4), :] = mid_ref[r4]
            return nat_ref[...]

        for lt in range(n_lt):
            o_ref[:, lt * LANES:(lt + 1) * LANES] = gather(ost_ref, lt).astype(o_ref.dtype)
        lse_ref[...] = gather(lst_ref, 0)


def _banded_attention(q, k, v, dil, seq, nat_tile):
    _, n_rows, _ = q.shape
    n = n_rows * dil
    nat_tile = min(nat_tile, seq)
    tm = nat_tile // dil
    qb = min(2 * RADIUS, tm)
    assert dil in (1, 4, 16) and seq % nat_tile == 0 and tm % qb == 0 and qb % RADIUS == 0
    hb = tm // RADIUS
    n_halo = n_rows // RADIUS
    win = qb + 2 * RADIUS
    cur = pl.BlockSpec((dil, tm, ATT_WIDTH), lambda i: (0, i, 0))
    prev = pl.BlockSpec((dil, RADIUS, ATT_WIDTH), lambda i: (0, jnp.maximum(i * hb - 1, 0), 0))
    nxt = pl.BlockSpec((dil, RADIUS, ATT_WIDTH), lambda i: (0, jnp.minimum((i + 1) * hb, n_halo - 1), 0))
    return pl.pallas_call(
        functools.partial(_attn_kernel, dil=dil, tm=tm, qb=qb, tiles_per_seq=seq // nat_tile),
        grid=(n // nat_tile,),
        in_specs=[cur, prev, cur, nxt, prev, cur, nxt],
        out_specs=[pl.BlockSpec((nat_tile, ATT_WIDTH), lambda i: (i, 0)),
                   pl.BlockSpec((nat_tile, LANES), lambda i: (i, 0))],
        out_shape=[jax.ShapeDtypeStruct((n, ATT_WIDTH), BF16), jax.ShapeDtypeStruct((n, LANES), F32)],
        scratch_shapes=[
            pltpu.VMEM((dil, tm + 2 * RADIUS, ATT_WIDTH), BF16),
            pltpu.VMEM((dil, tm + 2 * RADIUS, ATT_WIDTH), BF16),
            pltpu.VMEM((4, 2 * qb, win), F32),
            pltpu.VMEM((N_HEADS // 2, 2 * qb, win), F32),
            pltpu.VMEM((N_HEADS // 2, 2 * qb, win), BF16),
            pltpu.VMEM((N_HEADS // 2, 2 * qb, LANES), F32),
            pltpu.VMEM((dil, tm, ATT_WIDTH), F32),
            pltpu.VMEM((dil, tm, LANES), F32),
            pltpu.VMEM((nat_tile, LANES), F32),
            pltpu.VMEM((4, nat_tile // 4, LANES), F32),
        ],
        compiler_params=_cparams(("arbitrary",)),
        name=f"attn_d{dil}",
    )(q, k, k, k, v, v, v)


def _s5_kernel(ug_ref, w1_ref, cs_ref, pw_ref, dec_ref, y_ref, x_ref, h_ref, *, n_i):
    half = LANES // 2
    z = jnp.dot(ug_ref[...], w1_ref[...], preferred_element_type=F32)
    y_ref[...] = z[:, :CW]
    for seg in range(N_SEG):
        rows = slice(seg * n_i, (seg + 1) * n_i)
        x_ref[0, pl.ds(seg, n_i, stride=N_SEG), :] = z[rows, CW:CW + LANES]
        x_ref[1, pl.ds(seg, n_i, stride=N_SEG), :] = z[rows, CW + LANES:]
    lane = lax.broadcasted_iota(jnp.int32, (N_SEG, LANES), 1)
    fwd = lane < half
    a_re = jnp.broadcast_to(dec_ref[0:1, :LANES], (N_SEG, LANES))
    a_im = jnp.broadcast_to(dec_ref[0:1, LANES:], (N_SEG, LANES))

    def scan_body(s, carry):
        hre, him = carry
        rf = pl.ds(pl.multiple_of(s * N_SEG, N_SEG), N_SEG)
        rb = pl.ds(pl.multiple_of((n_i - 1 - s) * N_SEG, N_SEG), N_SEG)
        h_ref[0, rf, 0:half] = hre[:, :half]
        h_ref[0, rb, half:] = hre[:, half:]
        h_ref[1, rf, 0:half] = him[:, :half]
        h_ref[1, rb, half:] = him[:, half:]
        xre = jnp.where(fwd, x_ref[0, rf, :], x_ref[0, rb, :])
        xim = jnp.where(fwd, x_ref[1, rf, :], x_ref[1, rb, :])
        return a_re * hre - a_im * him + xre, a_re * him + a_im * hre + xim

    zero = jnp.zeros((N_SEG, LANES), F32)
    l_re, l_im = lax.fori_loop(0, n_i, scan_body, (zero, zero))

    lane1 = lax.broadcasted_iota(jnp.int32, (1, LANES), 1)
    fwd1 = lane1 < half
    s_re = dec_ref[1:2, :LANES]
    s_im = dec_ref[1:2, LANES:]
    e_re = jnp.zeros((1, LANES), F32)
    e_im = jnp.zeros((1, LANES), F32)
    ins = []
    for s in range(N_SEG):
        ins.append((e_re, e_im))
        lre = jnp.where(fwd1, l_re[s:s + 1], l_re[N_SEG - 1 - s:N_SEG - s])
        lim = jnp.where(fwd1, l_im[s:s + 1], l_im[N_SEG - 1 - s:N_SEG - s])
        e_re, e_im = s_re * e_re - s_im * e_im + lre, s_re * e_im + s_im * e_re + lim
    ein_re = jnp.concatenate(
        [jnp.where(fwd1, ins[s][0], ins[N_SEG - 1 - s][0]) for s in range(N_SEG)], axis=0)
    ein_im = jnp.concatenate(
        [jnp.where(fwd1, ins[s][1], ins[N_SEG - 1 - s][1]) for s in range(N_SEG)], axis=0)

    def fix_body(ib, c):
        pw = pw_ref[pl.ds(pl.multiple_of(ib * 8, 8), 8), :]
        for r in range(8):
            rows = pl.ds(pl.multiple_of((ib * 8 + r) * N_SEG, N_SEG), N_SEG)
            p_re = pw[r:r + 1, :LANES]
            p_im = pw[r:r + 1, LANES:]
            h_ref[0, rows, :] = h_ref[0, rows, :] + (p_re * ein_re - p_im * ein_im)
            h_ref[1, rows, :] = h_ref[1, rows, :] + (p_re * ein_im + p_im * ein_re)
        return c

    lax.fori_loop(0, n_i // 8, fix_body, 0)
    ungroup = lambda c: jnp.concatenate(
        [h_ref[c, pl.ds(seg, n_i, stride=N_SEG), :] for seg in range(N_SEG)], axis=0)
    hcat = jnp.concatenate([ungroup(0), ungroup(1)], axis=1).astype(BF16)
    y_ref[...] += jnp.dot(hcat, cs_ref[...], preferred_element_type=F32)


def _s5_scan(ug, w1, cs, pw, dec, bsz):
    n_g, total_rows, width = ug.shape
    rows = total_rows // bsz
    n_i = rows // N_SEG
    grp = lambda b, g: (g, 0, 0)
    seq_rows = pl.BlockSpec((None, rows, width), lambda b, g: (g, b, 0))
    return pl.pallas_call(
        functools.partial(_s5_kernel, n_i=n_i),
        grid=(bsz, n_g),
        in_specs=[
            seq_rows,
            pl.BlockSpec((None, width, 2 * width), grp),
            pl.BlockSpec((None, width, width), grp),
            pl.BlockSpec((None, n_i, width), grp),
            pl.BlockSpec((None, 2, width), grp),
        ],
        out_specs=seq_rows,
        out_shape=jax.ShapeDtypeStruct((n_g, total_rows, width), F32),
        scratch_shapes=[pltpu.VMEM((2, rows, LANES), F32)] * 2,
        compiler_params=_cparams(("arbitrary",) * 2),
        name="s5_scan",
    )(ug, w1, cs, pw, dec)


def _s5_operators(a_re, a_im, log_step, b_re, b_im, c_re, c_im, n_i):
    a_re, a_im, b_re, b_im, c_re, c_im = (t.astype(F32) for t in (a_re, a_im, b_re, b_im, c_re, c_im))
    step = jnp.exp(log_step.astype(F32))[..., None]
    zr, zi = a_re * step, a_im * step

    def power(n):
        n = jnp.asarray(n, F32).reshape(n.shape + (1, 1, 1))
        mag = jnp.exp(n * zr)
        return mag * jnp.cos(n * zi), mag * jnp.sin(n * zi)

    abr, abi = power(np.array(1))
    den = a_re * a_re + a_im * a_im
    fr = ((abr - 1.0) * a_re + abi * a_im) / den
    fi = (abi * a_re - (abr - 1.0) * a_im) / den
    bbr = fr[..., None] * b_re - fi[..., None] * b_im
    bbi = fr[..., None] * b_im + fi[..., None] * b_re

    pr, pi = power(np.arange(CHUNK + 1))
    wr = pr[:CHUNK, ..., None] * bbr - pi[:CHUNK, ..., None] * bbi
    wi = pr[:CHUNK, ..., None] * bbi + pi[:CHUNK, ..., None] * bbr
    kern = (jnp.einsum('dgcp,tdgpe->dgtce', c_re, wr, precision=HI)
            - jnp.einsum('dgcp,tdgpe->dgtce', c_im, wi, precision=HI))

    step_of = _step_of_slot()
    n_q = N_GROUPS // SLOTS
    s_step = step_of[:, :, None]
    t_step = step_of[:, None, :]
    tau = np.arange(CHUNK)[None, :, None, None]
    sel_f = (t_step - s_step)[:, None] == tau
    sel_b = (s_step - t_step)[:, None] == tau
    sel = jnp.asarray(np.concatenate([sel_f, sel_b], axis=1), F32)
    kx = jnp.transpose(kern, (1, 0, 2, 3, 4)).reshape(n_q, SLOTS, 2 * CHUNK, SSM_GROUP, SSM_GROUP)
    m = jnp.einsum('yxst,qyxce->qysetc', sel, kx, precision=HI).reshape(N_GROUPS, CW, CW)

    onehot = lambda idx: jnp.asarray(idx[..., None] == np.arange(CHUNK + 1), F32)
    grouped = lambda t: t.reshape((t.shape[0], n_q, SLOTS) + t.shape[2:])
    take = lambda oh, t: jnp.einsum('ypn,nqyk->qypk', oh, grouped(t), precision=HI)

    def state_in(d, idx):
        ar, ai = take(onehot(idx), pr[:, d]), take(onehot(idx), pi[:, d])
        br = jnp.swapaxes(grouped(bbr[d][None])[0], -1, -2)
        bi = jnp.swapaxes(grouped(bbi[d][None])[0], -1, -2)
        re = ar[:, :, :, None, :] * br[:, :, None] - ai[:, :, :, None, :] * bi[:, :, None]
        im = ar[:, :, :, None, :] * bi[:, :, None] + ai[:, :, :, None, :] * br[:, :, None]
        return re.reshape(N_GROUPS, CW, SSM_STATE), im.reshape(N_GROUPS, CW, SSM_STATE)

    bf_re, bf_im = state_in(0, CHUNK - 1 - step_of)
    bb_re, bb_im = state_in(1, step_of)
    w1 = jnp.concatenate([m, bf_re, bb_re, bf_im, bb_im], axis=-1).astype(BF16)

    def state_out(d, idx):
        ar, ai = take(onehot(idx), pr[:, d]), take(onehot(idx), pi[:, d])
        cr = jnp.swapaxes(grouped(c_re[d][None])[0], -1, -2)
        ci = jnp.swapaxes(grouped(c_im[d][None])[0], -1, -2)
        ar, ai = jnp.swapaxes(ar, -1, -2)[..., None], jnp.swapaxes(ai, -1, -2)[..., None]
        re = cr[:, :, :, None, :] * ar - ci[:, :, :, None, :] * ai
        im = cr[:, :, :, None, :] * ai + ci[:, :, :, None, :] * ar
        return re.reshape(N_GROUPS, SSM_STATE, CW), im.reshape(N_GROUPS, SSM_STATE, CW)

    cf_re, cf_im = state_out(0, step_of + 1)
    cb_re, cb_im = state_out(1, CHUNK - step_of)
    cs = jnp.concatenate([cf_re, cb_re, -cf_im, -cb_im], axis=1).astype(BF16)

    i_idx = np.arange(n_i)
    pf_re, pf_im = power(CHUNK * i_idx)
    pb_re, pb_im = power(CHUNK * (n_i - 1 - i_idx))
    pw = jnp.concatenate([pf_re[:, 0], pb_re[:, 1], pf_im[:, 0], pb_im[:, 1]], axis=-1)
    pw = jnp.transpose(pw, (1, 0, 2))
    dr, di = power(np.array([CHUNK, CHUNK * n_i]))
    dec = jnp.concatenate([dr[:, 0], dr[:, 1], di[:, 0], di[:, 1]], axis=-1)
    dec = jnp.transpose(dec, (1, 0, 2))
    return w1, cs, pw, dec


def _rms(x, g):
    return (x * lax.rsqrt(jnp.mean(x * x, axis=-1, keepdims=True) + NORM_EPS)) * g


def _mix_kernel(o1_ref, o4_ref, o16_ref, l1_ref, l4_ref, l16_ref, yg_ref, u_ref, x_ref,
                d_ref, wg_ref, bg_ref, wo_ref, g_ref, out_ref, ys_ref):
    n_chunk = x_ref.shape[0] // CHUNK
    masks = _slot_masks()
    for qt in range(SSM_WIDTH // LANES):
        for hh in range(CHUNK // SLOTS):
            src = [yg_ref[SLOTS * qt + gam, :, hh * LANES:(hh + 1) * LANES] for gam in range(SLOTS)]
            for t8 in range(SLOTS):
                pre = src[SLOTS - 1]
                for gam in range(SLOTS - 2, -1, -1):
                    pre = jnp.where(masks[(gam + t8) % SLOTS], src[gam], pre)
                nat = pre if t8 == 0 else pltpu.roll(pre, LANES - SSM_GROUP * t8, 1)
                ys_ref[qt, pl.ds(SLOTS * hh + t8, n_chunk, stride=CHUNK), :] = nat
    ys = jnp.concatenate([ys_ref[qt] for qt in range(SSM_WIDTH // LANES)], axis=1)

    l1, l4, l16 = l1_ref[...], l4_ref[...], l16_ref[...]
    mx = jnp.maximum(jnp.maximum(l1, l4), l16)
    e1, e4, e16 = jnp.exp(l1 - mx), jnp.exp(l4 - mx), jnp.exp(l16 - mx)
    inv = 1.0 / (e1 + e4 + e16)
    head_row = lax.broadcasted_iota(jnp.int32, (LANES, ATT_WIDTH), 0)
    head_col = lax.broadcasted_iota(jnp.int32, (LANES, ATT_WIDTH), 1) // HEAD_DIM
    spread = jnp.where(head_row == head_col, 1.0, 0.0).astype(BF16)

    def per_column(w):
        hi = w.astype(BF16)
        lo = (w - hi.astype(F32)).astype(BF16)
        return (jnp.dot(hi, spread, preferred_element_type=F32)
                + jnp.dot(lo, spread, preferred_element_type=F32))

    att = (per_column(e1 * inv) * o1_ref[...].astype(F32) + per_column(e4 * inv) * o4_ref[...].astype(F32)
           + per_column(e16 * inv) * o16_ref[...].astype(F32))
    y = ys + d_ref[...] * u_ref[...]
    y = 0.5 * y * (1.0 + jnp.tanh(math.sqrt(2.0 / math.pi) * (y + 0.044715 * (y * y * y))))
    gate = jnp.dot(y.astype(BF16), wg_ref[...], preferred_element_type=F32) + bg_ref[...]
    ssm = y * (1.0 / (1.0 + jnp.exp(-gate)))
    mixed = (jnp.dot(att.astype(BF16), wo_ref[:ATT_WIDTH], preferred_element_type=F32)
             + jnp.dot(ssm.astype(BF16), wo_ref[ATT_WIDTH:], preferred_element_type=F32))
    out_ref[...] = x_ref[...] + _rms(mixed, g_ref[...])


def _mix(o1, o4, o16, l1, l4, l16, yg, u, x2, d, wg, bg, wo, g, tile):
    n = x2.shape[0]
    tok = lambda i: (i, 0)
    const = lambda i: (0, 0)
    half = pl.BlockSpec((tile, ATT_WIDTH), tok)
    stat = pl.BlockSpec((tile, LANES), tok)
    return pl.pallas_call(
        _mix_kernel,
        grid=(n // tile,),
        in_specs=[half] * 3 + [stat] * 3 + [
            pl.BlockSpec((N_GROUPS, tile // CHUNK, CW), lambda i: (0, i, 0)),
            half,
            pl.BlockSpec((tile, D_MODEL), tok),
            pl.BlockSpec((1, SSM_WIDTH), const),
            pl.BlockSpec((SSM_WIDTH, SSM_WIDTH), const),
            pl.BlockSpec((1, SSM_WIDTH), const),
            pl.BlockSpec((D_MODEL, D_MODEL), const),
            pl.BlockSpec((1, D_MODEL), const),
        ],
        out_specs=pl.BlockSpec((tile, D_MODEL), tok),
        out_shape=jax.ShapeDtypeStruct((n, D_MODEL), F32),
        scratch_shapes=[pltpu.VMEM((SSM_WIDTH // LANES, tile, LANES), F32)],
        compiler_params=_cparams(("arbitrary",)),
        name="mix",
    )(o1, o4, o16, l1, l4, l16, yg, u, x2, d, wg, bg, wo, g)


def _mlp_kernel(x_ref, gpre_ref, wu_ref, wd_ref, gpost_ref, out_ref, *, ff_chunk):
    x = x_ref[...]
    h = _rms(x, gpre_ref[...]).astype(BF16)
    acc = jnp.zeros(x.shape, F32)
    for f in range(D_FF // ff_chunk):
        sl = slice(f * ff_chunk, (f + 1) * ff_chunk)
        a = jnp.maximum(jnp.dot(h, wu_ref[:, sl], preferred_element_type=F32), 0.0)
        acc = acc + jnp.dot((a * a).astype(BF16), wd_ref[sl, :], preferred_element_type=F32)
    out_ref[...] = x + _rms(acc, gpost_ref[...])


def _mlp(x2, gpre, wu, wd, gpost, tile, ff_chunk):
    n = x2.shape[0]
    tok = lambda i: (i, 0)
    const = lambda i: (0, 0)
    return pl.pallas_call(
        functools.partial(_mlp_kernel, ff_chunk=ff_chunk),
        grid=(n // tile,),
        in_specs=[
            pl.BlockSpec((tile, D_MODEL), tok),
            pl.BlockSpec((1, D_MODEL), const),
            pl.BlockSpec((D_MODEL, D_FF), const),
            pl.BlockSpec((D_FF, D_MODEL), const),
            pl.BlockSpec((1, D_MODEL), const),
        ],
        out_specs=pl.BlockSpec((tile, D_MODEL), tok),
        out_shape=jax.ShapeDtypeStruct((n, D_MODEL), F32),
        compiler_params=_cparams(("arbitrary",)),
        name="mlp",
    )(x2, gpre, wu, wd, gpost)


def _rotary_tables(seq):
    half = HEAD_DIM // 2
    inv_freq = 1.0 / (ROPE_THETA ** (jnp.arange(half, dtype=F32) / half))
    ang = jnp.arange(seq, dtype=F32)[:, None] * inv_freq[None, :]
    cos, sin = jnp.cos(ang), jnp.sin(ang)
    return jnp.tile(cos, (1, 4)), jnp.tile(jnp.concatenate([-sin, sin], axis=-1), (1, 2))


def _prepare(seq, norm_mix_pre, w_in, a_re, a_im, log_step, b_re, b_im, c_re, c_im, d_skip, w_glu, b_glu,
             w_out, norm_mix_post, norm_mlp_pre, w_up, w_down, norm_mlp_post):
    row = lambda t: t.reshape(1, -1).astype(F32)
    return dict(
        tables=_rotary_tables(seq),
        s5=_s5_operators(a_re, a_im, log_step, b_re, b_im, c_re, c_im, seq // (CHUNK * N_SEG)),
        g_mix_pre=row(norm_mix_pre), w_in=w_in.astype(BF16), d_skip=row(d_skip), w_glu=w_glu.astype(BF16),
        b_glu=row(b_glu), w_out=w_out.astype(BF16), g_mix_post=row(norm_mix_post),
        g_mlp_pre=row(norm_mlp_pre), w_up=w_up.astype(BF16), w_down=w_down.astype(BF16),
        g_mlp_post=row(norm_mlp_post))


def _layer(x, p, tok_tile=512, attn_tile=2048):
    bsz, seq, _ = x.shape
    n = bsz * seq
    x2 = x.reshape(n, D_MODEL)
    cos_t, sin_t = p['tables']
    w1, cs, pw, dec = p['s5']

    q1, q4, q16, k1, k4, k16, v1, v4, v16, u, ug = _inproj(
        x2, p['g_mix_pre'], p['w_in'], cos_t, sin_t, seq, tok_tile)
    (o1, l1), (o4, l4), (o16, l16) = (
        _banded_attention(q1[None], k1[None], v1[None], 1, seq, attn_tile),
        _banded_attention(q4, k4, v4, 4, seq, attn_tile),
        _banded_attention(q16, k16, v16, 16, seq, attn_tile))
    yg = _s5_scan(ug, w1, cs, pw, dec, bsz)
    x1 = _mix(o1, o4, o16, l1, l4, l16, yg, u, x2, p['d_skip'], p['w_glu'], p['b_glu'], p['w_out'],
              p['g_mix_post'], tok_tile)
    y = _mlp(x1, p['g_mlp_pre'], p['w_up'], p['w_down'], p['g_mlp_post'], tok_tile, 1024)
    return y.reshape(bsz, seq, D_MODEL)


def kernel(x_prompt, x_sample, norm_mix_pre, w_in, ssm_a_re, ssm_a_im, ssm_log_step, ssm_b_re, ssm_b_im,
           ssm_c_re, ssm_c_im, ssm_d, w_glu, b_glu, w_out, norm_mix_post, norm_mlp_pre, w_up, w_down,
           norm_mlp_post):
    weights = (norm_mix_pre, w_in, ssm_a_re, ssm_a_im, ssm_log_step, ssm_b_re, ssm_b_im, ssm_c_re, ssm_c_im,
               ssm_d, w_glu, b_glu, w_out, norm_mix_post, norm_mlp_pre, w_up, w_down, norm_mlp_post)
    depth = norm_mix_pre.shape[0]
    prepared = {}

    def run(x):
        seq = x.shape[1]
        if seq not in prepared:
            prepared[seq] = [_prepare(seq, *(w[l] for w in weights)) for l in range(depth)]
        for l in range(depth):
            x = _layer(x, prepared[seq][l])
        return x

    return run(x_prompt), run(x_sample)
```

```python
import functools
import math

import numpy as np
import jax
import jax.numpy as jnp
from jax import lax
from jax.experimental import pallas as pl
from jax.experimental.pallas import tpu as pltpu

F32 = jnp.float32
BF16 = jnp.bfloat16

D_MODEL = 1024
ATT_WIDTH = 512
SSM_WIDTH = 512
HEAD_DIM = 64
N_HEADS = 8
DILATIONS = ((128, 1), (512, 4), (2048, 16))
RADIUS = 64
SSM_GROUP = 16
N_GROUPS = 32
SSM_STATE = 64
D_FF = 4096
IN_WIDTH = 2048
ROPE_THETA = 10000.0
NORM_EPS = 1e-6
MASK_VALUE = -1e30

CHUNK = 16
N_SEG = 8
LANES = 128
SLOTS = LANES // SSM_GROUP
CW = CHUNK * SSM_GROUP
VMEM_LIMIT = 56 * 1024 * 1024
HI = lax.Precision.HIGHEST
LOG2E = 1.4426950408889634
LN2 = 0.6931471805599453


def _cparams(sem):
    return pltpu.CompilerParams(dimension_semantics=sem, vmem_limit_bytes=VMEM_LIMIT)


def _slot_masks():
    lane = lax.broadcasted_iota(jnp.int32, (1, LANES), 1)
    return [(lane // SSM_GROUP) == s for s in range(SLOTS)]


def _step_of_slot():
    gam = np.arange(SLOTS)[:, None, None]
    hh = np.arange(CHUNK // SLOTS)[None, :, None]
    l = np.arange(SLOTS)[None, None, :]
    return (SLOTS * hh + (l - gam) % SLOTS).reshape(SLOTS, CHUNK)


def _inproj_kernel(x_ref, g_ref, w_ref, cos_ref, sin_ref,
                   q1_ref, q4_ref, q16_ref, k1_ref, k4_ref, k16_ref, v1_ref, v4_ref, v16_ref,
                   u_ref, ug_ref, rs_ref, st_ref):
    x = x_ref[...]
    tile = x.shape[0]
    ms = jnp.mean(x * x, axis=-1, keepdims=True)
    h = (x * lax.rsqrt(ms + NORM_EPS)) * g_ref[...]
    proj = jnp.dot(h.astype(BF16), w_ref[...], preferred_element_type=F32)
    cos = cos_ref[...]
    sin = sin_ref[...]
    lane = lax.broadcasted_iota(jnp.int32, cos.shape, 1)
    first_half = (lane & (HEAD_DIM // 2)) == 0
    n_lt = ATT_WIDTH // LANES

    def rot(t):
        partner = jnp.where(first_half, pltpu.roll(t, LANES - HEAD_DIM // 2, 1),
                            pltpu.roll(t, HEAD_DIM // 2, 1))
        return t * cos + partner * sin

    def spread(lane_tile, nat_ref, d4_ref, d16_ref):
        for c in range(n_lt):
            rs_ref[c] = lane_tile(c)
            nat_ref[:, c * LANES:(c + 1) * LANES] = rs_ref[c].astype(BF16)
        for c in range(n_lt):
            for r4 in range(4):
                blk = rs_ref[c, pl.ds(r4, tile // 4, stride=4), :]
                d4_ref[r4, :, c * LANES:(c + 1) * LANES] = blk.astype(BF16)
                st_ref[c * 4 + r4] = blk
        for c in range(n_lt):
            for r4 in range(4):
                for r2 in range(4):
                    blk = st_ref[c * 4 + r4, pl.ds(r2, tile // 16, stride=4), :]
                    d16_ref[r4 + 4 * r2, :, c * LANES:(c + 1) * LANES] = blk.astype(BF16)

    q_scale = (HEAD_DIM ** -0.5) * LOG2E
    spread(lambda c: rot(proj[:, c * LANES:(c + 1) * LANES]) * q_scale, q1_ref, q4_ref, q16_ref)
    spread(lambda c: rot(proj[:, ATT_WIDTH + c * LANES:ATT_WIDTH + (c + 1) * LANES]), k1_ref, k4_ref, k16_ref)
    spread(lambda c: proj[:, 2 * ATT_WIDTH + c * LANES:2 * ATT_WIDTH + (c + 1) * LANES], v1_ref, v4_ref, v16_ref)
    u = proj[:, 3 * ATT_WIDTH:]
    u_ref[...] = u

    n_chunk = tile // CHUNK
    masks = _slot_masks()
    for qt in range(SSM_WIDTH // LANES):
        rs_ref[qt] = u[:, qt * LANES:(qt + 1) * LANES]
    for qt in range(SSM_WIDTH // LANES):
        for hh in range(CHUNK // SLOTS):
            rolled = []
            for t8 in range(SLOTS):
                step_rows = rs_ref[qt, pl.ds(SLOTS * hh + t8, n_chunk, stride=CHUNK), :]
                rolled.append(step_rows if t8 == 0 else pltpu.roll(step_rows, SSM_GROUP * t8, 1))
            for gam in range(SLOTS):
                res = rolled[SLOTS - 1]
                for t8 in range(SLOTS - 2, -1, -1):
                    res = jnp.where(masks[(gam + t8) % SLOTS], rolled[t8], res)
                ug_ref[SLOTS * qt + gam, :, hh * LANES:(hh + 1) * LANES] = res.astype(BF16)


def _inproj(x2, g, w_bf, cos_t, sin_t, seq, tile):
    n = x2.shape[0]
    n_pos = seq // tile
    tok = lambda i: (i, 0)
    pos = lambda i: (i % n_pos, 0)
    const = lambda i: (0, 0)
    split = lambda i: (0, i, 0)
    qkv_specs = [pl.BlockSpec((tile, ATT_WIDTH), tok), pl.BlockSpec((4, tile // 4, ATT_WIDTH), split),
                 pl.BlockSpec((16, tile // 16, ATT_WIDTH), split)]
    qkv_shapes = [jax.ShapeDtypeStruct((n, ATT_WIDTH), BF16), jax.ShapeDtypeStruct((4, n // 4, ATT_WIDTH), BF16),
                  jax.ShapeDtypeStruct((16, n // 16, ATT_WIDTH), BF16)]
    return pl.pallas_call(
        _inproj_kernel,
        grid=(n // tile,),
        in_specs=[
            pl.BlockSpec((tile, D_MODEL), tok),
            pl.BlockSpec((1, D_MODEL), const),
            pl.BlockSpec((D_MODEL, IN_WIDTH), const),
            pl.BlockSpec((tile, LANES), pos),
            pl.BlockSpec((tile, LANES), pos),
        ],
        out_specs=qkv_specs * 3 + [pl.BlockSpec((tile, SSM_WIDTH), tok),
                                   pl.BlockSpec((N_GROUPS, tile // CHUNK, CW), split)],
        out_shape=qkv_shapes * 3 + [jax.ShapeDtypeStruct((n, SSM_WIDTH), F32),
                                    jax.ShapeDtypeStruct((N_GROUPS, n // CHUNK, CW), BF16)],
        scratch_shapes=[pltpu.VMEM((ATT_WIDTH // LANES, tile, LANES), F32),
                        pltpu.VMEM((4 * ATT_WIDTH // LANES, tile // 4, LANES), F32)],
        compiler_params=_cparams(("arbitrary",)),
        name="inproj",
    )(x2, g, w_bf, cos_t, sin_t)


def _attn_kernel(q_ref, kp_ref, kc_ref, kn_ref, vp_ref, vc_ref, vn_ref, o_ref, lse_ref,
                 kx_ref, vx_ref, bias_ref, s_ref, p_ref, m_ref, ost_ref, lst_ref, nat_ref, mid_ref,
                 *, dil, tm, qb, tiles_per_seq):
    tile_in_seq = pl.program_id(0) % tiles_per_seq
    n_sb = tm // qb
    if n_sb > 1:
        kx_ref[:, 0:RADIUS] = kp_ref[...]
        kx_ref[:, RADIUS:RADIUS + tm] = kc_ref[...]
        kx_ref[:, RADIUS + tm:] = kn_ref[...]
        vx_ref[:, 0:RADIUS] = vp_ref[...]
        vx_ref[:, RADIUS:RADIUS + tm] = vc_ref[...]
        vx_ref[:, RADIUS + tm:] = vn_ref[...]

    def window(prev_ref, cur_ref, next_ref, ext_ref, r, row0, sl):
        if n_sb > 1:
            return ext_ref[r, pl.ds(row0, win), sl]
        return jnp.concatenate([prev_ref[r, :, sl], cur_ref[r, :, sl], next_ref[r, :, sl]], axis=0)

    win = qb + 2 * RADIUS
    rows = 2 * qb
    qi = lax.broadcasted_iota(jnp.int32, (rows, win), 0) & (qb - 1)
    kcol = lax.broadcasted_iota(jnp.int32, (rows, win), 1)
    band = (kcol >= qi) & (kcol <= qi + 2 * RADIUS)
    after_start = kcol >= RADIUS
    before_end = kcol < qb + RADIUS
    neg = jnp.full((rows, win), MASK_VALUE, F32)
    zero = jnp.zeros((rows, win), F32)
    bias_ref[0] = jnp.where(band, zero, neg)
    bias_ref[1] = jnp.where(band & after_start, zero, neg)
    bias_ref[2] = jnp.where(band & before_end, zero, neg)
    bias_ref[3] = jnp.where(band & after_start & before_end, zero, neg)

    lane = lax.broadcasted_iota(jnp.int32, (qb, LANES), 1)
    head_a = lane < HEAD_DIM
    ones = jnp.ones((win, LANES), BF16)
    n_pair = N_HEADS // 2

    def item(w, carry):
        r = w // n_sb
        sb = w % n_sb
        row0 = pl.multiple_of(sb * qb, qb)
        at_start = jnp.logical_and(tile_in_seq == 0, sb == 0)
        at_end = jnp.logical_and(tile_in_seq == tiles_per_seq - 1, sb == n_sb - 1)
        bias = bias_ref[at_start.astype(jnp.int32) + 2 * at_end.astype(jnp.int32)]
        for p in range(n_pair):
            sl = slice(p * LANES, (p + 1) * LANES)
            q2 = q_ref[r, pl.ds(row0, qb), sl]
            zq = jnp.zeros_like(q2)
            qs = jnp.concatenate([jnp.where(head_a, q2, zq), jnp.where(head_a, zq, q2)], axis=0)
            kw = window(kp_ref, kc_ref, kn_ref, kx_ref, r, row0, sl)
            s_ref[p] = lax.dot_general(qs, kw, (((1,), (1,)), ((), ())), preferred_element_type=F32)
        for p in range(n_pair):
            s = s_ref[p] + bias
            m = jnp.max(s, axis=-1, keepdims=True)
            p_ref[p] = jnp.exp2(s - m).astype(BF16)
            m_ref[p] = jnp.broadcast_to(m, (rows, LANES))
        m8 = jnp.zeros((qb, LANES), F32)
        l8 = jnp.ones((qb, LANES), F32)
        for p in range(n_pair):
            sl = slice(p * LANES, (p + 1) * LANES)
            vaug = jnp.concatenate([window(vp_ref, vc_ref, vn_ref, vx_ref, r, row0, sl), ones], axis=1)
            res = jnp.dot(p_ref[p], vaug, preferred_element_type=F32)
            den = res[:, LANES:]
            mrow = m_ref[p]
            acc = jnp.where(head_a, res[:qb, :LANES], res[qb:, :LANES])
            ost_ref[r, pl.ds(row0, qb), sl] = acc / jnp.where(head_a, den[:qb], den[qb:])
            m8 = jnp.where(lane == 2 * p, mrow[:qb], jnp.where(lane == 2 * p + 1, mrow[qb:], m8))
            l8 = jnp.where(lane == 2 * p, den[:qb], jnp.where(lane == 2 * p + 1, den[qb:], l8))
        lst_ref[r, pl.ds(row0, qb), :] = m8 * LN2 + jnp.log(l8)
        return carry

    lax.fori_loop(0, dil * n_sb, item, 0, unroll=2)

    n_lt = ATT_WIDTH // LANES
    if dil == 1:
        o_ref[...] = ost_ref[0].astype(o_ref.dtype)
        lse_ref[...] = lst_ref[0]
    else:
        def gather(stage_ref, lt):
            sl = slice(lt * LANES, (lt + 1) * LANES)
            if dil == 4:
                for r in range(4):
                    nat_ref[pl.ds(r, tm, stride=4), :] = stage_ref[r, :, sl]
            else:
                for r4 in range(4):
                    for r2 in range(4):
                        mid_ref[r4, pl.ds(r2, tm, stride=4), :] = stage_ref[r4 + 4 * r2, :, sl]
                for r4 in range(4):
                    nat_ref[pl.ds(r4, 4 * tm, stride=4), :] = mid_ref[r4]
            return nat_ref[...]

        for lt in range(n_lt):
            o_ref[:, lt * LANES:(lt + 1) * LANES] = gather(ost_ref, lt).astype(o_ref.dtype)
        lse_ref[...] = gather(lst_ref, 0)


def _banded_attention(q, k, v, dil, seq, nat_tile):
    _, n_rows, _ = q.shape
    n = n_rows * dil
    nat_tile = min(nat_tile, seq)
    tm = nat_tile // dil
    qb = min(2 * RADIUS, tm)
    assert dil in (1, 4, 16) and seq % nat_tile == 0 and tm % qb == 0 and qb % RADIUS == 0
    hb = tm // RADIUS
    n_halo = n_rows // RADIUS
    win = qb + 2 * RADIUS
    ext_shape = (dil, tm + 2 * RADIUS, ATT_WIDTH) if tm > qb else (1, 16, LANES)
    cur = pl.BlockSpec((dil, tm, ATT_WIDTH), lambda i: (0, i, 0))
    prev = pl.BlockSpec((dil, RADIUS, ATT_WIDTH), lambda i: (0, jnp.maximum(i * hb - 1, 0), 0))
    nxt = pl.BlockSpec((dil, RADIUS, ATT_WIDTH), lambda i: (0, jnp.minimum((i + 1) * hb, n_halo - 1), 0))
    return pl.pallas_call(
        functools.partial(_attn_kernel, dil=dil, tm=tm, qb=qb, tiles_per_seq=seq // nat_tile),
        grid=(n // nat_tile,),
        in_specs=[cur, prev, cur, nxt, prev, cur, nxt],
        out_specs=[pl.BlockSpec((nat_tile, ATT_WIDTH), lambda i: (i, 0)),
                   pl.BlockSpec((nat_tile, LANES), lambda i: (i, 0))],
        out_shape=[jax.ShapeDtypeStruct((n, ATT_WIDTH), BF16), jax.ShapeDtypeStruct((n, LANES), F32)],
        scratch_shapes=[
            pltpu.VMEM(ext_shape, BF16),
            pltpu.VMEM(ext_shape, BF16),
            pltpu.VMEM((4, 2 * qb, win), F32),
            pltpu.VMEM((N_HEADS // 2, 2 * qb, win), F32),
            pltpu.VMEM((N_HEADS // 2, 2 * qb, win), BF16),
            pltpu.VMEM((N_HEADS // 2, 2 * qb, LANES), F32),
            pltpu.VMEM((dil, tm, ATT_WIDTH), F32),
            pltpu.VMEM((dil, tm, LANES), F32),
            pltpu.VMEM((nat_tile, LANES), F32),
            pltpu.VMEM((4, nat_tile // 4, LANES), F32),
        ],
        compiler_params=_cparams(("arbitrary",)),
        name=f"attn_d{dil}",
    )(q, k, k, k, v, v, v)


def _s5_kernel(ug_ref, w1_ref, cs_ref, pw_ref, dec_ref, y_ref, x_ref, h_ref, *, n_i, n_grp):
    half = LANES // 2
    for g in range(n_grp):
        z = jnp.dot(ug_ref[g], w1_ref[g], preferred_element_type=F32)
        y_ref[g] = z[:, :CW]
        for seg in range(N_SEG):
            rows = slice(seg * n_i, (seg + 1) * n_i)
            x_ref[2 * g, pl.ds(seg, n_i, stride=N_SEG), :] = z[rows, CW:CW + LANES]
            x_ref[2 * g + 1, pl.ds(seg, n_i, stride=N_SEG), :] = z[rows, CW + LANES:]
    lane = lax.broadcasted_iota(jnp.int32, (N_SEG, LANES), 1)
    fwd = lane < half
    a_re = [jnp.broadcast_to(dec_ref[g, 0:1, :LANES], (N_SEG, LANES)) for g in range(n_grp)]
    a_im = [jnp.broadcast_to(dec_ref[g, 0:1, LANES:], (N_SEG, LANES)) for g in range(n_grp)]

    def scan_body(s, carry):
        rf = pl.ds(pl.multiple_of(s * N_SEG, N_SEG), N_SEG)
        rb = pl.ds(pl.multiple_of((n_i - 1 - s) * N_SEG, N_SEG), N_SEG)
        out = []
        for g in range(n_grp):
            hre, him = carry[2 * g], carry[2 * g + 1]
            h_ref[2 * g, rf, 0:half] = hre[:, :half]
            h_ref[2 * g, rb, half:] = hre[:, half:]
            h_ref[2 * g + 1, rf, 0:half] = him[:, :half]
            h_ref[2 * g + 1, rb, half:] = him[:, half:]
            xre = jnp.where(fwd, x_ref[2 * g, rf, :], x_ref[2 * g, rb, :])
            xim = jnp.where(fwd, x_ref[2 * g + 1, rf, :], x_ref[2 * g + 1, rb, :])
            out += [a_re[g] * hre - a_im[g] * him + xre, a_re[g] * him + a_im[g] * hre + xim]
        return tuple(out)

    zero = jnp.zeros((N_SEG, LANES), F32)
    ends = lax.fori_loop(0, n_i, scan_body, (zero,) * (2 * n_grp))

    lane1 = lax.broadcasted_iota(jnp.int32, (1, LANES), 1)
    fwd1 = lane1 < half
    entry = []
    for g in range(n_grp):
        l_re, l_im = ends[2 * g], ends[2 * g + 1]
        s_re = dec_ref[g, 1:2, :LANES]
        s_im = dec_ref[g, 1:2, LANES:]
        e_re = jnp.zeros((1, LANES), F32)
        e_im = jnp.zeros((1, LANES), F32)
        ins = []
        for s in range(N_SEG):
            ins.append((e_re, e_im))
            lre = jnp.where(fwd1, l_re[s:s + 1], l_re[N_SEG - 1 - s:N_SEG - s])
            lim = jnp.where(fwd1, l_im[s:s + 1], l_im[N_SEG - 1 - s:N_SEG - s])
            e_re, e_im = s_re * e_re - s_im * e_im + lre, s_re * e_im + s_im * e_re + lim
        entry.append((
            jnp.concatenate([jnp.where(fwd1, ins[s][0], ins[N_SEG - 1 - s][0]) for s in range(N_SEG)], axis=0),
            jnp.concatenate([jnp.where(fwd1, ins[s][1], ins[N_SEG - 1 - s][1]) for s in range(N_SEG)], axis=0)))

    def fix_body(ib, c):
        for g in range(n_grp):
            ein_re, ein_im = entry[g]
            pw = pw_ref[g, pl.ds(pl.multiple_of(ib * 8, 8), 8), :]
            for r in range(8):
                rows = pl.ds(pl.multiple_of((ib * 8 + r) * N_SEG, N_SEG), N_SEG)
                p_re = pw[r:r + 1, :LANES]
                p_im = pw[r:r + 1, LANES:]
                h_ref[2 * g, rows, :] = h_ref[2 * g, rows, :] + (p_re * ein_re - p_im * ein_im)
                h_ref[2 * g + 1, rows, :] = h_ref[2 * g + 1, rows, :] + (p_re * ein_im + p_im * ein_re)
        return c

    lax.fori_loop(0, n_i // 8, fix_body, 0)
    for g in range(n_grp):
        ungroup = lambda c: jnp.concatenate(
            [h_ref[c, pl.ds(seg, n_i, stride=N_SEG), :] for seg in range(N_SEG)], axis=0)
        hcat = jnp.concatenate([ungroup(2 * g), ungroup(2 * g + 1)], axis=1).astype(BF16)
        y_ref[g] += jnp.dot(hcat, cs_ref[g], preferred_element_type=F32)


def _s5_scan(ug, w1, cs, pw, dec, bsz, n_grp=4):
    n_g, total_rows, width = ug.shape
    rows = total_rows // bsz
    n_i = rows // N_SEG
    grp = lambda b, g: (g, 0, 0)
    seq_rows = pl.BlockSpec((n_grp, rows, width), lambda b, g: (g, b, 0))
    return pl.pallas_call(
        functools.partial(_s5_kernel, n_i=n_i, n_grp=n_grp),
        grid=(bsz, n_g // n_grp),
        in_specs=[
            seq_rows,
            pl.BlockSpec((n_grp, width, 2 * width), grp),
            pl.BlockSpec((n_grp, width, width), grp),
            pl.BlockSpec((n_grp, n_i, width), grp),
            pl.BlockSpec((n_grp, 2, width), grp),
        ],
        out_specs=seq_rows,
        out_shape=jax.ShapeDtypeStruct((n_g, total_rows, width), F32),
        scratch_shapes=[pltpu.VMEM((2 * n_grp, rows, LANES), F32)] * 2,
        compiler_params=_cparams(("arbitrary",) * 2),
        name="s5_scan",
    )(ug, w1, cs, pw, dec)


def _s5_operators(a_re, a_im, log_step, b_re, b_im, c_re, c_im, n_i):
    a_re, a_im, b_re, b_im, c_re, c_im = (t.astype(F32) for t in (a_re, a_im, b_re, b_im, c_re, c_im))
    step = jnp.exp(log_step.astype(F32))[..., None]
    zr, zi = a_re * step, a_im * step

    def power(n):
        n = jnp.asarray(n, F32).reshape(n.shape + (1, 1, 1))
        mag = jnp.exp(n * zr)
        return mag * jnp.cos(n * zi), mag * jnp.sin(n * zi)

    abr, abi = power(np.array(1))
    den = a_re * a_re + a_im * a_im
    fr = ((abr - 1.0) * a_re + abi * a_im) / den
    fi = (abi * a_re - (abr - 1.0) * a_im) / den
    bbr = fr[..., None] * b_re - fi[..., None] * b_im
    bbi = fr[..., None] * b_im + fi[..., None] * b_re

    pr, pi = power(np.arange(CHUNK + 1))
    wr = pr[:CHUNK, ..., None] * bbr - pi[:CHUNK, ..., None] * bbi
    wi = pr[:CHUNK, ..., None] * bbi + pi[:CHUNK, ..., None] * bbr
    kern = (jnp.einsum('dgcp,tdgpe->dgtce', c_re, wr, precision=HI)
            - jnp.einsum('dgcp,tdgpe->dgtce', c_im, wi, precision=HI))

    step_of = _step_of_slot()
    n_q = N_GROUPS // SLOTS
    s_step = step_of[:, :, None]
    t_step = step_of[:, None, :]
    tau = np.arange(CHUNK)[None, :, None, None]
    sel_f = (t_step - s_step)[:, None] == tau
    sel_b = (s_step - t_step)[:, None] == tau
    sel = jnp.asarray(np.concatenate([sel_f, sel_b], axis=1), F32)
    kx = jnp.transpose(kern, (1, 0, 2, 3, 4)).reshape(n_q, SLOTS, 2 * CHUNK, SSM_GROUP, SSM_GROUP)
    m = jnp.einsum('yxst,qyxce->qysetc', sel, kx, precision=HI).reshape(N_GROUPS, CW, CW)

    onehot = lambda idx: jnp.asarray(idx[..., None] == np.arange(CHUNK + 1), F32)
    grouped = lambda t: t.reshape((t.shape[0], n_q, SLOTS) + t.shape[2:])
    take = lambda oh, t: jnp.einsum('ypn,nqyk->qypk', oh, grouped(t), precision=HI)

    def state_in(d, idx):
        ar, ai = take(onehot(idx), pr[:, d]), take(onehot(idx), pi[:, d])
        br = jnp.swapaxes(grouped(bbr[d][None])[0], -1, -2)
        bi = jnp.swapaxes(grouped(bbi[d][None])[0], -1, -2)
        re = ar[:, :, :, None, :] * br[:, :, None] - ai[:, :, :, None, :] * bi[:, :, None]
        im = ar[:, :, :, None, :] * bi[:, :, None] + ai[:, :, :, None, :] * br[:, :, None]
        return re.reshape(N_GROUPS, CW, SSM_STATE), im.reshape(N_GROUPS, CW, SSM_STATE)

    bf_re, bf_im = state_in(0, CHUNK - 1 - step_of)
    bb_re, bb_im = state_in(1, step_of)
    w1 = jnp.concatenate([m, bf_re, bb_re, bf_im, bb_im], axis=-1).astype(BF16)

    def state_out(d, idx):
        ar, ai = take(onehot(idx), pr[:, d]), take(onehot(idx), pi[:, d])
        cr = jnp.swapaxes(grouped(c_re[d][None])[0], -1, -2)
        ci = jnp.swapaxes(grouped(c_im[d][None])[0], -1, -2)
        ar, ai = jnp.swapaxes(ar, -1, -2)[..., None], jnp.swapaxes(ai, -1, -2)[..., None]
        re = cr[:, :, :, None, :] * ar - ci[:, :, :, None, :] * ai
        im = cr[:, :, :, None, :] * ai + ci[:, :, :, None, :] * ar
        return re.reshape(N_GROUPS, SSM_STATE, CW), im.reshape(N_GROUPS, SSM_STATE, CW)

    cf_re, cf_im = state_out(0, step_of + 1)
    cb_re, cb_im = state_out(1, CHUNK - step_of)
    cs = jnp.concatenate([cf_re, cb_re, -cf_im, -cb_im], axis=1).astype(BF16)

    i_idx = np.arange(n_i)
    pf_re, pf_im = power(CHUNK * i_idx)
    pb_re, pb_im = power(CHUNK * (n_i - 1 - i_idx))
    pw = jnp.concatenate([pf_re[:, 0], pb_re[:, 1], pf_im[:, 0], pb_im[:, 1]], axis=-1)
    pw = jnp.transpose(pw, (1, 0, 2))
    dr, di = power(np.array([CHUNK, CHUNK * n_i]))
    dec = jnp.concatenate([dr[:, 0], dr[:, 1], di[:, 0], di[:, 1]], axis=-1)
    dec = jnp.transpose(dec, (1, 0, 2))
    return w1, cs, pw, dec


def _rms(x, g):
    return (x * lax.rsqrt(jnp.mean(x * x, axis=-1, keepdims=True) + NORM_EPS)) * g


def _mix_kernel(o1_ref, o4_ref, o16_ref, l1_ref, l4_ref, l16_ref, yg_ref, u_ref, x_ref,
                d_ref, wg_ref, bg_ref, wo_ref, g_ref, out_ref, ys_ref):
    n_chunk = x_ref.shape[0] // CHUNK
    masks = _slot_masks()
    for qt in range(SSM_WIDTH // LANES):
        for hh in range(CHUNK // SLOTS):
            src = [yg_ref[SLOTS * qt + gam, :, hh * LANES:(hh + 1) * LANES] for gam in range(SLOTS)]
            for t8 in range(SLOTS):
                pre = src[SLOTS - 1]
                for gam in range(SLOTS - 2, -1, -1):
                    pre = jnp.where(masks[(gam + t8) % SLOTS], src[gam], pre)
                nat = pre if t8 == 0 else pltpu.roll(pre, LANES - SSM_GROUP * t8, 1)
                ys_ref[qt, pl.ds(SLOTS * hh + t8, n_chunk, stride=CHUNK), :] = nat
    ys = jnp.concatenate([ys_ref[qt] for qt in range(SSM_WIDTH // LANES)], axis=1)

    l1, l4, l16 = l1_ref[...], l4_ref[...], l16_ref[...]
    mx = jnp.maximum(jnp.maximum(l1, l4), l16)
    e1, e4, e16 = jnp.exp(l1 - mx), jnp.exp(l4 - mx), jnp.exp(l16 - mx)
    inv = 1.0 / (e1 + e4 + e16)
    head_row = lax.broadcasted_iota(jnp.int32, (LANES, ATT_WIDTH), 0)
    head_col = lax.broadcasted_iota(jnp.int32, (LANES, ATT_WIDTH), 1) // HEAD_DIM
    spread = jnp.where(head_row == head_col, 1.0, 0.0).astype(BF16)

    def per_column(w):
        hi = w.astype(BF16)
        lo = (w - hi.astype(F32)).astype(BF16)
        return (jnp.dot(hi, spread, preferred_element_type=F32)
                + jnp.dot(lo, spread, preferred_element_type=F32))

    att = (per_column(e1 * inv) * o1_ref[...].astype(F32) + per_column(e4 * inv) * o4_ref[...].astype(F32)
           + per_column(e16 * inv) * o16_ref[...].astype(F32))
    y = ys + d_ref[...] * u_ref[...]
    y = 0.5 * y * (1.0 + jnp.tanh(math.sqrt(2.0 / math.pi) * (y + 0.044715 * (y * y * y))))
    gate = jnp.dot(y.astype(BF16), wg_ref[...], preferred_element_type=F32) + bg_ref[...]
    ssm = y * (1.0 / (1.0 + jnp.exp(-gate)))
    mixed = (jnp.dot(att.astype(BF16), wo_ref[:ATT_WIDTH], preferred_element_type=F32)
             + jnp.dot(ssm.astype(BF16), wo_ref[ATT_WIDTH:], preferred_element_type=F32))
    out_ref[...] = x_ref[...] + _rms(mixed, g_ref[...])


def _mix(o1, o4, o16, l1, l4, l16, yg, u, x2, d, wg, bg, wo, g, tile):
    n = x2.shape[0]
    tok = lambda i: (i, 0)
    const = lambda i: (0, 0)
    half = pl.BlockSpec((tile, ATT_WIDTH), tok)
    stat = pl.BlockSpec((tile, LANES), tok)
    return pl.pallas_call(
        _mix_kernel,
        grid=(n // tile,),
        in_specs=[half] * 3 + [stat] * 3 + [
            pl.BlockSpec((N_GROUPS, tile // CHUNK, CW), lambda i: (0, i, 0)),
            half,
            pl.BlockSpec((tile, D_MODEL), tok),
            pl.BlockSpec((1, SSM_WIDTH), const),
            pl.BlockSpec((SSM_WIDTH, SSM_WIDTH), const),
            pl.BlockSpec((1, SSM_WIDTH), const),
            pl.BlockSpec((D_MODEL, D_MODEL), const),
            pl.BlockSpec((1, D_MODEL), const),
        ],
        out_specs=pl.BlockSpec((tile, D_MODEL), tok),
        out_shape=jax.ShapeDtypeStruct((n, D_MODEL), F32),
        scratch_shapes=[pltpu.VMEM((SSM_WIDTH // LANES, tile, LANES), F32)],
        compiler_params=_cparams(("arbitrary",)),
        name="mix",
    )(o1, o4, o16, l1, l4, l16, yg, u, x2, d, wg, bg, wo, g)


def _mlp_kernel(x_ref, gpre_ref, wu_ref, wd_ref, gpost_ref, out_ref, *, ff_chunk):
    x = x_ref[...]
    h = _rms(x, gpre_ref[...]).astype(BF16)
    acc = jnp.zeros(x.shape, F32)
    for f in range(D_FF // ff_chunk):
        sl = slice(f * ff_chunk, (f + 1) * ff_chunk)
        a = jnp.maximum(jnp.dot(h, wu_ref[:, sl], preferred_element_type=F32), 0.0)
        acc = acc + jnp.dot((a * a).astype(BF16), wd_ref[sl, :], preferred_element_type=F32)
    out_ref[...] = x + _rms(acc, gpost_ref[...])


def _mlp(x2, gpre, wu, wd, gpost, tile, ff_chunk):
    n = x2.shape[0]
    tok = lambda i: (i, 0)
    const = lambda i: (0, 0)
    return pl.pallas_call(
        functools.partial(_mlp_kernel, ff_chunk=ff_chunk),
        grid=(n // tile,),
        in_specs=[
            pl.BlockSpec((tile, D_MODEL), tok),
            pl.BlockSpec((1, D_MODEL), const),
            pl.BlockSpec((D_MODEL, D_FF), const),
            pl.BlockSpec((D_FF, D_MODEL), const),
            pl.BlockSpec((1, D_MODEL), const),
        ],
        out_specs=pl.BlockSpec((tile, D_MODEL), tok),
        out_shape=jax.ShapeDtypeStruct((n, D_MODEL), F32),
        compiler_params=_cparams(("arbitrary",)),
        name="mlp",
    )(x2, gpre, wu, wd, gpost)


def _rotary_tables(seq):
    half = HEAD_DIM // 2
    inv_freq = 1.0 / (ROPE_THETA ** (jnp.arange(half, dtype=F32) / half))
    ang = jnp.arange(seq, dtype=F32)[:, None] * inv_freq[None, :]
    cos, sin = jnp.cos(ang), jnp.sin(ang)
    return jnp.tile(cos, (1, 4)), jnp.tile(jnp.concatenate([-sin, sin], axis=-1), (1, 2))


def _prepare(seq, norm_mix_pre, w_in, a_re, a_im, log_step, b_re, b_im, c_re, c_im, d_skip, w_glu, b_glu,
             w_out, norm_mix_post, norm_mlp_pre, w_up, w_down, norm_mlp_post):
    row = lambda t: t.reshape(1, -1).astype(F32)
    return dict(
        tables=_rotary_tables(seq),
        s5=_s5_operators(a_re, a_im, log_step, b_re, b_im, c_re, c_im, seq // (CHUNK * N_SEG)),
        g_mix_pre=row(norm_mix_pre), w_in=w_in.astype(BF16), d_skip=row(d_skip), w_glu=w_glu.astype(BF16),
        b_glu=row(b_glu), w_out=w_out.astype(BF16), g_mix_post=row(norm_mix_post),
        g_mlp_pre=row(norm_mlp_pre), w_up=w_up.astype(BF16), w_down=w_down.astype(BF16),
        g_mlp_post=row(norm_mlp_post))


def _layer(x, p, tok_tile=512, attn_tile=2048):
    bsz, seq, _ = x.shape
    n = bsz * seq
    x2 = x.reshape(n, D_MODEL)
    cos_t, sin_t = p['tables']
    w1, cs, pw, dec = p['s5']

    q1, q4, q16, k1, k4, k16, v1, v4, v16, u, ug = _inproj(
        x2, p['g_mix_pre'], p['w_in'], cos_t, sin_t, seq, tok_tile)
    (o1, l1), (o4, l4), (o16, l16) = (
        _banded_attention(q1[None], k1[None], v1[None], 1, seq, attn_tile),
        _banded_attention(q4, k4, v4, 4, seq, attn_tile),
        _banded_attention(q16, k16, v16, 16, seq, attn_tile))
    yg = _s5_scan(ug, w1, cs, pw, dec, bsz)
    x1 = _mix(o1, o4, o16, l1, l4, l16, yg, u, x2, p['d_skip'], p['w_glu'], p['b_glu'], p['w_out'],
              p['g_mix_post'], tok_tile)
    y = _mlp(x1, p['g_mlp_pre'], p['w_up'], p['w_down'], p['g_mlp_post'], tok_tile, 1024)
    return y.reshape(bsz, seq, D_MODEL)


def kernel(x_prompt, x_sample, norm_mix_pre, w_in, ssm_a_re, ssm_a_im, ssm_log_step, ssm_b_re, ssm_b_im,
           ssm_c_re, ssm_c_im, ssm_d, w_glu, b_glu, w_out, norm_mix_post, norm_mlp_pre, w_up, w_down,
           norm_mlp_post):
    weights = (norm_mix_pre, w_in, ssm_a_re, ssm_a_im, ssm_log_step, ssm_b_re, ssm_b_im, ssm_c_re, ssm_c_im,
               ssm_d, w_glu, b_glu, w_out, norm_mix_post, norm_mlp_pre, w_up, w_down, norm_mlp_post)
    depth = norm_mix_pre.shape[0]
    prepared = {}

    def run(x):
        seq = x.shape[1]
        if seq not in prepared:
            prepared[seq] = [_prepare(seq, *(w[l] for w in weights)) for l in range(depth)]
        for l in range(depth):
            x = _layer(x, prepared[seq][l])
        return x

    return run(x_prompt), run(x_sample)
```

```python
import functools
import math

import numpy as np
import jax
import jax.numpy as jnp
from jax import lax
from jax.experimental import pallas as pl
from jax.experimental.pallas import tpu as pltpu

F32 = jnp.float32
BF16 = jnp.bfloat16

D_MODEL = 1024
ATT_WIDTH = 512
SSM_WIDTH = 512
HEAD_DIM = 64
N_HEADS = 8
DILATIONS = ((128, 1), (512, 4), (2048, 16))
RADIUS = 64
SSM_GROUP = 16
N_GROUPS = 32
SSM_STATE = 64
D_FF = 4096
IN_WIDTH = 2048
ROPE_THETA = 10000.0
NORM_EPS = 1e-6
MASK_VALUE = -1e30

CHUNK = 16
N_SEG = 8
LANES = 128
SLOTS = LANES // SSM_GROUP
CW = CHUNK * SSM_GROUP
VMEM_LIMIT = 56 * 1024 * 1024
HI = lax.Precision.HIGHEST
LOG2E = 1.4426950408889634
LN2 = 0.6931471805599453


def _cparams(sem):
    return pltpu.CompilerParams(dimension_semantics=sem, vmem_limit_bytes=VMEM_LIMIT)


def _slot_masks():
    lane = lax.broadcasted_iota(jnp.int32, (1, LANES), 1)
    return [(lane // SSM_GROUP) == s for s in range(SLOTS)]


def _step_of_slot():
    gam = np.arange(SLOTS)[:, None, None]
    hh = np.arange(CHUNK // SLOTS)[None, :, None]
    l = np.arange(SLOTS)[None, None, :]
    return (SLOTS * hh + (l - gam) % SLOTS).reshape(SLOTS, CHUNK)


def _inproj_kernel(x_ref, g_ref, w_ref, cos_ref, sin_ref,
                   q1_ref, q4_ref, q16_ref, k1_ref, k4_ref, k16_ref, v1_ref, v4_ref, v16_ref,
                   u_ref, ug_ref, rs_ref, st_ref):
    x = x_ref[...]
    tile = x.shape[0]
    ms = jnp.mean(x * x, axis=-1, keepdims=True)
    h = (x * lax.rsqrt(ms + NORM_EPS)) * g_ref[...]
    proj = jnp.dot(h.astype(BF16), w_ref[...], preferred_element_type=F32)
    cos = cos_ref[...]
    sin = sin_ref[...]
    lane = lax.broadcasted_iota(jnp.int32, cos.shape, 1)
    first_half = (lane & (HEAD_DIM // 2)) == 0
    n_lt = ATT_WIDTH // LANES

    def rot(t):
        partner = jnp.where(first_half, pltpu.roll(t, LANES - HEAD_DIM // 2, 1),
                            pltpu.roll(t, HEAD_DIM // 2, 1))
        return t * cos + partner * sin

    def spread(lane_tile, nat_ref, d4_ref, d16_ref):
        for c in range(n_lt):
            rs_ref[c] = lane_tile(c)
            nat_ref[:, c * LANES:(c + 1) * LANES] = rs_ref[c].astype(BF16)
        for c in range(n_lt):
            for r4 in range(4):
                blk = rs_ref[c, pl.ds(r4, tile // 4, stride=4), :]
                d4_ref[r4, :, c * LANES:(c + 1) * LANES] = blk.astype(BF16)
                st_ref[c * 4 + r4] = blk
        for c in range(n_lt):
            for r4 in range(4):
                for r2 in range(4):
                    blk = st_ref[c * 4 + r4, pl.ds(r2, tile // 16, stride=4), :]
                    d16_ref[r4 + 4 * r2, :, c * LANES:(c + 1) * LANES] = blk.astype(BF16)

    q_scale = (HEAD_DIM ** -0.5) * LOG2E
    spread(lambda c: rot(proj[:, c * LANES:(c + 1) * LANES]) * q_scale, q1_ref, q4_ref, q16_ref)
    spread(lambda c: rot(proj[:, ATT_WIDTH + c * LANES:ATT_WIDTH + (c + 1) * LANES]), k1_ref, k4_ref, k16_ref)
    spread(lambda c: proj[:, 2 * ATT_WIDTH + c * LANES:2 * ATT_WIDTH + (c + 1) * LANES], v1_ref, v4_ref, v16_ref)
    u = proj[:, 3 * ATT_WIDTH:]
    u_ref[...] = u

    n_chunk = tile // CHUNK
    masks = _slot_masks()
    for qt in range(SSM_WIDTH // LANES):
        rs_ref[qt] = u[:, qt * LANES:(qt + 1) * LANES]
    for qt in range(SSM_WIDTH // LANES):
        for hh in range(CHUNK // SLOTS):
            rolled = []
            for t8 in range(SLOTS):
                step_rows = rs_ref[qt, pl.ds(SLOTS * hh + t8, n_chunk, stride=CHUNK), :]
                rolled.append(step_rows if t8 == 0 else pltpu.roll(step_rows, SSM_GROUP * t8, 1))
            for gam in range(SLOTS):
                res = rolled[SLOTS - 1]
                for t8 in range(SLOTS - 2, -1, -1):
                    res = jnp.where(masks[(gam + t8) % SLOTS], rolled[t8], res)
                ug_ref[SLOTS * qt + gam, :, hh * LANES:(hh + 1) * LANES] = res.astype(BF16)


def _inproj(x2, g, w_bf, cos_t, sin_t, seq, tile):
    n = x2.shape[0]
    n_pos = seq // tile
    tok = lambda i: (i, 0)
    pos = lambda i: (i % n_pos, 0)
    const = lambda i: (0, 0)
    split = lambda i: (0, i, 0)
    qkv_specs = [pl.BlockSpec((tile, ATT_WIDTH), tok), pl.BlockSpec((4, tile // 4, ATT_WIDTH), split),
                 pl.BlockSpec((16, tile // 16, ATT_WIDTH), split)]
    qkv_shapes = [jax.ShapeDtypeStruct((n, ATT_WIDTH), BF16), jax.ShapeDtypeStruct((4, n // 4, ATT_WIDTH), BF16),
                  jax.ShapeDtypeStruct((16, n // 16, ATT_WIDTH), BF16)]
    return pl.pallas_call(
        _inproj_kernel,
        grid=(n // tile,),
        in_specs=[
            pl.BlockSpec((tile, D_MODEL), tok),
            pl.BlockSpec((1, D_MODEL), const),
            pl.BlockSpec((D_MODEL, IN_WIDTH), const),
            pl.BlockSpec((tile, LANES), pos),
            pl.BlockSpec((tile, LANES), pos),
        ],
        out_specs=qkv_specs * 3 + [pl.BlockSpec((tile, SSM_WIDTH), tok),
                                   pl.BlockSpec((N_GROUPS, tile // CHUNK, CW), split)],
        out_shape=qkv_shapes * 3 + [jax.ShapeDtypeStruct((n, SSM_WIDTH), F32),
                                    jax.ShapeDtypeStruct((N_GROUPS, n // CHUNK, CW), BF16)],
        scratch_shapes=[pltpu.VMEM((ATT_WIDTH // LANES, tile, LANES), F32),
                        pltpu.VMEM((4 * ATT_WIDTH // LANES, tile // 4, LANES), F32)],
        compiler_params=_cparams(("arbitrary",)),
        name="inproj",
    )(x2, g, w_bf, cos_t, sin_t)


def _attn_kernel(q_ref, kp_ref, kc_ref, kn_ref, vp_ref, vc_ref, vn_ref, o_ref, lse_ref,
                 kx_ref, vx_ref, bias_ref, s_ref, p_ref, m_ref, ost_ref, lst_ref, nat_ref, mid_ref,
                 *, dil, tm, qb, tiles_per_seq):
    tile_in_seq = pl.program_id(0) % tiles_per_seq
    n_sb = tm // qb
    if n_sb > 1:
        kx_ref[:, 0:RADIUS] = kp_ref[...]
        kx_ref[:, RADIUS:RADIUS + tm] = kc_ref[...]
        kx_ref[:, RADIUS + tm:] = kn_ref[...]
        vx_ref[:, 0:RADIUS] = vp_ref[...]
        vx_ref[:, RADIUS:RADIUS + tm] = vc_ref[...]
        vx_ref[:, RADIUS + tm:] = vn_ref[...]

    def window(prev_ref, cur_ref, next_ref, ext_ref, r, row0, sl):
        if n_sb > 1:
            return ext_ref[r, row0:row0 + win, sl]
        return jnp.concatenate([prev_ref[r, :, sl], cur_ref[r, :, sl], next_ref[r, :, sl]], axis=0)

    win = qb + 2 * RADIUS
    rows = 2 * qb
    qi = lax.broadcasted_iota(jnp.int32, (rows, win), 0) & (qb - 1)
    kcol = lax.broadcasted_iota(jnp.int32, (rows, win), 1)
    band = (kcol >= qi) & (kcol <= qi + 2 * RADIUS)
    after_start = kcol >= RADIUS
    before_end = kcol < qb + RADIUS
    neg = jnp.full((rows, win), MASK_VALUE, F32)
    zero = jnp.zeros((rows, win), F32)
    bias_ref[0] = jnp.where(band, zero, neg)
    bias_ref[1] = jnp.where(band & after_start, zero, neg)
    bias_ref[2] = jnp.where(band & before_end, zero, neg)
    bias_ref[3] = jnp.where(band & after_start & before_end, zero, neg)
    first_tile = (tile_in_seq == 0).astype(jnp.int32)
    last_tile = (tile_in_seq == tiles_per_seq - 1).astype(jnp.int32)

    lane = lax.broadcasted_iota(jnp.int32, (qb, LANES), 1)
    head_a = lane < HEAD_DIM
    ones = jnp.ones((win, LANES), BF16)
    n_pair = N_HEADS // 2

    def item(r, sb, buf):
        row0 = sb * qb
        edge = (first_tile if sb == 0 else 0) + (2 * last_tile if sb == n_sb - 1 else 0)
        bias = bias_ref[edge]
        for p in range(n_pair):
            sl = slice(p * LANES, (p + 1) * LANES)
            q2 = q_ref[r, row0:row0 + qb, sl]
            zq = jnp.zeros_like(q2)
            qs = jnp.concatenate([jnp.where(head_a, q2, zq), jnp.where(head_a, zq, q2)], axis=0)
            kw = window(kp_ref, kc_ref, kn_ref, kx_ref, r, row0, sl)
            s_ref[buf + p] = lax.dot_general(qs, kw, (((1,), (1,)), ((), ())), preferred_element_type=F32)
        for p in range(n_pair):
            s = s_ref[buf + p] + bias
            m = jnp.max(s, axis=-1, keepdims=True)
            p_ref[buf + p] = jnp.exp2(s - m).astype(BF16)
            m_ref[buf + p] = jnp.broadcast_to(m, (rows, LANES))
        m8 = jnp.zeros((qb, LANES), F32)
        l8 = jnp.ones((qb, LANES), F32)
        for p in range(n_pair):
            sl = slice(p * LANES, (p + 1) * LANES)
            vaug = jnp.concatenate([window(vp_ref, vc_ref, vn_ref, vx_ref, r, row0, sl), ones], axis=1)
            res = jnp.dot(p_ref[buf + p], vaug, preferred_element_type=F32)
            den = res[:, LANES:]
            mrow = m_ref[buf + p]
            acc = jnp.where(head_a, res[:qb, :LANES], res[qb:, :LANES])
            ost_ref[r, row0:row0 + qb, sl] = acc / jnp.where(head_a, den[:qb], den[qb:])
            m8 = jnp.where(lane == 2 * p, mrow[:qb], jnp.where(lane == 2 * p + 1, mrow[qb:], m8))
            l8 = jnp.where(lane == 2 * p, den[:qb], jnp.where(lane == 2 * p + 1, den[qb:], l8))
        lst_ref[r, row0:row0 + qb, :] = m8 * LN2 + jnp.log(l8)

    for w in range(dil * n_sb):
        item(w // n_sb, w % n_sb, (w % 2) * n_pair)

    n_lt = ATT_WIDTH // LANES
    if dil == 1:
        o_ref[...] = ost_ref[0].astype(o_ref.dtype)
        lse_ref[...] = lst_ref[0]
    else:
        def gather(stage_ref, lt):
            sl = slice(lt * LANES, (lt + 1) * LANES)
            if dil == 4:
                for r in range(4):
                    nat_ref[pl.ds(r, tm, stride=4), :] = stage_ref[r, :, sl]
            else:
                for r4 in range(4):
                    for r2 in range(4):
                        mid_ref[r4, pl.ds(r2, tm, stride=4), :] = stage_ref[r4 + 4 * r2, :, sl]
                for r4 in range(4):
                    nat_ref[pl.ds(r4, 4 * tm, stride=4), :] = mid_ref[r4]
            return nat_ref[...]

        for lt in range(n_lt):
            o_ref[:, lt * LANES:(lt + 1) * LANES] = gather(ost_ref, lt).astype(o_ref.dtype)
        lse_ref[...] = gather(lst_ref, 0)


def _banded_attention(q, k, v, dil, seq, nat_tile):
    _, n_rows, _ = q.shape
    n = n_rows * dil
    nat_tile = min(nat_tile, seq)
    tm = nat_tile // dil
    qb = min(2 * RADIUS, tm)
    assert dil in (1, 4, 16) and seq % nat_tile == 0 and tm % qb == 0 and qb % RADIUS == 0
    hb = tm // RADIUS
    n_halo = n_rows // RADIUS
    win = qb + 2 * RADIUS
    ext_shape = (dil, tm + 2 * RADIUS, ATT_WIDTH) if tm > qb else (1, 16, LANES)
    cur = pl.BlockSpec((dil, tm, ATT_WIDTH), lambda i: (0, i, 0))
    prev = pl.BlockSpec((dil, RADIUS, ATT_WIDTH), lambda i: (0, jnp.maximum(i * hb - 1, 0), 0))
    nxt = pl.BlockSpec((dil, RADIUS, ATT_WIDTH), lambda i: (0, jnp.minimum((i + 1) * hb, n_halo - 1), 0))
    return pl.pallas_call(
        functools.partial(_attn_kernel, dil=dil, tm=tm, qb=qb, tiles_per_seq=seq // nat_tile),
        grid=(n // nat_tile,),
        in_specs=[cur, prev, cur, nxt, prev, cur, nxt],
        out_specs=[pl.BlockSpec((nat_tile, ATT_WIDTH), lambda i: (i, 0)),
                   pl.BlockSpec((nat_tile, LANES), lambda i: (i, 0))],
        out_shape=[jax.ShapeDtypeStruct((n, ATT_WIDTH), BF16), jax.ShapeDtypeStruct((n, LANES), F32)],
        scratch_shapes=[
            pltpu.VMEM(ext_shape, BF16),
            pltpu.VMEM(ext_shape, BF16),
            pltpu.VMEM((4, 2 * qb, win), F32),
            pltpu.VMEM((N_HEADS, 2 * qb, win), F32),
            pltpu.VMEM((N_HEADS, 2 * qb, win), BF16),
            pltpu.VMEM((N_HEADS, 2 * qb, LANES), F32),
            pltpu.VMEM((dil, tm, ATT_WIDTH), F32),
            pltpu.VMEM((dil, tm, LANES), F32),
            pltpu.VMEM((nat_tile, LANES), F32),
            pltpu.VMEM((4, nat_tile // 4, LANES), F32),
        ],
        compiler_params=_cparams(("arbitrary",)),
        name=f"attn_d{dil}",
    )(q, k, k, k, v, v, v)


def _s5_kernel(ug_ref, w1_ref, cs_ref, pw_ref, dec_ref, y_ref, x_ref, h_ref, *, n_i, n_grp):
    half = LANES // 2
    for g in range(n_grp):
        z = jnp.dot(ug_ref[g], w1_ref[g], preferred_element_type=F32)
        y_ref[g] = z[:, :CW]
        for seg in range(N_SEG):
            rows = slice(seg * n_i, (seg + 1) * n_i)
            x_ref[2 * g, pl.ds(seg, n_i, stride=N_SEG), :] = z[rows, CW:CW + LANES]
            x_ref[2 * g + 1, pl.ds(seg, n_i, stride=N_SEG), :] = z[rows, CW + LANES:]
    lane = lax.broadcasted_iota(jnp.int32, (N_SEG, LANES), 1)
    fwd = lane < half
    a_re = [jnp.broadcast_to(dec_ref[g, 0:1, :LANES], (N_SEG, LANES)) for g in range(n_grp)]
    a_im = [jnp.broadcast_to(dec_ref[g, 0:1, LANES:], (N_SEG, LANES)) for g in range(n_grp)]

    def scan_body(s, carry):
        rf = pl.ds(pl.multiple_of(s * N_SEG, N_SEG), N_SEG)
        rb = pl.ds(pl.multiple_of((n_i - 1 - s) * N_SEG, N_SEG), N_SEG)
        out = []
        for g in range(n_grp):
            hre, him = carry[2 * g], carry[2 * g + 1]
            h_ref[2 * g, rf, 0:half] = hre[:, :half]
            h_ref[2 * g, rb, half:] = hre[:, half:]
            h_ref[2 * g + 1, rf, 0:half] = him[:, :half]
            h_ref[2 * g + 1, rb, half:] = him[:, half:]
            xre = jnp.where(fwd, x_ref[2 * g, rf, :], x_ref[2 * g, rb, :])
            xim = jnp.where(fwd, x_ref[2 * g + 1, rf, :], x_ref[2 * g + 1, rb, :])
            out += [a_re[g] * hre - a_im[g] * him + xre, a_re[g] * him + a_im[g] * hre + xim]
        return tuple(out)

    zero = jnp.zeros((N_SEG, LANES), F32)
    ends = lax.fori_loop(0, n_i, scan_body, (zero,) * (2 * n_grp))

    lane1 = lax.broadcasted_iota(jnp.int32, (1, LANES), 1)
    fwd1 = lane1 < half
    entry = []
    for g in range(n_grp):
        l_re, l_im = ends[2 * g], ends[2 * g + 1]
        s_re = dec_ref[g, 1:2, :LANES]
        s_im = dec_ref[g, 1:2, LANES:]
        e_re = jnp.zeros((1, LANES), F32)
        e_im = jnp.zeros((1, LANES), F32)
        ins = []
        for s in range(N_SEG):
            ins.append((e_re, e_im))
            lre = jnp.where(fwd1, l_re[s:s + 1], l_re[N_SEG - 1 - s:N_SEG - s])
            lim = jnp.where(fwd1, l_im[s:s + 1], l_im[N_SEG - 1 - s:N_SEG - s])
            e_re, e_im = s_re * e_re - s_im * e_im + lre, s_re * e_im + s_im * e_re + lim
        entry.append((
            jnp.concatenate([jnp.where(fwd1, ins[s][0], ins[N_SEG - 1 - s][0]) for s in range(N_SEG)], axis=0),
            jnp.concatenate([jnp.where(fwd1, ins[s][1], ins[N_SEG - 1 - s][1]) for s in range(N_SEG)], axis=0)))

    def fix_body(ib, c):
        for g in range(n_grp):
            ein_re, ein_im = entry[g]
            pw = pw_ref[g, pl.ds(pl.multiple_of(ib * 8, 8), 8), :]
            for r in range(8):
                rows = pl.ds(pl.multiple_of((ib * 8 + r) * N_SEG, N_SEG), N_SEG)
                p_re = pw[r:r + 1, :LANES]
                p_im = pw[r:r + 1, LANES:]
                h_ref[2 * g, rows, :] = h_ref[2 * g, rows, :] + (p_re * ein_re - p_im * ein_im)
                h_ref[2 * g + 1, rows, :] = h_ref[2 * g + 1, rows, :] + (p_re * ein_im + p_im * ein_re)
        return c

    lax.fori_loop(0, n_i // 8, fix_body, 0)
    for g in range(n_grp):
        ungroup = lambda c: jnp.concatenate(
            [h_ref[c, pl.ds(seg, n_i, stride=N_SEG), :] for seg in range(N_SEG)], axis=0)
        hcat = jnp.concatenate([ungroup(2 * g), ungroup(2 * g + 1)], axis=1).astype(BF16)
        y_ref[g] += jnp.dot(hcat, cs_ref[g], preferred_element_type=F32)


def _s5_scan(ug, w1, cs, pw, dec, bsz, n_grp=4):
    n_g, total_rows, width = ug.shape
    rows = total_rows // bsz
    n_i = rows // N_SEG
    grp = lambda b, g: (g, 0, 0)
    seq_rows = pl.BlockSpec((n_grp, rows, width), lambda b, g: (g, b, 0))
    return pl.pallas_call(
        functools.partial(_s5_kernel, n_i=n_i, n_grp=n_grp),
        grid=(bsz, n_g // n_grp),
        in_specs=[
            seq_rows,
            pl.BlockSpec((n_grp, width, 2 * width), grp),
            pl.BlockSpec((n_grp, width, width), grp),
            pl.BlockSpec((n_grp, n_i, width), grp),
            pl.BlockSpec((n_grp, 2, width), grp),
        ],
        out_specs=seq_rows,
        out_shape=jax.ShapeDtypeStruct((n_g, total_rows, width), F32),
        scratch_shapes=[pltpu.VMEM((2 * n_grp, rows, LANES), F32)] * 2,
        compiler_params=_cparams(("arbitrary",) * 2),
        name="s5_scan",
    )(ug, w1, cs, pw, dec)


def _s5_operators(a_re, a_im, log_step, b_re, b_im, c_re, c_im, n_i):
    a_re, a_im, b_re, b_im, c_re, c_im = (t.astype(F32) for t in (a_re, a_im, b_re, b_im, c_re, c_im))
    step = jnp.exp(log_step.astype(F32))[..., None]
    zr, zi = a_re * step, a_im * step

    def power(n):
        n = jnp.asarray(n, F32).reshape(n.shape + (1, 1, 1))
        mag = jnp.exp(n * zr)
        return mag * jnp.cos(n * zi), mag * jnp.sin(n * zi)

    abr, abi = power(np.array(1))
    den = a_re * a_re + a_im * a_im
    fr = ((abr - 1.0) * a_re + abi * a_im) / den
    fi = (abi * a_re - (abr - 1.0) * a_im) / den
    bbr = fr[..., None] * b_re - fi[..., None] * b_im
    bbi = fr[..., None] * b_im + fi[..., None] * b_re

    pr, pi = power(np.arange(CHUNK + 1))
    wr = pr[:CHUNK, ..., None] * bbr - pi[:CHUNK, ..., None] * bbi
    wi = pr[:CHUNK, ..., None] * bbi + pi[:CHUNK, ..., None] * bbr
    kern = (jnp.einsum('dgcp,tdgpe->dgtce', c_re, wr, precision=HI)
            - jnp.einsum('dgcp,tdgpe->dgtce', c_im, wi, precision=HI))

    step_of = _step_of_slot()
    n_q = N_GROUPS // SLOTS
    s_step = step_of[:, :, None]
    t_step = step_of[:, None, :]
    tau = np.arange(CHUNK)[None, :, None, None]
    sel_f = (t_step - s_step)[:, None] == tau
    sel_b = (s_step - t_step)[:, None] == tau
    sel = jnp.asarray(np.concatenate([sel_f, sel_b], axis=1), F32)
    kx = jnp.transpose(kern, (1, 0, 2, 3, 4)).reshape(n_q, SLOTS, 2 * CHUNK, SSM_GROUP, SSM_GROUP)
    m = jnp.einsum('yxst,qyxce->qysetc', sel, kx, precision=HI).reshape(N_GROUPS, CW, CW)

    onehot = lambda idx: jnp.asarray(idx[..., None] == np.arange(CHUNK + 1), F32)
    grouped = lambda t: t.reshape((t.shape[0], n_q, SLOTS) + t.shape[2:])
    take = lambda oh, t: jnp.einsum('ypn,nqyk->qypk', oh, grouped(t), precision=HI)

    def state_in(d, idx):
        ar, ai = take(onehot(idx), pr[:, d]), take(onehot(idx), pi[:, d])
        br = jnp.swapaxes(grouped(bbr[d][None])[0], -1, -2)
        bi = jnp.swapaxes(grouped(bbi[d][None])[0], -1, -2)
        re = ar[:, :, :, None, :] * br[:, :, None] - ai[:, :, :, None, :] * bi[:, :, None]
        im = ar[:, :, :, None, :] * bi[:, :, None] + ai[:, :, :, None, :] * br[:, :, None]
        return re.reshape(N_GROUPS, CW, SSM_STATE), im.reshape(N_GROUPS, CW, SSM_STATE)

    bf_re, bf_im = state_in(0, CHUNK - 1 - step_of)
    bb_re, bb_im = state_in(1, step_of)
    w1 = jnp.concatenate([m, bf_re, bb_re, bf_im, bb_im], axis=-1).astype(BF16)

    def state_out(d, idx):
        ar, ai = take(onehot(idx), pr[:, d]), take(onehot(idx), pi[:, d])
        cr = jnp.swapaxes(grouped(c_re[d][None])[0], -1, -2)
        ci = jnp.swapaxes(grouped(c_im[d][None])[0], -1, -2)
        ar, ai = jnp.swapaxes(ar, -1, -2)[..., None], jnp.swapaxes(ai, -1, -2)[..., None]
        re = cr[:, :, :, None, :] * ar - ci[:, :, :, None, :] * ai
        im = cr[:, :, :, None, :] * ai + ci[:, :, :, None, :] * ar
        return re.reshape(N_GROUPS, SSM_STATE, CW), im.reshape(N_GROUPS, SSM_STATE, CW)

    cf_re, cf_im = state_out(0, step_of + 1)
    cb_re, cb_im = state_out(1, CHUNK - step_of)
    cs = jnp.concatenate([cf_re, cb_re, -cf_im, -cb_im], axis=1).astype(BF16)

    i_idx = np.arange(n_i)
    pf_re, pf_im = power(CHUNK * i_idx)
    pb_re, pb_im = power(CHUNK * (n_i - 1 - i_idx))
    pw = jnp.concatenate([pf_re[:, 0], pb_re[:, 1], pf_im[:, 0], pb_im[:, 1]], axis=-1)
    pw = jnp.transpose(pw, (1, 0, 2))
    dr, di = power(np.array([CHUNK, CHUNK * n_i]))
    dec = jnp.concatenate([dr[:, 0], dr[:, 1], di[:, 0], di[:, 1]], axis=-1)
    dec = jnp.transpose(dec, (1, 0, 2))
    return w1, cs, pw, dec


def _rms(x, g):
    return (x * lax.rsqrt(jnp.mean(x * x, axis=-1, keepdims=True) + NORM_EPS)) * g


def _mix_kernel(o1_ref, o4_ref, o16_ref, l1_ref, l4_ref, l16_ref, yg_ref, u_ref, x_ref,
                d_ref, wg_ref, bg_ref, wo_ref, g_ref, out_ref, ys_ref):
    n_chunk = x_ref.shape[0] // CHUNK
    masks = _slot_masks()
    for qt in range(SSM_WIDTH // LANES):
        for hh in range(CHUNK // SLOTS):
            src = [yg_ref[SLOTS * qt + gam, :, hh * LANES:(hh + 1) * LANES] for gam in range(SLOTS)]
            for t8 in range(SLOTS):
                pre = src[SLOTS - 1]
                for gam in range(SLOTS - 2, -1, -1):
                    pre = jnp.where(masks[(gam + t8) % SLOTS], src[gam], pre)
                nat = pre if t8 == 0 else pltpu.roll(pre, LANES - SSM_GROUP * t8, 1)
                ys_ref[qt, pl.ds(SLOTS * hh + t8, n_chunk, stride=CHUNK), :] = nat
    ys = jnp.concatenate([ys_ref[qt] for qt in range(SSM_WIDTH // LANES)], axis=1)

    l1, l4, l16 = l1_ref[...], l4_ref[...], l16_ref[...]
    mx = jnp.maximum(jnp.maximum(l1, l4), l16)
    e1, e4, e16 = jnp.exp(l1 - mx), jnp.exp(l4 - mx), jnp.exp(l16 - mx)
    inv = 1.0 / (e1 + e4 + e16)
    head_row = lax.broadcasted_iota(jnp.int32, (LANES, ATT_WIDTH), 0)
    head_col = lax.broadcasted_iota(jnp.int32, (LANES, ATT_WIDTH), 1) // HEAD_DIM
    spread = jnp.where(head_row == head_col, 1.0, 0.0).astype(BF16)

    def per_column(w):
        hi = w.astype(BF16)
        lo = (w - hi.astype(F32)).astype(BF16)
        return (jnp.dot(hi, spread, preferred_element_type=F32)
                + jnp.dot(lo, spread, preferred_element_type=F32))

    att = (per_column(e1 * inv) * o1_ref[...].astype(F32) + per_column(e4 * inv) * o4_ref[...].astype(F32)
           + per_column(e16 * inv) * o16_ref[...].astype(F32))
    y = ys + d_ref[...] * u_ref[...]
    y = 0.5 * y * (1.0 + jnp.tanh(math.sqrt(2.0 / math.pi) * (y + 0.044715 * (y * y * y))))
    gate = jnp.dot(y.astype(BF16), wg_ref[...], preferred_element_type=F32) + bg_ref[...]
    ssm = y * (1.0 / (1.0 + jnp.exp(-gate)))
    mixed = (jnp.dot(att.astype(BF16), wo_ref[:ATT_WIDTH], preferred_element_type=F32)
             + jnp.dot(ssm.astype(BF16), wo_ref[ATT_WIDTH:], preferred_element_type=F32))
    out_ref[...] = x_ref[...] + _rms(mixed, g_ref[...])


def _mix(o1, o4, o16, l1, l4, l16, yg, u, x2, d, wg, bg, wo, g, tile):
    n = x2.shape[0]
    tok = lambda i: (i, 0)
    const = lambda i: (0, 0)
    half = pl.BlockSpec((tile, ATT_WIDTH), tok)
    stat = pl.BlockSpec((tile, LANES), tok)
    return pl.pallas_call(
        _mix_kernel,
        grid=(n // tile,),
        in_specs=[half] * 3 + [stat] * 3 + [
            pl.BlockSpec((N_GROUPS, tile // CHUNK, CW), lambda i: (0, i, 0)),
            half,
            pl.BlockSpec((tile, D_MODEL), tok),
            pl.BlockSpec((1, SSM_WIDTH), const),
            pl.BlockSpec((SSM_WIDTH, SSM_WIDTH), const),
            pl.BlockSpec((1, SSM_WIDTH), const),
            pl.BlockSpec((D_MODEL, D_MODEL), const),
            pl.BlockSpec((1, D_MODEL), const),
        ],
        out_specs=pl.BlockSpec((tile, D_MODEL), tok),
        out_shape=jax.ShapeDtypeStruct((n, D_MODEL), F32),
        scratch_shapes=[pltpu.VMEM((SSM_WIDTH // LANES, tile, LANES), F32)],
        compiler_params=_cparams(("arbitrary",)),
        name="mix",
    )(o1, o4, o16, l1, l4, l16, yg, u, x2, d, wg, bg, wo, g)


def _mlp_kernel(x_ref, gpre_ref, wu_ref, wd_ref, gpost_ref, out_ref, *, ff_chunk):
    x = x_ref[...]
    h = _rms(x, gpre_ref[...]).astype(BF16)
    acc = jnp.zeros(x.shape, F32)
    for f in range(D_FF // ff_chunk):
        sl = slice(f * ff_chunk, (f + 1) * ff_chunk)
        a = jnp.maximum(jnp.dot(h, wu_ref[:, sl], preferred_element_type=F32), 0.0)
        acc = acc + jnp.dot((a * a).astype(BF16), wd_ref[sl, :], preferred_element_type=F32)
    out_ref[...] = x + _rms(acc, gpost_ref[...])


def _mlp(x2, gpre, wu, wd, gpost, tile, ff_chunk):
    n = x2.shape[0]
    tok = lambda i: (i, 0)
    const = lambda i: (0, 0)
    return pl.pallas_call(
        functools.partial(_mlp_kernel, ff_chunk=ff_chunk),
        grid=(n // tile,),
        in_specs=[
            pl.BlockSpec((tile, D_MODEL), tok),
            pl.BlockSpec((1, D_MODEL), const),
            pl.BlockSpec((D_MODEL, D_FF), const),
            pl.BlockSpec((D_FF, D_MODEL), const),
            pl.BlockSpec((1, D_MODEL), const),
        ],
        out_specs=pl.BlockSpec((tile, D_MODEL), tok),
        out_shape=jax.ShapeDtypeStruct((n, D_MODEL), F32),
        compiler_params=_cparams(("arbitrary",)),
        name="mlp",
    )(x2, gpre, wu, wd, gpost)


def _rotary_tables(seq):
    half = HEAD_DIM // 2
    inv_freq = 1.0 / (ROPE_THETA ** (jnp.arange(half, dtype=F32) / half))
    ang = jnp.arange(seq, dtype=F32)[:, None] * inv_freq[None, :]
    cos, sin = jnp.cos(ang), jnp.sin(ang)
    return jnp.tile(cos, (1, 4)), jnp.tile(jnp.concatenate([-sin, sin], axis=-1), (1, 2))


def _prepare(seq, norm_mix_pre, w_in, a_re, a_im, log_step, b_re, b_im, c_re, c_im, d_skip, w_glu, b_glu,
             w_out, norm_mix_post, norm_mlp_pre, w_up, w_down, norm_mlp_post):
    row = lambda t: t.reshape(1, -1).astype(F32)
    return dict(
        tables=_rotary_tables(seq),
        s5=_s5_operators(a_re, a_im, log_step, b_re, b_im, c_re, c_im, seq // (CHUNK * N_SEG)),
        g_mix_pre=row(norm_mix_pre), w_in=w_in.astype(BF16), d_skip=row(d_skip), w_glu=w_glu.astype(BF16),
        b_glu=row(b_glu), w_out=w_out.astype(BF16), g_mix_post=row(norm_mix_post),
        g_mlp_pre=row(norm_mlp_pre), w_up=w_up.astype(BF16), w_down=w_down.astype(BF16),
        g_mlp_post=row(norm_mlp_post))


def _layer(x, p, tok_tile=512, attn_tile=2048):
    bsz, seq, _ = x.shape
    n = bsz * seq
    x2 = x.reshape(n, D_MODEL)
    cos_t, sin_t = p['tables']
    w1, cs, pw, dec = p['s5']

    q1, q4, q16, k1, k4, k16, v1, v4, v16, u, ug = _inproj(
        x2, p['g_mix_pre'], p['w_in'], cos_t, sin_t, seq, tok_tile)
    (o1, l1), (o4, l4), (o16, l16) = (
        _banded_attention(q1[None], k1[None], v1[None], 1, seq, attn_tile),
        _banded_attention(q4, k4, v4, 4, seq, attn_tile),
        _banded_attention(q16, k16, v16, 16, seq, attn_tile))
    yg = _s5_scan(ug, w1, cs, pw, dec, bsz)
    x1 = _mix(o1, o4, o16, l1, l4, l16, yg, u, x2, p['d_skip'], p['w_glu'], p['b_glu'], p['w_out'],
              p['g_mix_post'], tok_tile)
    y = _mlp(x1, p['g_mlp_pre'], p['w_up'], p['w_down'], p['g_mlp_post'], tok_tile, 1024)
    return y.reshape(bsz, seq, D_MODEL)


def kernel(x_prompt, x_sample, norm_mix_pre, w_in, ssm_a_re, ssm_a_im, ssm_log_step, ssm_b_re, ssm_b_im,
           ssm_c_re, ssm_c_im, ssm_d, w_glu, b_glu, w_out, norm_mix_post, norm_mlp_pre, w_up, w_down,
           norm_mlp_post):
    weights = (norm_mix_pre, w_in, ssm_a_re, ssm_a_im, ssm_log_step, ssm_b_re, ssm_b_im, ssm_c_re, ssm_c_im,
               ssm_d, w_glu, b_glu, w_out, norm_mix_post, norm_mlp_pre, w_up, w_down, norm_mlp_post)
    depth = norm_mix_pre.shape[0]
    prepared = {}

    def run(x):
        seq = x.shape[1]
        if seq not in prepared:
            prepared[seq] = [_prepare(seq, *(w[l] for w in weights)) for l in range(depth)]
        for l in range(depth):
            x = _layer(x, prepared[seq][l])
        return x

    return run(x_prompt), run(x_sample)
```

```python
import functools
import math

import numpy as np
import jax
import jax.numpy as jnp
from jax import lax
from jax.experimental import pallas as pl
from jax.experimental.pallas import tpu as pltpu

F32 = jnp.float32
BF16 = jnp.bfloat16

D_MODEL = 1024
ATT_WIDTH = 512
SSM_WIDTH = 512
HEAD_DIM = 64
N_HEADS = 8
DILATIONS = ((128, 1), (512, 4), (2048, 16))
RADIUS = 64
SSM_GROUP = 16
N_GROUPS = 32
SSM_STATE = 64
D_FF = 4096
IN_WIDTH = 2048
ROPE_THETA = 10000.0
NORM_EPS = 1e-6
MASK_VALUE = -1e30

CHUNK = 16
N_SEG = 8
LANES = 128
SLOTS = LANES // SSM_GROUP
CW = CHUNK * SSM_GROUP
VMEM_LIMIT = 56 * 1024 * 1024
HI = lax.Precision.HIGHEST
LOG2E = 1.4426950408889634
LN2 = 0.6931471805599453


def _cparams(sem):
    return pltpu.CompilerParams(dimension_semantics=sem, vmem_limit_bytes=VMEM_LIMIT)


def _slot_masks():
    lane = lax.broadcasted_iota(jnp.int32, (1, LANES), 1)
    return [(lane // SSM_GROUP) == s for s in range(SLOTS)]


def _step_of_slot():
    gam = np.arange(SLOTS)[:, None, None]
    hh = np.arange(CHUNK // SLOTS)[None, :, None]
    l = np.arange(SLOTS)[None, None, :]
    return (SLOTS * hh + (l - gam) % SLOTS).reshape(SLOTS, CHUNK)


def _inproj_kernel(x_ref, g_ref, w_ref, cos_ref, sin_ref,
                   q1_ref, q4_ref, q16_ref, k1_ref, k4_ref, k16_ref, v1_ref, v4_ref, v16_ref,
                   u_ref, ug_ref, rs_ref, st_ref):
    x = x_ref[...]
    tile = x.shape[0]
    ms = jnp.mean(x * x, axis=-1, keepdims=True)
    h = (x * lax.rsqrt(ms + NORM_EPS)) * g_ref[...]
    proj = jnp.dot(h.astype(BF16), w_ref[...], preferred_element_type=F32)
    cos = cos_ref[...]
    sin = sin_ref[...]
    lane = lax.broadcasted_iota(jnp.int32, cos.shape, 1)
    first_half = (lane & (HEAD_DIM // 2)) == 0
    n_lt = ATT_WIDTH // LANES

    def rot(t):
        partner = jnp.where(first_half, pltpu.roll(t, LANES - HEAD_DIM // 2, 1),
                            pltpu.roll(t, HEAD_DIM // 2, 1))
        return t * cos + partner * sin

    def spread(lane_tile, nat_ref, d4_ref, d16_ref):
        for c in range(n_lt):
            rs_ref[c] = lane_tile(c)
            nat_ref[:, c * LANES:(c + 1) * LANES] = rs_ref[c].astype(BF16)
        for c in range(n_lt):
            for r4 in range(4):
                blk = rs_ref[c, pl.ds(r4, tile // 4, stride=4), :]
                d4_ref[r4, :, c * LANES:(c + 1) * LANES] = blk.astype(BF16)
                st_ref[c * 4 + r4] = blk
        for c in range(n_lt):
            for r4 in range(4):
                for r2 in range(4):
                    blk = st_ref[c * 4 + r4, pl.ds(r2, tile // 16, stride=4), :]
                    d16_ref[r4 + 4 * r2, :, c * LANES:(c + 1) * LANES] = blk.astype(BF16)

    q_scale = (HEAD_DIM ** -0.5) * LOG2E
    spread(lambda c: rot(proj[:, c * LANES:(c + 1) * LANES]) * q_scale, q1_ref, q4_ref, q16_ref)
    spread(lambda c: rot(proj[:, ATT_WIDTH + c * LANES:ATT_WIDTH + (c + 1) * LANES]), k1_ref, k4_ref, k16_ref)
    spread(lambda c: proj[:, 2 * ATT_WIDTH + c * LANES:2 * ATT_WIDTH + (c + 1) * LANES], v1_ref, v4_ref, v16_ref)
    u = proj[:, 3 * ATT_WIDTH:]
    u_ref[...] = u

    n_chunk = tile // CHUNK
    masks = _slot_masks()
    for qt in range(SSM_WIDTH // LANES):
        rs_ref[qt] = u[:, qt * LANES:(qt + 1) * LANES]
    for qt in range(SSM_WIDTH // LANES):
        for hh in range(CHUNK // SLOTS):
            rolled = []
            for t8 in range(SLOTS):
                step_rows = rs_ref[qt, pl.ds(SLOTS * hh + t8, n_chunk, stride=CHUNK), :]
                rolled.append(step_rows if t8 == 0 else pltpu.roll(step_rows, SSM_GROUP * t8, 1))
            for gam in range(SLOTS):
                res = rolled[SLOTS - 1]
                for t8 in range(SLOTS - 2, -1, -1):
                    res = jnp.where(masks[(gam + t8) % SLOTS], rolled[t8], res)
                ug_ref[SLOTS * qt + gam, :, hh * LANES:(hh + 1) * LANES] = res.astype(BF16)


def _inproj(x2, g, w_bf, cos_t, sin_t, seq, tile):
    n = x2.shape[0]
    n_pos = seq // tile
    tok = lambda i: (i, 0)
    pos = lambda i: (i % n_pos, 0)
    const = lambda i: (0, 0)
    split = lambda i: (0, i, 0)
    qkv_specs = [pl.BlockSpec((tile, ATT_WIDTH), tok), pl.BlockSpec((4, tile // 4, ATT_WIDTH), split),
                 pl.BlockSpec((16, tile // 16, ATT_WIDTH), split)]
    qkv_shapes = [jax.ShapeDtypeStruct((n, ATT_WIDTH), BF16), jax.ShapeDtypeStruct((4, n // 4, ATT_WIDTH), BF16),
                  jax.ShapeDtypeStruct((16, n // 16, ATT_WIDTH), BF16)]
    return pl.pallas_call(
        _inproj_kernel,
        grid=(n // tile,),
        in_specs=[
            pl.BlockSpec((tile, D_MODEL), tok),
            pl.BlockSpec((1, D_MODEL), const),
            pl.BlockSpec((D_MODEL, IN_WIDTH), const),
            pl.BlockSpec((tile, LANES), pos),
            pl.BlockSpec((tile, LANES), pos),
        ],
        out_specs=qkv_specs * 3 + [pl.BlockSpec((tile, SSM_WIDTH), tok),
                                   pl.BlockSpec((N_GROUPS, tile // CHUNK, CW), split)],
        out_shape=qkv_shapes * 3 + [jax.ShapeDtypeStruct((n, SSM_WIDTH), F32),
                                    jax.ShapeDtypeStruct((N_GROUPS, n // CHUNK, CW), BF16)],
        scratch_shapes=[pltpu.VMEM((ATT_WIDTH // LANES, tile, LANES), F32),
                        pltpu.VMEM((4 * ATT_WIDTH // LANES, tile // 4, LANES), F32)],
        compiler_params=_cparams(("arbitrary",)),
        name="inproj",
    )(x2, g, w_bf, cos_t, sin_t)


def _attn_kernel(q_ref, kp_ref, kc_ref, kn_ref, vp_ref, vc_ref, vn_ref, o_ref, lse_ref,
                 kx_ref, vx_ref, bias_ref, s_ref, p_ref, m_ref, ost_ref, lst_ref, nat_ref, mid_ref,
                 *, dil, tm, qb, tiles_per_seq):
    tile_in_seq = pl.program_id(0) % tiles_per_seq
    n_sb = tm // qb
    if n_sb > 1:
        kx_ref[:, 0:RADIUS] = kp_ref[...]
        kx_ref[:, RADIUS:RADIUS + tm] = kc_ref[...]
        kx_ref[:, RADIUS + tm:] = kn_ref[...]
        vx_ref[:, 0:RADIUS] = vp_ref[...]
        vx_ref[:, RADIUS:RADIUS + tm] = vc_ref[...]
        vx_ref[:, RADIUS + tm:] = vn_ref[...]

    def window(prev_ref, cur_ref, next_ref, ext_ref, r, row0, sl):
        if n_sb > 1:
            return ext_ref[r, row0:row0 + win, sl]
        return jnp.concatenate([prev_ref[r, :, sl], cur_ref[r, :, sl], next_ref[r, :, sl]], axis=0)

    win = qb + 2 * RADIUS
    rows = 2 * qb
    qi = lax.broadcasted_iota(jnp.int32, (rows, win), 0) & (qb - 1)
    kcol = lax.broadcasted_iota(jnp.int32, (rows, win), 1)
    band = (kcol >= qi) & (kcol <= qi + 2 * RADIUS)
    after_start = kcol >= RADIUS
    before_end = kcol < qb + RADIUS
    neg = jnp.full((rows, win), MASK_VALUE, F32)
    zero = jnp.zeros((rows, win), F32)
    bias_ref[0] = jnp.where(band, zero, neg)
    bias_ref[1] = jnp.where(band & after_start, zero, neg)
    bias_ref[2] = jnp.where(band & before_end, zero, neg)
    bias_ref[3] = jnp.where(band & after_start & before_end, zero, neg)
    first_tile = (tile_in_seq == 0).astype(jnp.int32)
    last_tile = (tile_in_seq == tiles_per_seq - 1).astype(jnp.int32)

    lane = lax.broadcasted_iota(jnp.int32, (qb, LANES), 1)
    head_a = lane < HEAD_DIM
    ones = jnp.ones((win, LANES), BF16)
    n_pair = N_HEADS // 2

    def item(r, sb, buf):
        row0 = sb * qb
        edge = (first_tile if sb == 0 else 0) + (2 * last_tile if sb == n_sb - 1 else 0)
        bias = bias_ref[edge]
        for p in range(n_pair):
            sl = slice(p * LANES, (p + 1) * LANES)
            q2 = q_ref[r, row0:row0 + qb, sl]
            zq = jnp.zeros_like(q2)
            qs = jnp.concatenate([jnp.where(head_a, q2, zq), jnp.where(head_a, zq, q2)], axis=0)
            kw = window(kp_ref, kc_ref, kn_ref, kx_ref, r, row0, sl)
            s_ref[buf + p] = lax.dot_general(qs, kw, (((1,), (1,)), ((), ())), preferred_element_type=F32)
        for p in range(n_pair):
            s = s_ref[buf + p] + bias
            m = jnp.max(s, axis=-1, keepdims=True)
            p_ref[buf + p] = jnp.exp2(s - m).astype(BF16)
            m_ref[buf + p] = jnp.broadcast_to(m, (rows, LANES))
        m8 = jnp.zeros((qb, LANES), F32)
        l8 = jnp.ones((qb, LANES), F32)
        for p in range(n_pair):
            sl = slice(p * LANES, (p + 1) * LANES)
            vaug = jnp.concatenate([window(vp_ref, vc_ref, vn_ref, vx_ref, r, row0, sl), ones], axis=1)
            res = jnp.dot(p_ref[buf + p], vaug, preferred_element_type=F32)
            den = res[:, LANES:]
            mrow = m_ref[buf + p]
            acc = jnp.where(head_a, res[:qb, :LANES], res[qb:, :LANES])
            ost_ref[r, row0:row0 + qb, sl] = acc / jnp.where(head_a, den[:qb], den[qb:])
            m8 = jnp.where(lane == 2 * p, mrow[:qb], jnp.where(lane == 2 * p + 1, mrow[qb:], m8))
            l8 = jnp.where(lane == 2 * p, den[:qb], jnp.where(lane == 2 * p + 1, den[qb:], l8))
        lst_ref[r, row0:row0 + qb, :] = m8 * LN2 + jnp.log(l8)

    for w in range(dil * n_sb):
        item(w // n_sb, w % n_sb, (w % 2) * n_pair)

    n_lt = ATT_WIDTH // LANES
    if dil == 1:
        o_ref[...] = ost_ref[0].astype(o_ref.dtype)
        lse_ref[...] = lst_ref[0]
    else:
        def gather(stage_ref, lt):
            sl = slice(lt * LANES, (lt + 1) * LANES)
            if dil == 4:
                for r in range(4):
                    nat_ref[pl.ds(r, tm, stride=4), :] = stage_ref[r, :, sl]
            else:
                for r4 in range(4):
                    for r2 in range(4):
                        mid_ref[r4, pl.ds(r2, tm, stride=4), :] = stage_ref[r4 + 4 * r2, :, sl]
                for r4 in range(4):
                    nat_ref[pl.ds(r4, 4 * tm, stride=4), :] = mid_ref[r4]
            return nat_ref[...]

        for lt in range(n_lt):
            o_ref[:, lt * LANES:(lt + 1) * LANES] = gather(ost_ref, lt).astype(o_ref.dtype)
        lse_ref[...] = gather(lst_ref, 0)


def _banded_attention(q, k, v, dil, seq, nat_tile):
    _, n_rows, _ = q.shape
    n = n_rows * dil
    nat_tile = min(nat_tile, seq)
    tm = nat_tile // dil
    qb = min(2 * RADIUS, tm)
    assert dil in (1, 4, 16) and seq % nat_tile == 0 and tm % qb == 0 and qb % RADIUS == 0
    hb = tm // RADIUS
    n_halo = n_rows // RADIUS
    win = qb + 2 * RADIUS
    ext_shape = (dil, tm + 2 * RADIUS, ATT_WIDTH) if tm > qb else (1, 16, LANES)
    cur = pl.BlockSpec((dil, tm, ATT_WIDTH), lambda i: (0, i, 0))
    prev = pl.BlockSpec((dil, RADIUS, ATT_WIDTH), lambda i: (0, jnp.maximum(i * hb - 1, 0), 0))
    nxt = pl.BlockSpec((dil, RADIUS, ATT_WIDTH), lambda i: (0, jnp.minimum((i + 1) * hb, n_halo - 1), 0))
    return pl.pallas_call(
        functools.partial(_attn_kernel, dil=dil, tm=tm, qb=qb, tiles_per_seq=seq // nat_tile),
        grid=(n // nat_tile,),
        in_specs=[cur, prev, cur, nxt, prev, cur, nxt],
        out_specs=[pl.BlockSpec((nat_tile, ATT_WIDTH), lambda i: (i, 0)),
                   pl.BlockSpec((nat_tile, LANES), lambda i: (i, 0))],
        out_shape=[jax.ShapeDtypeStruct((n, ATT_WIDTH), BF16), jax.ShapeDtypeStruct((n, LANES), F32)],
        scratch_shapes=[
            pltpu.VMEM(ext_shape, BF16),
            pltpu.VMEM(ext_shape, BF16),
            pltpu.VMEM((4, 2 * qb, win), F32),
            pltpu.VMEM((N_HEADS, 2 * qb, win), F32),
            pltpu.VMEM((N_HEADS, 2 * qb, win), BF16),
            pltpu.VMEM((N_HEADS, 2 * qb, LANES), F32),
            pltpu.VMEM((dil, tm, ATT_WIDTH), F32),
            pltpu.VMEM((dil, tm, LANES), F32),
            pltpu.VMEM((nat_tile, LANES), F32),
            pltpu.VMEM((4, nat_tile // 4, LANES), F32),
        ],
        compiler_params=_cparams(("arbitrary",)),
        name=f"attn_d{dil}",
    )(q, k, k, k, v, v, v)


def _s5_kernel(ug_ref, m_ref, bs_ref, cs_ref, pw_ref, dec_ref, y_ref, x_ref, h_ref, *, n_i, n_grp):
    half = LANES // 2
    for g in range(n_grp):
        y_ref[g] = jnp.dot(ug_ref[g], m_ref[g], preferred_element_type=F32)
        z = jnp.dot(ug_ref[g], bs_ref[g], preferred_element_type=F32)
        for seg in range(N_SEG):
            rows = slice(seg * n_i, (seg + 1) * n_i)
            x_ref[2 * g, pl.ds(seg, n_i, stride=N_SEG), :] = z[rows, :LANES]
            x_ref[2 * g + 1, pl.ds(seg, n_i, stride=N_SEG), :] = z[rows, LANES:]
    lane = lax.broadcasted_iota(jnp.int32, (N_SEG, LANES), 1)
    fwd = lane < half
    a_re = [jnp.broadcast_to(dec_ref[g, 0:1, :LANES], (N_SEG, LANES)) for g in range(n_grp)]
    a_im = [jnp.broadcast_to(dec_ref[g, 0:1, LANES:], (N_SEG, LANES)) for g in range(n_grp)]

    def scan_body(s, carry):
        rf = pl.ds(pl.multiple_of(s * N_SEG, N_SEG), N_SEG)
        rb = pl.ds(pl.multiple_of((n_i - 1 - s) * N_SEG, N_SEG), N_SEG)
        out = []
        for g in range(n_grp):
            hre, him = carry[2 * g], carry[2 * g + 1]
            h_ref[2 * g, rf, 0:half] = hre[:, :half]
            h_ref[2 * g, rb, half:] = hre[:, half:]
            h_ref[2 * g + 1, rf, 0:half] = him[:, :half]
            h_ref[2 * g + 1, rb, half:] = him[:, half:]
            xre = jnp.where(fwd, x_ref[2 * g, rf, :], x_ref[2 * g, rb, :])
            xim = jnp.where(fwd, x_ref[2 * g + 1, rf, :], x_ref[2 * g + 1, rb, :])
            out += [a_re[g] * hre - a_im[g] * him + xre, a_re[g] * him + a_im[g] * hre + xim]
        return tuple(out)

    zero = jnp.zeros((N_SEG, LANES), F32)
    ends = lax.fori_loop(0, n_i, scan_body, (zero,) * (2 * n_grp))

    lane1 = lax.broadcasted_iota(jnp.int32, (1, LANES), 1)
    fwd1 = lane1 < half
    entry = []
    for g in range(n_grp):
        l_re, l_im = ends[2 * g], ends[2 * g + 1]
        s_re = dec_ref[g, 1:2, :LANES]
        s_im = dec_ref[g, 1:2, LANES:]
        e_re = jnp.zeros((1, LANES), F32)
        e_im = jnp.zeros((1, LANES), F32)
        ins = []
        for s in range(N_SEG):
            ins.append((e_re, e_im))
            lre = jnp.where(fwd1, l_re[s:s + 1], l_re[N_SEG - 1 - s:N_SEG - s])
            lim = jnp.where(fwd1, l_im[s:s + 1], l_im[N_SEG - 1 - s:N_SEG - s])
            e_re, e_im = s_re * e_re - s_im * e_im + lre, s_re * e_im + s_im * e_re + lim
        entry.append((
            jnp.concatenate([jnp.where(fwd1, ins[s][0], ins[N_SEG - 1 - s][0]) for s in range(N_SEG)], axis=0),
            jnp.concatenate([jnp.where(fwd1, ins[s][1], ins[N_SEG - 1 - s][1]) for s in range(N_SEG)], axis=0)))

    def fix_body(ib, c):
        for g in range(n_grp):
            ein_re, ein_im = entry[g]
            pw = pw_ref[g, pl.ds(pl.multiple_of(ib * 8, 8), 8), :]
            for r in range(8):
                rows = pl.ds(pl.multiple_of((ib * 8 + r) * N_SEG, N_SEG), N_SEG)
                p_re = pw[r:r + 1, :LANES]
                p_im = pw[r:r + 1, LANES:]
                h_ref[2 * g, rows, :] = h_ref[2 * g, rows, :] + (p_re * ein_re - p_im * ein_im)
                h_ref[2 * g + 1, rows, :] = h_ref[2 * g + 1, rows, :] + (p_re * ein_im + p_im * ein_re)
        return c

    lax.fori_loop(0, n_i // 8, fix_body, 0)
    for g in range(n_grp):
        ungroup = lambda c: jnp.concatenate(
            [h_ref[c, pl.ds(seg, n_i, stride=N_SEG), :] for seg in range(N_SEG)], axis=0)
        hcat = jnp.concatenate([ungroup(2 * g), ungroup(2 * g + 1)], axis=1).astype(BF16)
        y_ref[g] += jnp.dot(hcat, cs_ref[g], preferred_element_type=F32)


def _s5_scan(ug, m, bs, cs, pw, dec, bsz, n_grp=4):
    n_g, total_rows, width = ug.shape
    rows = total_rows // bsz
    n_i = rows // N_SEG
    grp = lambda b, g: (g, 0, 0)
    seq_rows = pl.BlockSpec((n_grp, rows, width), lambda b, g: (g, b, 0))
    return pl.pallas_call(
        functools.partial(_s5_kernel, n_i=n_i, n_grp=n_grp),
        grid=(bsz, n_g // n_grp),
        in_specs=[
            seq_rows,
            pl.BlockSpec((n_grp, width, width), grp),
            pl.BlockSpec((n_grp, width, width), grp),
            pl.BlockSpec((n_grp, width, width), grp),
            pl.BlockSpec((n_grp, n_i, width), grp),
            pl.BlockSpec((n_grp, 2, width), grp),
        ],
        out_specs=seq_rows,
        out_shape=jax.ShapeDtypeStruct((n_g, total_rows, width), F32),
        scratch_shapes=[pltpu.VMEM((2 * n_grp, rows, LANES), F32)] * 2,
        compiler_params=_cparams(("arbitrary",) * 2),
        name="s5_scan",
    )(ug, m, bs, cs, pw, dec)


def _within_chunk_kernel(kin_ref, m_ref):
    step_of = _step_of_slot()
    masks = _slot_masks()
    for gam in range(SLOTS):
        rolled = {}

        def piece(x, slot):
            key = (x // SLOTS, (slot - x) % SLOTS)
            if key not in rolled:
                src = kin_ref[gam, :, key[0] * LANES:(key[0] + 1) * LANES]
                rolled[key] = src if key[1] == 0 else pltpu.roll(src, SSM_GROUP * key[1], 1)
            return rolled[key]

        for s_pos in range(CHUNK):
            for hh in range(CHUNK // SLOTS):
                lag = lambda l: int(step_of[gam, SLOTS * hh + l] - step_of[gam, s_pos]) + CHUNK - 1
                acc = piece(lag(SLOTS - 1), SLOTS - 1)
                for l in range(SLOTS - 2, -1, -1):
                    acc = jnp.where(masks[l], piece(lag(l), l), acc)
                m_ref[gam, s_pos * SSM_GROUP:(s_pos + 1) * SSM_GROUP, hh * LANES:(hh + 1) * LANES] = acc.astype(BF16)


def _within_chunk(kin):
    n_lag_lanes = kin.shape[-1]
    return pl.pallas_call(
        _within_chunk_kernel,
        grid=(N_GROUPS // SLOTS,),
        in_specs=[pl.BlockSpec((SLOTS, SSM_GROUP, n_lag_lanes), lambda i: (i, 0, 0))],
        out_specs=pl.BlockSpec((SLOTS, CW, CW), lambda i: (i, 0, 0)),
        out_shape=jax.ShapeDtypeStruct((N_GROUPS, CW, CW), BF16),
        compiler_params=_cparams(("arbitrary",)),
        name="s5_within_chunk",
    )(kin)


def _s5_operators(a_re, a_im, log_step, b_re, b_im, c_re, c_im, n_i):
    a_re, a_im, b_re, b_im, c_re, c_im = (t.astype(F32) for t in (a_re, a_im, b_re, b_im, c_re, c_im))
    step = jnp.exp(log_step.astype(F32))[..., None]
    zr, zi = a_re * step, a_im * step

    def power(n):
        n = jnp.asarray(n, F32).reshape(n.shape + (1, 1, 1))
        mag = jnp.exp(n * zr)
        return mag * jnp.cos(n * zi), mag * jnp.sin(n * zi)

    abr, abi = power(np.array(1))
    den = a_re * a_re + a_im * a_im
    fr = ((abr - 1.0) * a_re + abi * a_im) / den
    fi = (abi * a_re - (abr - 1.0) * a_im) / den
    bbr = fr[..., None] * b_re - fi[..., None] * b_im
    bbi = fr[..., None] * b_im + fi[..., None] * b_re

    pr, pi = power(np.arange(CHUNK + 1))
    wr = pr[:CHUNK, ..., None] * bbr - pi[:CHUNK, ..., None] * bbi
    wi = pr[:CHUNK, ..., None] * bbi + pi[:CHUNK, ..., None] * bbr
    kern = (jnp.einsum('dgcp,tdgpe->dgtce', c_re, wr, precision=HI)
            - jnp.einsum('dgcp,tdgpe->dgtce', c_im, wi, precision=HI))

    step_of = _step_of_slot()
    n_q = N_GROUPS // SLOTS
    lag = np.arange(2 * CHUNK - 1)[:, None] - (CHUNK - 1)
    tau = np.arange(CHUNK)[None, :]
    pick = jnp.asarray(np.stack([lag == tau, -lag == tau], axis=1), F32)
    kin = jnp.einsum('xdt,dgtce->gexc', pick, kern, precision=HI)
    kin = jnp.pad(kin.reshape(N_GROUPS, SSM_GROUP, (2 * CHUNK - 1) * SSM_GROUP),
                  ((0, 0), (0, 0), (0, SSM_GROUP)))
    m = _within_chunk(kin)

    onehot = lambda idx: jnp.asarray(idx[..., None] == np.arange(CHUNK + 1), F32)
    grouped = lambda t: t.reshape((t.shape[0], n_q, SLOTS) + t.shape[2:])
    take = lambda oh, t: jnp.einsum('ypn,nqyk->qypk', oh, grouped(t), precision=HI)

    def state_in(d, idx):
        ar, ai = take(onehot(idx), pr[:, d]), take(onehot(idx), pi[:, d])
        br = jnp.swapaxes(grouped(bbr[d][None])[0], -1, -2)
        bi = jnp.swapaxes(grouped(bbi[d][None])[0], -1, -2)
        re = ar[:, :, :, None, :] * br[:, :, None] - ai[:, :, :, None, :] * bi[:, :, None]
        im = ar[:, :, :, None, :] * bi[:, :, None] + ai[:, :, :, None, :] * br[:, :, None]
        return re.reshape(N_GROUPS, CW, SSM_STATE), im.reshape(N_GROUPS, CW, SSM_STATE)

    bf_re, bf_im = state_in(0, CHUNK - 1 - step_of)
    bb_re, bb_im = state_in(1, step_of)
    bs = jnp.concatenate([bf_re, bb_re, bf_im, bb_im], axis=-1).astype(BF16)

    def state_out(d, idx):
        ar, ai = take(onehot(idx), pr[:, d]), take(onehot(idx), pi[:, d])
        cr = jnp.swapaxes(grouped(c_re[d][None])[0], -1, -2)
        ci = jnp.swapaxes(grouped(c_im[d][None])[0], -1, -2)
        ar, ai = jnp.swapaxes(ar, -1, -2)[..., None], jnp.swapaxes(ai, -1, -2)[..., None]
        re = cr[:, :, :, None, :] * ar - ci[:, :, :, None, :] * ai
        im = cr[:, :, :, None, :] * ai + ci[:, :, :, None, :] * ar
        return re.reshape(N_GROUPS, SSM_STATE, CW), im.reshape(N_GROUPS, SSM_STATE, CW)

    cf_re, cf_im = state_out(0, step_of + 1)
    cb_re, cb_im = state_out(1, CHUNK - step_of)
    cs = jnp.concatenate([cf_re, cb_re, -cf_im, -cb_im], axis=1).astype(BF16)

    i_idx = np.arange(n_i)
    pf_re, pf_im = power(CHUNK * i_idx)
    pb_re, pb_im = power(CHUNK * (n_i - 1 - i_idx))
    pw = jnp.concatenate([pf_re[:, 0], pb_re[:, 1], pf_im[:, 0], pb_im[:, 1]], axis=-1)
    pw = jnp.transpose(pw, (1, 0, 2))
    dr, di = power(np.array([CHUNK, CHUNK * n_i]))
    dec = jnp.concatenate([dr[:, 0], dr[:, 1], di[:, 0], di[:, 1]], axis=-1)
    dec = jnp.transpose(dec, (1, 0, 2))
    return m, bs, cs, pw, dec


def _rms(x, g):
    return (x * lax.rsqrt(jnp.mean(x * x, axis=-1, keepdims=True) + NORM_EPS)) * g


def _mix_kernel(o1_ref, o4_ref, o16_ref, l1_ref, l4_ref, l16_ref, yg_ref, u_ref, x_ref,
                d_ref, wg_ref, bg_ref, wo_ref, g_ref, out_ref, ys_ref):
    n_chunk = x_ref.shape[0] // CHUNK
    masks = _slot_masks()
    for qt in range(SSM_WIDTH // LANES):
        for hh in range(CHUNK // SLOTS):
            src = [yg_ref[SLOTS * qt + gam, :, hh * LANES:(hh + 1) * LANES] for gam in range(SLOTS)]
            for t8 in range(SLOTS):
                pre = src[SLOTS - 1]
                for gam in range(SLOTS - 2, -1, -1):
                    pre = jnp.where(masks[(gam + t8) % SLOTS], src[gam], pre)
                nat = pre if t8 == 0 else pltpu.roll(pre, LANES - SSM_GROUP * t8, 1)
                ys_ref[qt, pl.ds(SLOTS * hh + t8, n_chunk, stride=CHUNK), :] = nat
    ys = jnp.concatenate([ys_ref[qt] for qt in range(SSM_WIDTH // LANES)], axis=1)

    l1, l4, l16 = l1_ref[...], l4_ref[...], l16_ref[...]
    mx = jnp.maximum(jnp.maximum(l1, l4), l16)
    e1, e4, e16 = jnp.exp(l1 - mx), jnp.exp(l4 - mx), jnp.exp(l16 - mx)
    inv = 1.0 / (e1 + e4 + e16)
    head_row = lax.broadcasted_iota(jnp.int32, (LANES, ATT_WIDTH), 0)
    head_col = lax.broadcasted_iota(jnp.int32, (LANES, ATT_WIDTH), 1) // HEAD_DIM
    spread = jnp.where(head_row == head_col, 1.0, 0.0).astype(BF16)

    def per_column(w):
        hi = w.astype(BF16)
        lo = (w - hi.astype(F32)).astype(BF16)
        return (jnp.dot(hi, spread, preferred_element_type=F32)
                + jnp.dot(lo, spread, preferred_element_type=F32))

    att = (per_column(e1 * inv) * o1_ref[...].astype(F32) + per_column(e4 * inv) * o4_ref[...].astype(F32)
           + per_column(e16 * inv) * o16_ref[...].astype(F32))
    y = ys + d_ref[...] * u_ref[...]
    y = 0.5 * y * (1.0 + jnp.tanh(math.sqrt(2.0 / math.pi) * (y + 0.044715 * (y * y * y))))
    gate = jnp.dot(y.astype(BF16), wg_ref[...], preferred_element_type=F32) + bg_ref[...]
    ssm = y * (1.0 / (1.0 + jnp.exp(-gate)))
    mixed = (jnp.dot(att.astype(BF16), wo_ref[:ATT_WIDTH], preferred_element_type=F32)
             + jnp.dot(ssm.astype(BF16), wo_ref[ATT_WIDTH:], preferred_element_type=F32))
    out_ref[...] = x_ref[...] + _rms(mixed, g_ref[...])


def _mix(o1, o4, o16, l1, l4, l16, yg, u, x2, d, wg, bg, wo, g, tile):
    n = x2.shape[0]
    tok = lambda i: (i, 0)
    const = lambda i: (0, 0)
    half = pl.BlockSpec((tile, ATT_WIDTH), tok)
    stat = pl.BlockSpec((tile, LANES), tok)
    return pl.pallas_call(
        _mix_kernel,
        grid=(n // tile,),
        in_specs=[half] * 3 + [stat] * 3 + [
            pl.BlockSpec((N_GROUPS, tile // CHUNK, CW), lambda i: (0, i, 0)),
            half,
            pl.BlockSpec((tile, D_MODEL), tok),
            pl.BlockSpec((1, SSM_WIDTH), const),
            pl.BlockSpec((SSM_WIDTH, SSM_WIDTH), const),
            pl.BlockSpec((1, SSM_WIDTH), const),
            pl.BlockSpec((D_MODEL, D_MODEL), const),
            pl.BlockSpec((1, D_MODEL), const),
        ],
        out_specs=pl.BlockSpec((tile, D_MODEL), tok),
        out_shape=jax.ShapeDtypeStruct((n, D_MODEL), F32),
        scratch_shapes=[pltpu.VMEM((SSM_WIDTH // LANES, tile, LANES), F32)],
        compiler_params=_cparams(("arbitrary",)),
        name="mix",
    )(o1, o4, o16, l1, l4, l16, yg, u, x2, d, wg, bg, wo, g)


def _mlp_kernel(x_ref, gpre_ref, wu_ref, wd_ref, gpost_ref, out_ref, *, ff_chunk):
    x = x_ref[...]
    h = _rms(x, gpre_ref[...]).astype(BF16)
    acc = jnp.zeros(x.shape, F32)
    for f in range(D_FF // ff_chunk):
        sl = slice(f * ff_chunk, (f + 1) * ff_chunk)
        a = jnp.maximum(jnp.dot(h, wu_ref[:, sl], preferred_element_type=F32), 0.0)
        acc = acc + jnp.dot((a * a).astype(BF16), wd_ref[sl, :], preferred_element_type=F32)
    out_ref[...] = x + _rms(acc, gpost_ref[...])


def _mlp(x2, gpre, wu, wd, gpost, tile, ff_chunk):
    n = x2.shape[0]
    tok = lambda i: (i, 0)
    const = lambda i: (0, 0)
    return pl.pallas_call(
        functools.partial(_mlp_kernel, ff_chunk=ff_chunk),
        grid=(n // tile,),
        in_specs=[
            pl.BlockSpec((tile, D_MODEL), tok),
            pl.BlockSpec((1, D_MODEL), const),
            pl.BlockSpec((D_MODEL, D_FF), const),
            pl.BlockSpec((D_FF, D_MODEL), const),
            pl.BlockSpec((1, D_MODEL), const),
        ],
        out_specs=pl.BlockSpec((tile, D_MODEL), tok),
        out_shape=jax.ShapeDtypeStruct((n, D_MODEL), F32),
        compiler_params=_cparams(("arbitrary",)),
        name="mlp",
    )(x2, gpre, wu, wd, gpost)


def _rotary_tables(seq):
    half = HEAD_DIM // 2
    inv_freq = 1.0 / (ROPE_THETA ** (jnp.arange(half, dtype=F32) / half))
    ang = jnp.arange(seq, dtype=F32)[:, None] * inv_freq[None, :]
    cos, sin = jnp.cos(ang), jnp.sin(ang)
    return jnp.tile(cos, (1, 4)), jnp.tile(jnp.concatenate([-sin, sin], axis=-1), (1, 2))


def _prepare(seq, norm_mix_pre, w_in, a_re, a_im, log_step, b_re, b_im, c_re, c_im, d_skip, w_glu, b_glu,
             w_out, norm_mix_post, norm_mlp_pre, w_up, w_down, norm_mlp_post):
    row = lambda t: t.reshape(1, -1).astype(F32)
    return dict(
        tables=_rotary_tables(seq),
        s5=_s5_operators(a_re, a_im, log_step, b_re, b_im, c_re, c_im, seq // (CHUNK * N_SEG)),
        g_mix_pre=row(norm_mix_pre), w_in=w_in.astype(BF16), d_skip=row(d_skip), w_glu=w_glu.astype(BF16),
        b_glu=row(b_glu), w_out=w_out.astype(BF16), g_mix_post=row(norm_mix_post),
        g_mlp_pre=row(norm_mlp_pre), w_up=w_up.astype(BF16), w_down=w_down.astype(BF16),
        g_mlp_post=row(norm_mlp_post))


def _layer(x, p, tok_tile=512, attn_tile=2048):
    bsz, seq, _ = x.shape
    n = bsz * seq
    x2 = x.reshape(n, D_MODEL)
    cos_t, sin_t = p['tables']
    m, bs, cs, pw, dec = p['s5']

    q1, q4, q16, k1, k4, k16, v1, v4, v16, u, ug = _inproj(
        x2, p['g_mix_pre'], p['w_in'], cos_t, sin_t, seq, tok_tile)
    (o1, l1), (o4, l4), (o16, l16) = (
        _banded_attention(q1[None], k1[None], v1[None], 1, seq, attn_tile),
        _banded_attention(q4, k4, v4, 4, seq, attn_tile),
        _banded_attention(q16, k16, v16, 16, seq, attn_tile))
    yg = _s5_scan(ug, m, bs, cs, pw, dec, bsz)
    x1 = _mix(o1, o4, o16, l1, l4, l16, yg, u, x2, p['d_skip'], p['w_glu'], p['b_glu'], p['w_out'],
              p['g_mix_post'], tok_tile)
    y = _mlp(x1, p['g_mlp_pre'], p['w_up'], p['w_down'], p['g_mlp_post'], tok_tile, 1024)
    return y.reshape(bsz, seq, D_MODEL)


def kernel(x_prompt, x_sample, norm_mix_pre, w_in, ssm_a_re, ssm_a_im, ssm_log_step, ssm_b_re, ssm_b_im,
           ssm_c_re, ssm_c_im, ssm_d, w_glu, b_glu, w_out, norm_mix_post, norm_mlp_pre, w_up, w_down,
           norm_mlp_post):
    weights = (norm_mix_pre, w_in, ssm_a_re, ssm_a_im, ssm_log_step, ssm_b_re, ssm_b_im, ssm_c_re, ssm_c_im,
               ssm_d, w_glu, b_glu, w_out, norm_mix_post, norm_mlp_pre, w_up, w_down, norm_mlp_post)
    depth = norm_mix_pre.shape[0]
    prepared = {}

    def run(x):
        seq = x.shape[1]
        if seq not in prepared:
            prepared[seq] = [_prepare(seq, *(w[l] for w in weights)) for l in range(depth)]
        for l in range(depth):
            x = _layer(x, prepared[seq][l])
        return x

    return run(x_prompt), run(x_sample)
```

```python
import functools
import math

import numpy as np
import jax
import jax.numpy as jnp
from jax import lax
from jax.experimental import pallas as pl
from jax.experimental.pallas import tpu as pltpu

F32 = jnp.float32
BF16 = jnp.bfloat16

D_MODEL = 1024
ATT_WIDTH = 512
SSM_WIDTH = 512
HEAD_DIM = 64
N_HEADS = 8
DILATIONS = ((128, 1), (512, 4), (2048, 16))
RADIUS = 64
SSM_GROUP = 16
N_GROUPS = 32
SSM_STATE = 64
D_FF = 4096
IN_WIDTH = 2048
ROPE_THETA = 10000.0
NORM_EPS = 1e-6
MASK_VALUE = -1e30

CHUNK = 16
N_SEG = 8
LANES = 128
SLOTS = LANES // SSM_GROUP
CW = CHUNK * SSM_GROUP
VMEM_LIMIT = 56 * 1024 * 1024
HI = lax.Precision.HIGHEST
LOG2E = 1.4426950408889634
LN2 = 0.6931471805599453


def _cparams(sem):
    return pltpu.CompilerParams(dimension_semantics=sem, vmem_limit_bytes=VMEM_LIMIT)


def _slot_masks():
    lane = lax.broadcasted_iota(jnp.int32, (1, LANES), 1)
    return [(lane // SSM_GROUP) == s for s in range(SLOTS)]


def _step_of_slot():
    gam = np.arange(SLOTS)[:, None, None]
    hh = np.arange(CHUNK // SLOTS)[None, :, None]
    l = np.arange(SLOTS)[None, None, :]
    return (SLOTS * hh + (l - gam) % SLOTS).reshape(SLOTS, CHUNK)


def _inproj_kernel(x_ref, g_ref, w_ref, cos_ref, sin_ref,
                   q1_ref, q4_ref, q16_ref, k1_ref, k4_ref, k16_ref, v1_ref, v4_ref, v16_ref,
                   u_ref, ug_ref, rs_ref, st_ref):
    x = x_ref[...]
    tile = x.shape[0]
    ms = jnp.mean(x * x, axis=-1, keepdims=True)
    h = (x * lax.rsqrt(ms + NORM_EPS)) * g_ref[...]
    proj = jnp.dot(h.astype(BF16), w_ref[...], preferred_element_type=F32)
    cos = cos_ref[...]
    sin = sin_ref[...]
    lane = lax.broadcasted_iota(jnp.int32, cos.shape, 1)
    first_half = (lane & (HEAD_DIM // 2)) == 0
    n_lt = ATT_WIDTH // LANES

    def rot(t):
        partner = jnp.where(first_half, pltpu.roll(t, LANES - HEAD_DIM // 2, 1),
                            pltpu.roll(t, HEAD_DIM // 2, 1))
        return t * cos + partner * sin

    def spread(lane_tile, nat_ref, d4_ref, d16_ref):
        for c in range(n_lt):
            rs_ref[c] = lane_tile(c)
            nat_ref[:, c * LANES:(c + 1) * LANES] = rs_ref[c].astype(BF16)
        for c in range(n_lt):
            for r4 in range(4):
                blk = rs_ref[c, pl.ds(r4, tile // 4, stride=4), :]
                d4_ref[r4, :, c * LANES:(c + 1) * LANES] = blk.astype(BF16)
                st_ref[c * 4 + r4] = blk
        for c in range(n_lt):
            for r4 in range(4):
                for r2 in range(4):
                    blk = st_ref[c * 4 + r4, pl.ds(r2, tile // 16, stride=4), :]
                    d16_ref[r4 + 4 * r2, :, c * LANES:(c + 1) * LANES] = blk.astype(BF16)

    q_scale = (HEAD_DIM ** -0.5) * LOG2E
    spread(lambda c: rot(proj[:, c * LANES:(c + 1) * LANES]) * q_scale, q1_ref, q4_ref, q16_ref)
    spread(lambda c: rot(proj[:, ATT_WIDTH + c * LANES:ATT_WIDTH + (c + 1) * LANES]), k1_ref, k4_ref, k16_ref)
    spread(lambda c: proj[:, 2 * ATT_WIDTH + c * LANES:2 * ATT_WIDTH + (c + 1) * LANES], v1_ref, v4_ref, v16_ref)
    u = proj[:, 3 * ATT_WIDTH:]
    u_ref[...] = u

    n_chunk = tile // CHUNK
    masks = _slot_masks()
    for qt in range(SSM_WIDTH // LANES):
        rs_ref[qt] = u[:, qt * LANES:(qt + 1) * LANES]
    for qt in range(SSM_WIDTH // LANES):
        for hh in range(CHUNK // SLOTS):
            rolled = []
            for t8 in range(SLOTS):
                step_rows = rs_ref[qt, pl.ds(SLOTS * hh + t8, n_chunk, stride=CHUNK), :]
                rolled.append(step_rows if t8 == 0 else pltpu.roll(step_rows, SSM_GROUP * t8, 1))
            for gam in range(SLOTS):
                res = rolled[SLOTS - 1]
                for t8 in range(SLOTS - 2, -1, -1):
                    res = jnp.where(masks[(gam + t8) % SLOTS], rolled[t8], res)
                ug_ref[SLOTS * qt + gam, :, hh * LANES:(hh + 1) * LANES] = res.astype(BF16)


def _inproj(x2, g, w_bf, cos_t, sin_t, seq, tile):
    n = x2.shape[0]
    n_pos = seq // tile
    tok = lambda i: (i, 0)
    pos = lambda i: (i % n_pos, 0)
    const = lambda i: (0, 0)
    split = lambda i: (0, i, 0)
    qkv_specs = [pl.BlockSpec((tile, ATT_WIDTH), tok), pl.BlockSpec((4, tile // 4, ATT_WIDTH), split),
                 pl.BlockSpec((16, tile // 16, ATT_WIDTH), split)]
    qkv_shapes = [jax.ShapeDtypeStruct((n, ATT_WIDTH), BF16), jax.ShapeDtypeStruct((4, n // 4, ATT_WIDTH), BF16),
                  jax.ShapeDtypeStruct((16, n // 16, ATT_WIDTH), BF16)]
    return pl.pallas_call(
        _inproj_kernel,
        grid=(n // tile,),
        in_specs=[
            pl.BlockSpec((tile, D_MODEL), tok),
            pl.BlockSpec((1, D_MODEL), const),
            pl.BlockSpec((D_MODEL, IN_WIDTH), const),
            pl.BlockSpec((tile, LANES), pos),
            pl.BlockSpec((tile, LANES), pos),
        ],
        out_specs=qkv_specs * 3 + [pl.BlockSpec((tile, SSM_WIDTH), tok),
                                   pl.BlockSpec((N_GROUPS, tile // CHUNK, CW), split)],
        out_shape=qkv_shapes * 3 + [jax.ShapeDtypeStruct((n, SSM_WIDTH), F32),
                                    jax.ShapeDtypeStruct((N_GROUPS, n // CHUNK, CW), BF16)],
        scratch_shapes=[pltpu.VMEM((ATT_WIDTH // LANES, tile, LANES), F32),
                        pltpu.VMEM((4 * ATT_WIDTH // LANES, tile // 4, LANES), F32)],
        compiler_params=_cparams(("arbitrary",)),
        name="inproj",
    )(x2, g, w_bf, cos_t, sin_t)


def _attn_kernel(q_ref, kp_ref, kc_ref, kn_ref, vp_ref, vc_ref, vn_ref, o_ref, lse_ref,
                 kx_ref, vx_ref, bias_ref, s_ref, p_ref, m_ref, ost_ref, lst_ref, nat_ref, mid_ref,
                 *, dil, tm, qb, tiles_per_seq):
    tile_in_seq = pl.program_id(0) % tiles_per_seq
    n_sb = tm // qb
    if n_sb > 1:
        kx_ref[:, 0:RADIUS] = kp_ref[...]
        kx_ref[:, RADIUS:RADIUS + tm] = kc_ref[...]
        kx_ref[:, RADIUS + tm:] = kn_ref[...]
        vx_ref[:, 0:RADIUS] = vp_ref[...]
        vx_ref[:, RADIUS:RADIUS + tm] = vc_ref[...]
        vx_ref[:, RADIUS + tm:] = vn_ref[...]

    def window(prev_ref, cur_ref, next_ref, ext_ref, r, row0, sl):
        if n_sb > 1:
            return ext_ref[r, row0:row0 + win, sl]
        return jnp.concatenate([prev_ref[r, :, sl], cur_ref[r, :, sl], next_ref[r, :, sl]], axis=0)

    win = qb + 2 * RADIUS
    rows = 2 * qb

    @pl.when(pl.program_id(0) == 0)
    def _():
        qi = lax.broadcasted_iota(jnp.int32, (rows, win), 0) & (qb - 1)
        kcol = lax.broadcasted_iota(jnp.int32, (rows, win), 1)
        band = (kcol >= qi) & (kcol <= qi + 2 * RADIUS)
        after_start = kcol >= RADIUS
        before_end = kcol < qb + RADIUS
        neg = jnp.full((rows, win), MASK_VALUE, F32)
        zero = jnp.zeros((rows, win), F32)
        bias_ref[0] = jnp.where(band, zero, neg)
        bias_ref[1] = jnp.where(band & after_start, zero, neg)
        bias_ref[2] = jnp.where(band & before_end, zero, neg)
        bias_ref[3] = jnp.where(band & after_start & before_end, zero, neg)

    first_tile = (tile_in_seq == 0).astype(jnp.int32)
    last_tile = (tile_in_seq == tiles_per_seq - 1).astype(jnp.int32)

    lane = lax.broadcasted_iota(jnp.int32, (qb, LANES), 1)
    head_a = lane < HEAD_DIM
    ones = jnp.ones((win, LANES), BF16)
    n_pair = N_HEADS // 2

    def item(r, sb, buf):
        row0 = sb * qb
        edge = (first_tile if sb == 0 else 0) + (2 * last_tile if sb == n_sb - 1 else 0)
        bias = bias_ref[edge]
        for p in range(n_pair):
            sl = slice(p * LANES, (p + 1) * LANES)
            q2 = q_ref[r, row0:row0 + qb, sl]
            zq = jnp.zeros_like(q2)
            qs = jnp.concatenate([jnp.where(head_a, q2, zq), jnp.where(head_a, zq, q2)], axis=0)
            kw = window(kp_ref, kc_ref, kn_ref, kx_ref, r, row0, sl)
            s_ref[buf + p] = lax.dot_general(qs, kw, (((1,), (1,)), ((), ())), preferred_element_type=F32)
        for p in range(n_pair):
            s = s_ref[buf + p] + bias
            m = jnp.max(s, axis=-1, keepdims=True)
            p_ref[buf + p] = jnp.exp2(s - m).astype(BF16)
            m_ref[buf + p] = jnp.broadcast_to(m, (rows, LANES))
        m8 = jnp.zeros((qb, LANES), F32)
        l8 = jnp.ones((qb, LANES), F32)
        for p in range(n_pair):
            sl = slice(p * LANES, (p + 1) * LANES)
            vaug = jnp.concatenate([window(vp_ref, vc_ref, vn_ref, vx_ref, r, row0, sl), ones], axis=1)
            res = jnp.dot(p_ref[buf + p], vaug, preferred_element_type=F32)
            den = res[:, LANES:]
            mrow = m_ref[buf + p]
            acc = jnp.where(head_a, res[:qb, :LANES], res[qb:, :LANES])
            out = acc / jnp.where(head_a, den[:qb], den[qb:])
            if dil == 1:
                o_ref[row0:row0 + qb, sl] = out.astype(o_ref.dtype)
            else:
                ost_ref[r, row0:row0 + qb, sl] = out
            m8 = jnp.where(lane == 2 * p, mrow[:qb], jnp.where(lane == 2 * p + 1, mrow[qb:], m8))
            l8 = jnp.where(lane == 2 * p, den[:qb], jnp.where(lane == 2 * p + 1, den[qb:], l8))
        lse = m8 * LN2 + jnp.log(l8)
        if dil == 1:
            lse_ref[row0:row0 + qb, :] = lse
        else:
            lst_ref[r, row0:row0 + qb, :] = lse

    for w in range(dil * n_sb):
        item(w // n_sb, w % n_sb, (w % 2) * n_pair)

    n_lt = ATT_WIDTH // LANES
    if dil > 1:
        def gather(stage_ref, lt):
            sl = slice(lt * LANES, (lt + 1) * LANES)
            if dil == 4:
                for r in range(4):
                    nat_ref[pl.ds(r, tm, stride=4), :] = stage_ref[r, :, sl]
            else:
                for r4 in range(4):
                    for r2 in range(4):
                        mid_ref[r4, pl.ds(r2, tm, stride=4), :] = stage_ref[r4 + 4 * r2, :, sl]
                for r4 in range(4):
                    nat_ref[pl.ds(r4, 4 * tm, stride=4), :] = mid_ref[r4]
            return nat_ref[...]

        for lt in range(n_lt):
            o_ref[:, lt * LANES:(lt + 1) * LANES] = gather(ost_ref, lt).astype(o_ref.dtype)
        lse_ref[...] = gather(lst_ref, 0)


def _banded_attention(q, k, v, dil, seq, nat_tile):
    _, n_rows, _ = q.shape
    n = n_rows * dil
    nat_tile = min(nat_tile, seq)
    tm = nat_tile // dil
    qb = min(2 * RADIUS, tm)
    assert dil in (1, 4, 16) and seq % nat_tile == 0 and tm % qb == 0 and qb % RADIUS == 0
    hb = tm // RADIUS
    n_halo = n_rows // RADIUS
    win = qb + 2 * RADIUS
    ext_shape = (dil, tm + 2 * RADIUS, ATT_WIDTH) if tm > qb else (1, 16, LANES)
    stage_rows = tm if dil > 1 else 8
    cur = pl.BlockSpec((dil, tm, ATT_WIDTH), lambda i: (0, i, 0))
    prev = pl.BlockSpec((dil, RADIUS, ATT_WIDTH), lambda i: (0, jnp.maximum(i * hb - 1, 0), 0))
    nxt = pl.BlockSpec((dil, RADIUS, ATT_WIDTH), lambda i: (0, jnp.minimum((i + 1) * hb, n_halo - 1), 0))
    return pl.pallas_call(
        functools.partial(_attn_kernel, dil=dil, tm=tm, qb=qb, tiles_per_seq=seq // nat_tile),
        grid=(n // nat_tile,),
        in_specs=[cur, prev, cur, nxt, prev, cur, nxt],
        out_specs=[pl.BlockSpec((nat_tile, ATT_WIDTH), lambda i: (i, 0)),
                   pl.BlockSpec((nat_tile, LANES), lambda i: (i, 0))],
        out_shape=[jax.ShapeDtypeStruct((n, ATT_WIDTH), BF16), jax.ShapeDtypeStruct((n, LANES), F32)],
        scratch_shapes=[
            pltpu.VMEM(ext_shape, BF16),
            pltpu.VMEM(ext_shape, BF16),
            pltpu.VMEM((4, 2 * qb, win), F32),
            pltpu.VMEM((N_HEADS, 2 * qb, win), F32),
            pltpu.VMEM((N_HEADS, 2 * qb, win), BF16),
            pltpu.VMEM((N_HEADS, 2 * qb, LANES), F32),
            pltpu.VMEM((dil, stage_rows, ATT_WIDTH), F32),
            pltpu.VMEM((dil, stage_rows, LANES), F32),
            pltpu.VMEM((dil * stage_rows, LANES), F32),
            pltpu.VMEM((4, dil * stage_rows // 4, LANES), F32),
        ],
        compiler_params=_cparams(("arbitrary",)),
        name=f"attn_d{dil}",
    )(q, k, k, k, v, v, v)


def _s5_kernel(ug_ref, m_ref, bs_ref, cs_ref, pw_ref, dec_ref, y_ref, x_ref, h_ref, *, n_i, n_grp):
    half = LANES // 2
    for g in range(n_grp):
        y_ref[g] = jnp.dot(ug_ref[g], m_ref[g], preferred_element_type=F32)
        z = jnp.dot(ug_ref[g], bs_ref[g], preferred_element_type=F32)
        for seg in range(N_SEG):
            rows = slice(seg * n_i, (seg + 1) * n_i)
            x_ref[2 * g, pl.ds(seg, n_i, stride=N_SEG), :] = z[rows, :LANES]
            x_ref[2 * g + 1, pl.ds(seg, n_i, stride=N_SEG), :] = z[rows, LANES:]
    lane = lax.broadcasted_iota(jnp.int32, (N_SEG, LANES), 1)
    fwd = lane < half
    a_re = [jnp.broadcast_to(dec_ref[g, 0:1, :LANES], (N_SEG, LANES)) for g in range(n_grp)]
    a_im = [jnp.broadcast_to(dec_ref[g, 0:1, LANES:], (N_SEG, LANES)) for g in range(n_grp)]

    def scan_body(s, carry):
        rf = pl.ds(pl.multiple_of(s * N_SEG, N_SEG), N_SEG)
        rb = pl.ds(pl.multiple_of((n_i - 1 - s) * N_SEG, N_SEG), N_SEG)
        out = []
        for g in range(n_grp):
            hre, him = carry[2 * g], carry[2 * g + 1]
            h_ref[2 * g, rf, 0:half] = hre[:, :half]
            h_ref[2 * g, rb, half:] = hre[:, half:]
            h_ref[2 * g + 1, rf, 0:half] = him[:, :half]
            h_ref[2 * g + 1, rb, half:] = him[:, half:]
            xre = jnp.where(fwd, x_ref[2 * g, rf, :], x_ref[2 * g, rb, :])
            xim = jnp.where(fwd, x_ref[2 * g + 1, rf, :], x_ref[2 * g + 1, rb, :])
            out += [a_re[g] * hre - a_im[g] * him + xre, a_re[g] * him + a_im[g] * hre + xim]
        return tuple(out)

    zero = jnp.zeros((N_SEG, LANES), F32)
    ends = lax.fori_loop(0, n_i, scan_body, (zero,) * (2 * n_grp))

    lane1 = lax.broadcasted_iota(jnp.int32, (1, LANES), 1)
    fwd1 = lane1 < half
    entry = []
    for g in range(n_grp):
        l_re, l_im = ends[2 * g], ends[2 * g + 1]
        s_re = dec_ref[g, 1:2, :LANES]
        s_im = dec_ref[g, 1:2, LANES:]
        e_re = jnp.zeros((1, LANES), F32)
        e_im = jnp.zeros((1, LANES), F32)
        ins = []
        for s in range(N_SEG):
            ins.append((e_re, e_im))
            lre = jnp.where(fwd1, l_re[s:s + 1], l_re[N_SEG - 1 - s:N_SEG - s])
            lim = jnp.where(fwd1, l_im[s:s + 1], l_im[N_SEG - 1 - s:N_SEG - s])
            e_re, e_im = s_re * e_re - s_im * e_im + lre, s_re * e_im + s_im * e_re + lim
        entry.append((
            jnp.concatenate([jnp.where(fwd1, ins[s][0], ins[N_SEG - 1 - s][0]) for s in range(N_SEG)], axis=0),
            jnp.concatenate([jnp.where(fwd1, ins[s][1], ins[N_SEG - 1 - s][1]) for s in range(N_SEG)], axis=0)))

    def fix_body(ib, c):
        for g in range(n_grp):
            ein_re, ein_im = entry[g]
            pw = pw_ref[g, pl.ds(pl.multiple_of(ib * 8, 8), 8), :]
            for r in range(8):
                rows = pl.ds(pl.multiple_of((ib * 8 + r) * N_SEG, N_SEG), N_SEG)
                p_re = pw[r:r + 1, :LANES]
                p_im = pw[r:r + 1, LANES:]
                h_ref[2 * g, rows, :] = h_ref[2 * g, rows, :] + (p_re * ein_re - p_im * ein_im)
                h_ref[2 * g + 1, rows, :] = h_ref[2 * g + 1, rows, :] + (p_re * ein_im + p_im * ein_re)
        return c

    lax.fori_loop(0, n_i // 8, fix_body, 0)
    for g in range(n_grp):
        ungroup = lambda c: jnp.concatenate(
            [h_ref[c, pl.ds(seg, n_i, stride=N_SEG), :] for seg in range(N_SEG)], axis=0)
        hcat = jnp.concatenate([ungroup(2 * g), ungroup(2 * g + 1)], axis=1).astype(BF16)
        y_ref[g] += jnp.dot(hcat, cs_ref[g], preferred_element_type=F32)


def _s5_scan(ug, m, bs, cs, pw, dec, bsz, n_grp=8):
    n_g, total_rows, width = ug.shape
    rows = total_rows // bsz
    n_i = rows // N_SEG
    grp = lambda b, g: (g, 0, 0)
    seq_rows = pl.BlockSpec((n_grp, rows, width), lambda b, g: (g, b, 0))
    return pl.pallas_call(
        functools.partial(_s5_kernel, n_i=n_i, n_grp=n_grp),
        grid=(bsz, n_g // n_grp),
        in_specs=[
            seq_rows,
            pl.BlockSpec((n_grp, width, width), grp),
            pl.BlockSpec((n_grp, width, width), grp),
            pl.BlockSpec((n_grp, width, width), grp),
            pl.BlockSpec((n_grp, n_i, width), grp),
            pl.BlockSpec((n_grp, 2, width), grp),
        ],
        out_specs=seq_rows,
        out_shape=jax.ShapeDtypeStruct((n_g, total_rows, width), F32),
        scratch_shapes=[pltpu.VMEM((2 * n_grp, rows, LANES), F32)] * 2,
        compiler_params=_cparams(("arbitrary",) * 2),
        name="s5_scan",
    )(ug, m, bs, cs, pw, dec)


def _within_chunk_kernel(kin_ref, m_ref):
    step_of = _step_of_slot()
    masks = _slot_masks()
    for gam in range(SLOTS):
        rolled = {}

        def piece(x, slot):
            key = (x // SLOTS, (slot - x) % SLOTS)
            if key not in rolled:
                src = kin_ref[gam, :, key[0] * LANES:(key[0] + 1) * LANES]
                rolled[key] = src if key[1] == 0 else pltpu.roll(src, SSM_GROUP * key[1], 1)
            return rolled[key]

        for s_pos in range(CHUNK):
            for hh in range(CHUNK // SLOTS):
                lag = lambda l: int(step_of[gam, SLOTS * hh + l] - step_of[gam, s_pos]) + CHUNK - 1
                acc = piece(lag(SLOTS - 1), SLOTS - 1)
                for l in range(SLOTS - 2, -1, -1):
                    acc = jnp.where(masks[l], piece(lag(l), l), acc)
                m_ref[gam, s_pos * SSM_GROUP:(s_pos + 1) * SSM_GROUP, hh * LANES:(hh + 1) * LANES] = acc.astype(BF16)


def _within_chunk(kin):
    n_lag_lanes = kin.shape[-1]
    return pl.pallas_call(
        _within_chunk_kernel,
        grid=(N_GROUPS // SLOTS,),
        in_specs=[pl.BlockSpec((SLOTS, SSM_GROUP, n_lag_lanes), lambda i: (i, 0, 0))],
        out_specs=pl.BlockSpec((SLOTS, CW, CW), lambda i: (i, 0, 0)),
        out_shape=jax.ShapeDtypeStruct((N_GROUPS, CW, CW), BF16),
        compiler_params=_cparams(("arbitrary",)),
        name="s5_within_chunk",
    )(kin)


def _s5_operators(a_re, a_im, log_step, b_re, b_im, c_re, c_im, n_i):
    a_re, a_im, b_re, b_im, c_re, c_im = (t.astype(F32) for t in (a_re, a_im, b_re, b_im, c_re, c_im))
    step = jnp.exp(log_step.astype(F32))[..., None]
    zr, zi = a_re * step, a_im * step

    def power(n):
        n = jnp.asarray(n, F32).reshape(n.shape + (1, 1, 1))
        mag = jnp.exp(n * zr)
        return mag * jnp.cos(n * zi), mag * jnp.sin(n * zi)

    abr, abi = power(np.array(1))
    den = a_re * a_re + a_im * a_im
    fr = ((abr - 1.0) * a_re + abi * a_im) / den
    fi = (abi * a_re - (abr - 1.0) * a_im) / den
    bbr = fr[..., None] * b_re - fi[..., None] * b_im
    bbi = fr[..., None] * b_im + fi[..., None] * b_re

    pr, pi = power(np.arange(CHUNK + 1))
    wr = pr[:CHUNK, ..., None] * bbr - pi[:CHUNK, ..., None] * bbi
    wi = pr[:CHUNK, ..., None] * bbi + pi[:CHUNK, ..., None] * bbr
    kern = (jnp.einsum('dgcp,tdgpe->dgtce', c_re, wr, precision=HI)
            - jnp.einsum('dgcp,tdgpe->dgtce', c_im, wi, precision=HI))

    step_of = _step_of_slot()
    n_q = N_GROUPS // SLOTS
    lag = np.arange(2 * CHUNK - 1)[:, None] - (CHUNK - 1)
    tau = np.arange(CHUNK)[None, :]
    pick = jnp.asarray(np.stack([lag == tau, -lag == tau], axis=1), F32)
    kin = jnp.einsum('xdt,dgtce->gexc', pick, kern, precision=HI)
    kin = jnp.pad(kin.reshape(N_GROUPS, SSM_GROUP, (2 * CHUNK - 1) * SSM_GROUP),
                  ((0, 0), (0, 0), (0, SSM_GROUP)))
    m = _within_chunk(kin)

    onehot = lambda idx: jnp.asarray(idx[..., None] == np.arange(CHUNK + 1), F32)
    grouped = lambda t: t.reshape((t.shape[0], n_q, SLOTS) + t.shape[2:])
    take = lambda oh, t: jnp.einsum('ypn,nqyk->qypk', oh, grouped(t), precision=HI)

    def state_in(d, idx):
        ar, ai = take(onehot(idx), pr[:, d]), take(onehot(idx), pi[:, d])
        br = jnp.swapaxes(grouped(bbr[d][None])[0], -1, -2)
        bi = jnp.swapaxes(grouped(bbi[d][None])[0], -1, -2)
        re = ar[:, :, :, None, :] * br[:, :, None] - ai[:, :, :, None, :] * bi[:, :, None]
        im = ar[:, :, :, None, :] * bi[:, :, None] + ai[:, :, :, None, :] * br[:, :, None]
        return re.reshape(N_GROUPS, CW, SSM_STATE), im.reshape(N_GROUPS, CW, SSM_STATE)

    bf_re, bf_im = state_in(0, CHUNK - 1 - step_of)
    bb_re, bb_im = state_in(1, step_of)
    bs = jnp.concatenate([bf_re, bb_re, bf_im, bb_im], axis=-1).astype(BF16)

    def state_out(d, idx):
        ar, ai = take(onehot(idx), pr[:, d]), take(onehot(idx), pi[:, d])
        cr = jnp.swapaxes(grouped(c_re[d][None])[0], -1, -2)
        ci = jnp.swapaxes(grouped(c_im[d][None])[0], -1, -2)
        ar, ai = jnp.swapaxes(ar, -1, -2)[..., None], jnp.swapaxes(ai, -1, -2)[..., None]
        re = cr[:, :, :, None, :] * ar - ci[:, :, :, None, :] * ai
        im = cr[:, :, :, None, :] * ai + ci[:, :, :, None, :] * ar
        return re.reshape(N_GROUPS, SSM_STATE, CW), im.reshape(N_GROUPS, SSM_STATE, CW)

    cf_re, cf_im = state_out(0, step_of + 1)
    cb_re, cb_im = state_out(1, CHUNK - step_of)
    cs = jnp.concatenate([cf_re, cb_re, -cf_im, -cb_im], axis=1).astype(BF16)

    i_idx = np.arange(n_i)
    pf_re, pf_im = power(CHUNK * i_idx)
    pb_re, pb_im = power(CHUNK * (n_i - 1 - i_idx))
    pw = jnp.concatenate([pf_re[:, 0], pb_re[:, 1], pf_im[:, 0], pb_im[:, 1]], axis=-1)
    pw = jnp.transpose(pw, (1, 0, 2))
    dr, di = power(np.array([CHUNK, CHUNK * n_i]))
    dec = jnp.concatenate([dr[:, 0], dr[:, 1], di[:, 0], di[:, 1]], axis=-1)
    dec = jnp.transpose(dec, (1, 0, 2))
    return m, bs, cs, pw, dec


def _rms(x, g):
    return (x * lax.rsqrt(jnp.mean(x * x, axis=-1, keepdims=True) + NORM_EPS)) * g


def _mix_kernel(o1_ref, o4_ref, o16_ref, l1_ref, l4_ref, l16_ref, yg_ref, u_ref, x_ref,
                d_ref, wg_ref, bg_ref, wo_ref, g_ref, out_ref, ys_ref):
    n_chunk = x_ref.shape[0] // CHUNK
    masks = _slot_masks()
    for qt in range(SSM_WIDTH // LANES):
        for hh in range(CHUNK // SLOTS):
            src = [yg_ref[SLOTS * qt + gam, :, hh * LANES:(hh + 1) * LANES] for gam in range(SLOTS)]
            for t8 in range(SLOTS):
                pre = src[SLOTS - 1]
                for gam in range(SLOTS - 2, -1, -1):
                    pre = jnp.where(masks[(gam + t8) % SLOTS], src[gam], pre)
                nat = pre if t8 == 0 else pltpu.roll(pre, LANES - SSM_GROUP * t8, 1)
                ys_ref[qt, pl.ds(SLOTS * hh + t8, n_chunk, stride=CHUNK), :] = nat
    ys = jnp.concatenate([ys_ref[qt] for qt in range(SSM_WIDTH // LANES)], axis=1)

    l1, l4, l16 = l1_ref[...], l4_ref[...], l16_ref[...]
    mx = jnp.maximum(jnp.maximum(l1, l4), l16)
    e1, e4, e16 = jnp.exp(l1 - mx), jnp.exp(l4 - mx), jnp.exp(l16 - mx)
    inv = 1.0 / (e1 + e4 + e16)
    head_row = lax.broadcasted_iota(jnp.int32, (LANES, ATT_WIDTH), 0)
    head_col = lax.broadcasted_iota(jnp.int32, (LANES, ATT_WIDTH), 1) // HEAD_DIM
    spread = jnp.where(head_row == head_col, 1.0, 0.0).astype(BF16)

    def per_column(w):
        hi = w.astype(BF16)
        lo = (w - hi.astype(F32)).astype(BF16)
        return (jnp.dot(hi, spread, preferred_element_type=F32)
                + jnp.dot(lo, spread, preferred_element_type=F32))

    att = (per_column(e1 * inv) * o1_ref[...].astype(F32) + per_column(e4 * inv) * o4_ref[...].astype(F32)
           + per_column(e16 * inv) * o16_ref[...].astype(F32))
    y = ys + d_ref[...] * u_ref[...]
    y = 0.5 * y * (1.0 + jnp.tanh(math.sqrt(2.0 / math.pi) * (y + 0.044715 * (y * y * y))))
    gate = jnp.dot(y.astype(BF16), wg_ref[...], preferred_element_type=F32) + bg_ref[...]
    ssm = y * (1.0 / (1.0 + jnp.exp(-gate)))
    mixed = (jnp.dot(att.astype(BF16), wo_ref[:ATT_WIDTH], preferred_element_type=F32)
             + jnp.dot(ssm.astype(BF16), wo_ref[ATT_WIDTH:], preferred_element_type=F32))
    out_ref[...] = x_ref[...] + _rms(mixed, g_ref[...])


def _mix(o1, o4, o16, l1, l4, l16, yg, u, x2, d, wg, bg, wo, g, tile):
    n = x2.shape[0]
    tok = lambda i: (i, 0)
    const = lambda i: (0, 0)
    half = pl.BlockSpec((tile, ATT_WIDTH), tok)
    stat = pl.BlockSpec((tile, LANES), tok)
    return pl.pallas_call(
        _mix_kernel,
        grid=(n // tile,),
        in_specs=[half] * 3 + [stat] * 3 + [
            pl.BlockSpec((N_GROUPS, tile // CHUNK, CW), lambda i: (0, i, 0)),
            half,
            pl.BlockSpec((tile, D_MODEL), tok),
            pl.BlockSpec((1, SSM_WIDTH), const),
            pl.BlockSpec((SSM_WIDTH, SSM_WIDTH), const),
            pl.BlockSpec((1, SSM_WIDTH), const),
            pl.BlockSpec((D_MODEL, D_MODEL), const),
            pl.BlockSpec((1, D_MODEL), const),
        ],
        out_specs=pl.BlockSpec((tile, D_MODEL), tok),
        out_shape=jax.ShapeDtypeStruct((n, D_MODEL), F32),
        scratch_shapes=[pltpu.VMEM((SSM_WIDTH // LANES, tile, LANES), F32)],
        compiler_params=_cparams(("arbitrary",)),
        name="mix",
    )(o1, o4, o16, l1, l4, l16, yg, u, x2, d, wg, bg, wo, g)


def _mlp_kernel(x_ref, gpre_ref, wu_ref, wd_ref, gpost_ref, out_ref, *, ff_chunk):
    x = x_ref[...]
    h = _rms(x, gpre_ref[...]).astype(BF16)
    acc = jnp.zeros(x.shape, F32)
    for f in range(D_FF // ff_chunk):
        sl = slice(f * ff_chunk, (f + 1) * ff_chunk)
        a = jnp.maximum(jnp.dot(h, wu_ref[:, sl], preferred_element_type=F32), 0.0)
        acc = acc + jnp.dot((a * a).astype(BF16), wd_ref[sl, :], preferred_element_type=F32)
    out_ref[...] = x + _rms(acc, gpost_ref[...])


def _mlp(x2, gpre, wu, wd, gpost, tile, ff_chunk):
    n = x2.shape[0]
    tok = lambda i: (i, 0)
    const = lambda i: (0, 0)
    return pl.pallas_call(
        functools.partial(_mlp_kernel, ff_chunk=ff_chunk),
        grid=(n // tile,),
        in_specs=[
            pl.BlockSpec((tile, D_MODEL), tok),
            pl.BlockSpec((1, D_MODEL), const),
            pl.BlockSpec((D_MODEL, D_FF), const, pipeline_mode=pl.Buffered(1)),
            pl.BlockSpec((D_FF, D_MODEL), const, pipeline_mode=pl.Buffered(1)),
            pl.BlockSpec((1, D_MODEL), const),
        ],
        out_specs=pl.BlockSpec((tile, D_MODEL), tok),
        out_shape=jax.ShapeDtypeStruct((n, D_MODEL), F32),
        compiler_params=_cparams(("arbitrary",)),
        name="mlp",
    )(x2, gpre, wu, wd, gpost)


def _rotary_tables(seq):
    half = HEAD_DIM // 2
    inv_freq = 1.0 / (ROPE_THETA ** (jnp.arange(half, dtype=F32) / half))
    ang = jnp.arange(seq, dtype=F32)[:, None] * inv_freq[None, :]
    cos, sin = jnp.cos(ang), jnp.sin(ang)
    return jnp.tile(cos, (1, 4)), jnp.tile(jnp.concatenate([-sin, sin], axis=-1), (1, 2))


def _prepare(seq, norm_mix_pre, w_in, a_re, a_im, log_step, b_re, b_im, c_re, c_im, d_skip, w_glu, b_glu,
             w_out, norm_mix_post, norm_mlp_pre, w_up, w_down, norm_mlp_post):
    row = lambda t: t.reshape(1, -1).astype(F32)
    return dict(
        tables=_rotary_tables(seq),
        s5=_s5_operators(a_re, a_im, log_step, b_re, b_im, c_re, c_im, seq // (CHUNK * N_SEG)),
        g_mix_pre=row(norm_mix_pre), w_in=w_in.astype(BF16), d_skip=row(d_skip), w_glu=w_glu.astype(BF16),
        b_glu=row(b_glu), w_out=w_out.astype(BF16), g_mix_post=row(norm_mix_post),
        g_mlp_pre=row(norm_mlp_pre), w_up=w_up.astype(BF16), w_down=w_down.astype(BF16),
        g_mlp_post=row(norm_mlp_post))


def _layer(x, p, tok_tile=512, attn_tile=2048):
    bsz, seq, _ = x.shape
    n = bsz * seq
    x2 = x.reshape(n, D_MODEL)
    cos_t, sin_t = p['tables']
    m, bs, cs, pw, dec = p['s5']

    q1, q4, q16, k1, k4, k16, v1, v4, v16, u, ug = _inproj(
        x2, p['g_mix_pre'], p['w_in'], cos_t, sin_t, seq, tok_tile)
    (o1, l1), (o4, l4), (o16, l16) = (
        _banded_attention(q1[None], k1[None], v1[None], 1, seq, attn_tile),
        _banded_attention(q4, k4, v4, 4, seq, attn_tile),
        _banded_attention(q16, k16, v16, 16, seq, attn_tile))
    yg = _s5_scan(ug, m, bs, cs, pw, dec, bsz)
    x1 = _mix(o1, o4, o16, l1, l4, l16, yg, u, x2, p['d_skip'], p['w_glu'], p['b_glu'], p['w_out'],
              p['g_mix_post'], tok_tile)
    y = _mlp(x1, p['g_mlp_pre'], p['w_up'], p['w_down'], p['g_mlp_post'], 2 * tok_tile, 1024)
    return y.reshape(bsz, seq, D_MODEL)


def kernel(x_prompt, x_sample, norm_mix_pre, w_in, ssm_a_re, ssm_a_im, ssm_log_step, ssm_b_re, ssm_b_im,
           ssm_c_re, ssm_c_im, ssm_d, w_glu, b_glu, w_out, norm_mix_post, norm_mlp_pre, w_up, w_down,
           norm_mlp_post):
    weights = (norm_mix_pre, w_in, ssm_a_re, ssm_a_im, ssm_log_step, ssm_b_re, ssm_b_im, ssm_c_re, ssm_c_im,
               ssm_d, w_glu, b_glu, w_out, norm_mix_post, norm_mlp_pre, w_up, w_down, norm_mlp_post)
    depth = norm_mix_pre.shape[0]
    prepared = {}

    def run(x):
        seq = x.shape[1]
        if seq not in prepared:
            prepared[seq] = [_prepare(seq, *(w[l] for w in weights)) for l in range(depth)]
        for l in range(depth):
            x = _layer(x, prepared[seq][l])
        return x

    return run(x_prompt), run(x_sample)
```

```python
import functools
import math

import numpy as np
import jax
import jax.numpy as jnp
from jax import lax
from jax.experimental import pallas as pl
from jax.experimental.pallas import tpu as pltpu

F32 = jnp.float32
BF16 = jnp.bfloat16

D_MODEL = 1024
ATT_WIDTH = 512
SSM_WIDTH = 512
HEAD_DIM = 64
N_HEADS = 8
DILATIONS = ((128, 1), (512, 4), (2048, 16))
RADIUS = 64
SSM_GROUP = 16
N_GROUPS = 32
SSM_STATE = 64
D_FF = 4096
IN_WIDTH = 2048
ROPE_THETA = 10000.0
NORM_EPS = 1e-6
MASK_VALUE = -1e30

CHUNK = 16
N_SEG = 8
LANES = 128
SLOTS = LANES // SSM_GROUP
CW = CHUNK * SSM_GROUP
VMEM_LIMIT = 56 * 1024 * 1024
HI = lax.Precision.HIGHEST
LOG2E = 1.4426950408889634
LN2 = 0.6931471805599453


def _cparams(sem):
    return pltpu.CompilerParams(dimension_semantics=sem, vmem_limit_bytes=VMEM_LIMIT)


def _slot_masks():
    lane = lax.broadcasted_iota(jnp.int32, (1, LANES), 1)
    return [(lane // SSM_GROUP) == s for s in range(SLOTS)]


def _step_of_slot():
    gam = np.arange(SLOTS)[:, None, None]
    hh = np.arange(CHUNK // SLOTS)[None, :, None]
    l = np.arange(SLOTS)[None, None, :]
    return (SLOTS * hh + (l - gam) % SLOTS).reshape(SLOTS, CHUNK)


def _inproj_kernel(x_ref, g_ref, w_ref, cos_ref, sin_ref,
                   q1_ref, q4_ref, q16_ref, k1_ref, k4_ref, k16_ref, v1_ref, v4_ref, v16_ref,
                   u_ref, ug_ref, rs_ref, st_ref):
    x = x_ref[...]
    tile = x.shape[0]
    ms = jnp.mean(x * x, axis=-1, keepdims=True)
    h = (x * lax.rsqrt(ms + NORM_EPS)) * g_ref[...]
    proj = jnp.dot(h.astype(BF16), w_ref[...], preferred_element_type=F32)
    cos = cos_ref[...]
    sin = sin_ref[...]
    lane = lax.broadcasted_iota(jnp.int32, cos.shape, 1)
    first_half = (lane & (HEAD_DIM // 2)) == 0
    n_lt = ATT_WIDTH // LANES

    def rot(t):
        partner = jnp.where(first_half, pltpu.roll(t, LANES - HEAD_DIM // 2, 1),
                            pltpu.roll(t, HEAD_DIM // 2, 1))
        return t * cos + partner * sin

    def spread(lane_tile, nat_ref, d4_ref, d16_ref):
        for c in range(n_lt):
            rs_ref[c] = lane_tile(c)
            nat_ref[:, c * LANES:(c + 1) * LANES] = rs_ref[c].astype(BF16)
        for c in range(n_lt):
            for r4 in range(4):
                blk = rs_ref[c, pl.ds(r4, tile // 4, stride=4), :]
                d4_ref[r4, :, c * LANES:(c + 1) * LANES] = blk.astype(BF16)
                st_ref[c * 4 + r4] = blk
        for c in range(n_lt):
            for r4 in range(4):
                for r2 in range(4):
                    blk = st_ref[c * 4 + r4, pl.ds(r2, tile // 16, stride=4), :]
                    d16_ref[r4 + 4 * r2, :, c * LANES:(c + 1) * LANES] = blk.astype(BF16)

    q_scale = (HEAD_DIM ** -0.5) * LOG2E
    spread(lambda c: rot(proj[:, c * LANES:(c + 1) * LANES]) * q_scale, q1_ref, q4_ref, q16_ref)
    spread(lambda c: rot(proj[:, ATT_WIDTH + c * LANES:ATT_WIDTH + (c + 1) * LANES]), k1_ref, k4_ref, k16_ref)
    spread(lambda c: proj[:, 2 * ATT_WIDTH + c * LANES:2 * ATT_WIDTH + (c + 1) * LANES], v1_ref, v4_ref, v16_ref)
    u = proj[:, 3 * ATT_WIDTH:]
    u_ref[...] = u

    n_chunk = tile // CHUNK
    masks = _slot_masks()
    for qt in range(SSM_WIDTH // LANES):
        rs_ref[qt] = u[:, qt * LANES:(qt + 1) * LANES]
    for qt in range(SSM_WIDTH // LANES):
        for hh in range(CHUNK // SLOTS):
            rolled = []
            for t8 in range(SLOTS):
                step_rows = rs_ref[qt, pl.ds(SLOTS * hh + t8, n_chunk, stride=CHUNK), :]
                rolled.append(step_rows if t8 == 0 else pltpu.roll(step_rows, SSM_GROUP * t8, 1))
            for gam in range(SLOTS):
                res = rolled[SLOTS - 1]
                for t8 in range(SLOTS - 2, -1, -1):
                    res = jnp.where(masks[(gam + t8) % SLOTS], rolled[t8], res)
                ug_ref[SLOTS * qt + gam, :, hh * LANES:(hh + 1) * LANES] = res.astype(BF16)


def _inproj(x2, g, w_bf, cos_t, sin_t, seq, tile):
    n = x2.shape[0]
    n_pos = seq // tile
    tok = lambda i: (i, 0)
    pos = lambda i: (i % n_pos, 0)
    const = lambda i: (0, 0)
    split = lambda i: (0, i, 0)
    qkv_specs = [pl.BlockSpec((tile, ATT_WIDTH), tok), pl.BlockSpec((4, tile // 4, ATT_WIDTH), split),
                 pl.BlockSpec((16, tile // 16, ATT_WIDTH), split)]
    qkv_shapes = [jax.ShapeDtypeStruct((n, ATT_WIDTH), BF16), jax.ShapeDtypeStruct((4, n // 4, ATT_WIDTH), BF16),
                  jax.ShapeDtypeStruct((16, n // 16, ATT_WIDTH), BF16)]
    return pl.pallas_call(
        _inproj_kernel,
        grid=(n // tile,),
        in_specs=[
            pl.BlockSpec((tile, D_MODEL), tok),
            pl.BlockSpec((1, D_MODEL), const),
            pl.BlockSpec((D_MODEL, IN_WIDTH), const, pipeline_mode=pl.Buffered(1)),
            pl.BlockSpec((tile, LANES), pos),
            pl.BlockSpec((tile, LANES), pos),
        ],
        out_specs=qkv_specs * 3 + [pl.BlockSpec((tile, SSM_WIDTH), tok),
                                   pl.BlockSpec((N_GROUPS, tile // CHUNK, CW), split)],
        out_shape=qkv_shapes * 3 + [jax.ShapeDtypeStruct((n, SSM_WIDTH), F32),
                                    jax.ShapeDtypeStruct((N_GROUPS, n // CHUNK, CW), BF16)],
        scratch_shapes=[pltpu.VMEM((ATT_WIDTH // LANES, tile, LANES), F32),
                        pltpu.VMEM((4 * ATT_WIDTH // LANES, tile // 4, LANES), F32)],
        compiler_params=_cparams(("arbitrary",)),
        name="inproj",
    )(x2, g, w_bf, cos_t, sin_t)


def _attn_kernel(q_ref, kp_ref, kc_ref, kn_ref, vp_ref, vc_ref, vn_ref, o_ref, lse_ref,
                 kx_ref, vx_ref, bias_ref, s_ref, p_ref, m_ref, ost_ref, lst_ref, nat_ref, mid_ref,
                 *, dil, tm, qb, tiles_per_seq):
    tile_in_seq = pl.program_id(0) % tiles_per_seq
    n_sb = tm // qb
    if n_sb > 1:
        kx_ref[:, 0:RADIUS] = kp_ref[...]
        kx_ref[:, RADIUS:RADIUS + tm] = kc_ref[...]
        kx_ref[:, RADIUS + tm:] = kn_ref[...]
        vx_ref[:, 0:RADIUS] = vp_ref[...]
        vx_ref[:, RADIUS:RADIUS + tm] = vc_ref[...]
        vx_ref[:, RADIUS + tm:] = vn_ref[...]

    def window(prev_ref, cur_ref, next_ref, ext_ref, r, row0, sl):
        if n_sb > 1:
            return ext_ref[r, row0:row0 + win, sl]
        return jnp.concatenate([prev_ref[r, :, sl], cur_ref[r, :, sl], next_ref[r, :, sl]], axis=0)

    win = qb + 2 * RADIUS
    rows = 2 * qb

    @pl.when(pl.program_id(0) == 0)
    def _():
        qi = lax.broadcasted_iota(jnp.int32, (rows, win), 0) & (qb - 1)
        kcol = lax.broadcasted_iota(jnp.int32, (rows, win), 1)
        band = (kcol >= qi) & (kcol <= qi + 2 * RADIUS)
        after_start = kcol >= RADIUS
        before_end = kcol < qb + RADIUS
        neg = jnp.full((rows, win), MASK_VALUE, F32)
        zero = jnp.zeros((rows, win), F32)
        bias_ref[0] = jnp.where(band, zero, neg)
        bias_ref[1] = jnp.where(band & after_start, zero, neg)
        bias_ref[2] = jnp.where(band & before_end, zero, neg)
        bias_ref[3] = jnp.where(band & after_start & before_end, zero, neg)

    first_tile = (tile_in_seq == 0).astype(jnp.int32)
    last_tile = (tile_in_seq == tiles_per_seq - 1).astype(jnp.int32)

    lane = lax.broadcasted_iota(jnp.int32, (qb, LANES), 1)
    head_a = lane < HEAD_DIM
    ones = jnp.ones((win, LANES), BF16)
    n_pair = N_HEADS // 2

    def item(r, sb, buf):
        row0 = sb * qb
        edge = (first_tile if sb == 0 else 0) + (2 * last_tile if sb == n_sb - 1 else 0)
        bias = bias_ref[edge]
        for p in range(n_pair):
            sl = slice(p * LANES, (p + 1) * LANES)
            q2 = q_ref[r, row0:row0 + qb, sl]
            zq = jnp.zeros_like(q2)
            qs = jnp.concatenate([jnp.where(head_a, q2, zq), jnp.where(head_a, zq, q2)], axis=0)
            kw = window(kp_ref, kc_ref, kn_ref, kx_ref, r, row0, sl)
            s_ref[buf + p] = lax.dot_general(qs, kw, (((1,), (1,)), ((), ())), preferred_element_type=F32)
        for p in range(n_pair):
            s = s_ref[buf + p] + bias
            m = jnp.max(s, axis=-1, keepdims=True)
            p_ref[buf + p] = jnp.exp2(s - m).astype(BF16)
            m_ref[buf + p] = jnp.broadcast_to(m, (rows, LANES))
        m8 = jnp.zeros((qb, LANES), F32)
        l8 = jnp.ones((qb, LANES), F32)
        for p in range(n_pair):
            sl = slice(p * LANES, (p + 1) * LANES)
            vaug = jnp.concatenate([window(vp_ref, vc_ref, vn_ref, vx_ref, r, row0, sl), ones], axis=1)
            res = jnp.dot(p_ref[buf + p], vaug, preferred_element_type=F32)
            den = res[:, LANES:]
            mrow = m_ref[buf + p]
            acc = jnp.where(head_a, res[:qb, :LANES], res[qb:, :LANES])
            out = acc / jnp.where(head_a, den[:qb], den[qb:])
            if dil == 1:
                o_ref[row0:row0 + qb, sl] = out.astype(o_ref.dtype)
            else:
                ost_ref[r, row0:row0 + qb, sl] = out
            m8 = jnp.where(lane == 2 * p, mrow[:qb], jnp.where(lane == 2 * p + 1, mrow[qb:], m8))
            l8 = jnp.where(lane == 2 * p, den[:qb], jnp.where(lane == 2 * p + 1, den[qb:], l8))
        lse = m8 * LN2 + jnp.log(l8)
        if dil == 1:
            lse_ref[row0:row0 + qb, :] = lse
        else:
            lst_ref[r, row0:row0 + qb, :] = lse

    for w in range(dil * n_sb):
        item(w // n_sb, w % n_sb, (w % 2) * n_pair)

    n_lt = ATT_WIDTH // LANES
    if dil > 1:
        def gather(stage_ref, lt):
            sl = slice(lt * LANES, (lt + 1) * LANES)
            if dil == 4:
                for r in range(4):
                    nat_ref[pl.ds(r, tm, stride=4), :] = stage_ref[r, :, sl]
            else:
                for r4 in range(4):
                    for r2 in range(4):
                        mid_ref[r4, pl.ds(r2, tm, stride=4), :] = stage_ref[r4 + 4 * r2, :, sl]
                for r4 in range(4):
                    nat_ref[pl.ds(r4, 4 * tm, stride=4), :] = mid_ref[r4]
            return nat_ref[...]

        for lt in range(n_lt):
            o_ref[:, lt * LANES:(lt + 1) * LANES] = gather(ost_ref, lt).astype(o_ref.dtype)
        lse_ref[...] = gather(lst_ref, 0)


def _banded_attention(q, k, v, dil, seq, nat_tile):
    _, n_rows, _ = q.shape
    n = n_rows * dil
    nat_tile = min(nat_tile, seq)
    tm = nat_tile // dil
    qb = min(2 * RADIUS, tm)
    assert dil in (1, 4, 16) and seq % nat_tile == 0 and tm % qb == 0 and qb % RADIUS == 0
    hb = tm // RADIUS
    n_halo = n_rows // RADIUS
    win = qb + 2 * RADIUS
    ext_shape = (dil, tm + 2 * RADIUS, ATT_WIDTH) if tm > qb else (1, 16, LANES)
    stage_rows = tm if dil > 1 else 8
    cur = pl.BlockSpec((dil, tm, ATT_WIDTH), lambda i: (0, i, 0))
    prev = pl.BlockSpec((dil, RADIUS, ATT_WIDTH), lambda i: (0, jnp.maximum(i * hb - 1, 0), 0))
    nxt = pl.BlockSpec((dil, RADIUS, ATT_WIDTH), lambda i: (0, jnp.minimum((i + 1) * hb, n_halo - 1), 0))
    return pl.pallas_call(
        functools.partial(_attn_kernel, dil=dil, tm=tm, qb=qb, tiles_per_seq=seq // nat_tile),
        grid=(n // nat_tile,),
        in_specs=[cur, prev, cur, nxt, prev, cur, nxt],
        out_specs=[pl.BlockSpec((nat_tile, ATT_WIDTH), lambda i: (i, 0)),
                   pl.BlockSpec((nat_tile, LANES), lambda i: (i, 0))],
        out_shape=[jax.ShapeDtypeStruct((n, ATT_WIDTH), BF16), jax.ShapeDtypeStruct((n, LANES), F32)],
        scratch_shapes=[
            pltpu.VMEM(ext_shape, BF16),
            pltpu.VMEM(ext_shape, BF16),
            pltpu.VMEM((4, 2 * qb, win), F32),
            pltpu.VMEM((N_HEADS, 2 * qb, win), F32),
            pltpu.VMEM((N_HEADS, 2 * qb, win), BF16),
            pltpu.VMEM((N_HEADS, 2 * qb, LANES), F32),
            pltpu.VMEM((dil, stage_rows, ATT_WIDTH), F32),
            pltpu.VMEM((dil, stage_rows, LANES), F32),
            pltpu.VMEM((dil * stage_rows, LANES), F32),
            pltpu.VMEM((4, dil * stage_rows // 4, LANES), F32),
        ],
        compiler_params=_cparams(("arbitrary",)),
        name=f"attn_d{dil}",
    )(q, k, k, k, v, v, v)


def _s5_kernel(ug_ref, m_ref, bs_ref, cs_ref, pw_ref, dec_ref, y_ref, x_ref, h_ref, *, n_i, n_grp):
    half = LANES // 2
    for g in range(n_grp):
        y_ref[g] = jnp.dot(ug_ref[g], m_ref[g], preferred_element_type=F32)
        z = jnp.dot(ug_ref[g], bs_ref[g], preferred_element_type=F32)
        for seg in range(N_SEG):
            rows = slice(seg * n_i, (seg + 1) * n_i)
            x_ref[2 * g, pl.ds(seg, n_i, stride=N_SEG), :] = z[rows, :LANES]
            x_ref[2 * g + 1, pl.ds(seg, n_i, stride=N_SEG), :] = z[rows, LANES:]
    lane = lax.broadcasted_iota(jnp.int32, (N_SEG, LANES), 1)
    fwd = lane < half
    a_re = [jnp.broadcast_to(dec_ref[g, 0:1, :LANES], (N_SEG, LANES)) for g in range(n_grp)]
    a_im = [jnp.broadcast_to(dec_ref[g, 0:1, LANES:], (N_SEG, LANES)) for g in range(n_grp)]

    def scan_body(s, carry):
        rf = pl.ds(pl.multiple_of(s * N_SEG, N_SEG), N_SEG)
        rb = pl.ds(pl.multiple_of((n_i - 1 - s) * N_SEG, N_SEG), N_SEG)
        out = []
        for g in range(n_grp):
            hre, him = carry[2 * g], carry[2 * g + 1]
            h_ref[2 * g, rf, 0:half] = hre[:, :half]
            h_ref[2 * g, rb, half:] = hre[:, half:]
            h_ref[2 * g + 1, rf, 0:half] = him[:, :half]
            h_ref[2 * g + 1, rb, half:] = him[:, half:]
            xre = jnp.where(fwd, x_ref[2 * g, rf, :], x_ref[2 * g, rb, :])
            xim = jnp.where(fwd, x_ref[2 * g + 1, rf, :], x_ref[2 * g + 1, rb, :])
            out += [a_re[g] * hre - a_im[g] * him + xre, a_re[g] * him + a_im[g] * hre + xim]
        return tuple(out)

    zero = jnp.zeros((N_SEG, LANES), F32)
    ends = lax.fori_loop(0, n_i, scan_body, (zero,) * (2 * n_grp))

    lane1 = lax.broadcasted_iota(jnp.int32, (1, LANES), 1)
    fwd1 = lane1 < half
    entry = []
    for g in range(n_grp):
        l_re, l_im = ends[2 * g], ends[2 * g + 1]
        s_re = dec_ref[g, 1:2, :LANES]
        s_im = dec_ref[g, 1:2, LANES:]
        e_re = jnp.zeros((1, LANES), F32)
        e_im = jnp.zeros((1, LANES), F32)
        ins = []
        for s in range(N_SEG):
            ins.append((e_re, e_im))
            lre = jnp.where(fwd1, l_re[s:s + 1], l_re[N_SEG - 1 - s:N_SEG - s])
            lim = jnp.where(fwd1, l_im[s:s + 1], l_im[N_SEG - 1 - s:N_SEG - s])
            e_re, e_im = s_re * e_re - s_im * e_im + lre, s_re * e_im + s_im * e_re + lim
        entry.append((
            jnp.concatenate([jnp.where(fwd1, ins[s][0], ins[N_SEG - 1 - s][0]) for s in range(N_SEG)], axis=0),
            jnp.concatenate([jnp.where(fwd1, ins[s][1], ins[N_SEG - 1 - s][1]) for s in range(N_SEG)], axis=0)))

    def fix_body(ib, c):
        for g in range(n_grp):
            ein_re, ein_im = entry[g]
            pw = pw_ref[g, pl.ds(pl.multiple_of(ib * 8, 8), 8), :]
            for r in range(8):
                rows = pl.ds(pl.multiple_of((ib * 8 + r) * N_SEG, N_SEG), N_SEG)
                p_re = pw[r:r + 1, :LANES]
                p_im = pw[r:r + 1, LANES:]
                h_ref[2 * g, rows, :] = h_ref[2 * g, rows, :] + (p_re * ein_re - p_im * ein_im)
                h_ref[2 * g + 1, rows, :] = h_ref[2 * g + 1, rows, :] + (p_re * ein_im + p_im * ein_re)
        return c

    lax.fori_loop(0, n_i // 8, fix_body, 0)
    for g in range(n_grp):
        ungroup = lambda c: jnp.concatenate(
            [h_ref[c, pl.ds(seg, n_i, stride=N_SEG), :] for seg in range(N_SEG)], axis=0)
        hcat = jnp.concatenate([ungroup(2 * g), ungroup(2 * g + 1)], axis=1).astype(BF16)
        y_ref[g] += jnp.dot(hcat, cs_ref[g], preferred_element_type=F32)


def _s5_scan(ug, m, bs, cs, pw, dec, bsz, n_grp=8):
    n_g, total_rows, width = ug.shape
    rows = total_rows // bsz
    n_i = rows // N_SEG
    grp = lambda b, g: (g, 0, 0)
    seq_rows = pl.BlockSpec((n_grp, rows, width), lambda b, g: (g, b, 0))
    return pl.pallas_call(
        functools.partial(_s5_kernel, n_i=n_i, n_grp=n_grp),
        grid=(bsz, n_g // n_grp),
        in_specs=[
            seq_rows,
            pl.BlockSpec((n_grp, width, width), grp),
            pl.BlockSpec((n_grp, width, width), grp),
            pl.BlockSpec((n_grp, width, width), grp),
            pl.BlockSpec((n_grp, n_i, width), grp),
            pl.BlockSpec((n_grp, 2, width), grp),
        ],
        out_specs=seq_rows,
        out_shape=jax.ShapeDtypeStruct((n_g, total_rows, width), F32),
        scratch_shapes=[pltpu.VMEM((2 * n_grp, rows, LANES), F32)] * 2,
        compiler_params=_cparams(("arbitrary",) * 2),
        name="s5_scan",
    )(ug, m, bs, cs, pw, dec)


def _within_chunk_kernel(kin_ref, m_ref):
    step_of = _step_of_slot()
    masks = _slot_masks()
    for gam in range(SLOTS):
        rolled = {}

        def piece(x, slot):
            key = (x // SLOTS, (slot - x) % SLOTS)
            if key not in rolled:
                src = kin_ref[gam, :, key[0] * LANES:(key[0] + 1) * LANES]
                rolled[key] = src if key[1] == 0 else pltpu.roll(src, SSM_GROUP * key[1], 1)
            return rolled[key]

        for s_pos in range(CHUNK):
            for hh in range(CHUNK // SLOTS):
                lag = lambda l: int(step_of[gam, SLOTS * hh + l] - step_of[gam, s_pos]) + CHUNK - 1
                acc = piece(lag(SLOTS - 1), SLOTS - 1)
                for l in range(SLOTS - 2, -1, -1):
                    acc = jnp.where(masks[l], piece(lag(l), l), acc)
                m_ref[gam, s_pos * SSM_GROUP:(s_pos + 1) * SSM_GROUP, hh * LANES:(hh + 1) * LANES] = acc.astype(BF16)


def _within_chunk(kin):
    n_lag_lanes = kin.shape[-1]
    return pl.pallas_call(
        _within_chunk_kernel,
        grid=(N_GROUPS // SLOTS,),
        in_specs=[pl.BlockSpec((SLOTS, SSM_GROUP, n_lag_lanes), lambda i: (i, 0, 0))],
        out_specs=pl.BlockSpec((SLOTS, CW, CW), lambda i: (i, 0, 0)),
        out_shape=jax.ShapeDtypeStruct((N_GROUPS, CW, CW), BF16),
        compiler_params=_cparams(("arbitrary",)),
        name="s5_within_chunk",
    )(kin)


def _s5_operators(a_re, a_im, log_step, b_re, b_im, c_re, c_im, n_i):
    a_re, a_im, b_re, b_im, c_re, c_im = (t.astype(F32) for t in (a_re, a_im, b_re, b_im, c_re, c_im))
    step = jnp.exp(log_step.astype(F32))[..., None]
    zr, zi = a_re * step, a_im * step

    def power(n):
        n = jnp.asarray(n, F32).reshape(n.shape + (1, 1, 1))
        mag = jnp.exp(n * zr)
        return mag * jnp.cos(n * zi), mag * jnp.sin(n * zi)

    abr, abi = power(np.array(1))
    den = a_re * a_re + a_im * a_im
    fr = ((abr - 1.0) * a_re + abi * a_im) / den
    fi = (abi * a_re - (abr - 1.0) * a_im) / den
    bbr = fr[..., None] * b_re - fi[..., None] * b_im
    bbi = fr[..., None] * b_im + fi[..., None] * b_re

    pr, pi = power(np.arange(CHUNK + 1))
    wr = pr[:CHUNK, ..., None] * bbr - pi[:CHUNK, ..., None] * bbi
    wi = pr[:CHUNK, ..., None] * bbi + pi[:CHUNK, ..., None] * bbr
    kern = (jnp.einsum('dgcp,tdgpe->dgtce', c_re, wr, precision=HI)
            - jnp.einsum('dgcp,tdgpe->dgtce', c_im, wi, precision=HI))

    step_of = _step_of_slot()
    n_q = N_GROUPS // SLOTS
    lag = np.arange(2 * CHUNK - 1)[:, None] - (CHUNK - 1)
    tau = np.arange(CHUNK)[None, :]
    pick = jnp.asarray(np.stack([lag == tau, -lag == tau], axis=1), F32)
    kin = jnp.einsum('xdt,dgtce->gexc', pick, kern, precision=HI)
    kin = jnp.pad(kin.reshape(N_GROUPS, SSM_GROUP, (2 * CHUNK - 1) * SSM_GROUP),
                  ((0, 0), (0, 0), (0, SSM_GROUP)))
    m = _within_chunk(kin)

    onehot = lambda idx: jnp.asarray(idx[..., None] == np.arange(CHUNK + 1), F32)
    grouped = lambda t: t.reshape((t.shape[0], n_q, SLOTS) + t.shape[2:])
    take = lambda oh, t: jnp.einsum('ypn,nqyk->qypk', oh, grouped(t), precision=HI)

    def state_in(d, idx):
        ar, ai = take(onehot(idx), pr[:, d]), take(onehot(idx), pi[:, d])
        br = jnp.swapaxes(grouped(bbr[d][None])[0], -1, -2)
        bi = jnp.swapaxes(grouped(bbi[d][None])[0], -1, -2)
        re = ar[:, :, :, None, :] * br[:, :, None] - ai[:, :, :, None, :] * bi[:, :, None]
        im = ar[:, :, :, None, :] * bi[:, :, None] + ai[:, :, :, None, :] * br[:, :, None]
        return re.reshape(N_GROUPS, CW, SSM_STATE), im.reshape(N_GROUPS, CW, SSM_STATE)

    bf_re, bf_im = state_in(0, CHUNK - 1 - step_of)
    bb_re, bb_im = state_in(1, step_of)
    bs = jnp.concatenate([bf_re, bb_re, bf_im, bb_im], axis=-1).astype(BF16)

    def state_out(d, idx):
        ar, ai = take(onehot(idx), pr[:, d]), take(onehot(idx), pi[:, d])
        cr = jnp.swapaxes(grouped(c_re[d][None])[0], -1, -2)
        ci = jnp.swapaxes(grouped(c_im[d][None])[0], -1, -2)
        ar, ai = jnp.swapaxes(ar, -1, -2)[..., None], jnp.swapaxes(ai, -1, -2)[..., None]
        re = cr[:, :, :, None, :] * ar - ci[:, :, :, None, :] * ai
        im = cr[:, :, :, None, :] * ai + ci[:, :, :, None, :] * ar
        return re.reshape(N_GROUPS, SSM_STATE, CW), im.reshape(N_GROUPS, SSM_STATE, CW)

    cf_re, cf_im = state_out(0, step_of + 1)
    cb_re, cb_im = state_out(1, CHUNK - step_of)
    cs = jnp.concatenate([cf_re, cb_re, -cf_im, -cb_im], axis=1).astype(BF16)

    i_idx = np.arange(n_i)
    pf_re, pf_im = power(CHUNK * i_idx)
    pb_re, pb_im = power(CHUNK * (n_i - 1 - i_idx))
    pw = jnp.concatenate([pf_re[:, 0], pb_re[:, 1], pf_im[:, 0], pb_im[:, 1]], axis=-1)
    pw = jnp.transpose(pw, (1, 0, 2))
    dr, di = power(np.array([CHUNK, CHUNK * n_i]))
    dec = jnp.concatenate([dr[:, 0], dr[:, 1], di[:, 0], di[:, 1]], axis=-1)
    dec = jnp.transpose(dec, (1, 0, 2))
    return m, bs, cs, pw, dec


def _rms(x, g):
    return (x * lax.rsqrt(jnp.mean(x * x, axis=-1, keepdims=True) + NORM_EPS)) * g


def _mix_kernel(o1_ref, o4_ref, o16_ref, l1_ref, l4_ref, l16_ref, yg_ref, u_ref, x_ref,
                d_ref, wg_ref, bg_ref, wo_ref, g_ref, out_ref, ys_ref):
    n_chunk = x_ref.shape[0] // CHUNK
    masks = _slot_masks()
    for qt in range(SSM_WIDTH // LANES):
        for hh in range(CHUNK // SLOTS):
            src = [yg_ref[SLOTS * qt + gam, :, hh * LANES:(hh + 1) * LANES] for gam in range(SLOTS)]
            for t8 in range(SLOTS):
                pre = src[SLOTS - 1]
                for gam in range(SLOTS - 2, -1, -1):
                    pre = jnp.where(masks[(gam + t8) % SLOTS], src[gam], pre)
                nat = pre if t8 == 0 else pltpu.roll(pre, LANES - SSM_GROUP * t8, 1)
                ys_ref[qt, pl.ds(SLOTS * hh + t8, n_chunk, stride=CHUNK), :] = nat
    ys = jnp.concatenate([ys_ref[qt] for qt in range(SSM_WIDTH // LANES)], axis=1)

    l1, l4, l16 = l1_ref[...], l4_ref[...], l16_ref[...]
    mx = jnp.maximum(jnp.maximum(l1, l4), l16)
    e1, e4, e16 = jnp.exp(l1 - mx), jnp.exp(l4 - mx), jnp.exp(l16 - mx)
    inv = 1.0 / (e1 + e4 + e16)
    head_row = lax.broadcasted_iota(jnp.int32, (LANES, ATT_WIDTH), 0)
    head_col = lax.broadcasted_iota(jnp.int32, (LANES, ATT_WIDTH), 1) // HEAD_DIM
    spread = jnp.where(head_row == head_col, 1.0, 0.0).astype(BF16)

    def per_column(w):
        hi = w.astype(BF16)
        lo = (w - hi.astype(F32)).astype(BF16)
        return (jnp.dot(hi, spread, preferred_element_type=F32)
                + jnp.dot(lo, spread, preferred_element_type=F32))

    att = (per_column(e1 * inv) * o1_ref[...].astype(F32) + per_column(e4 * inv) * o4_ref[...].astype(F32)
           + per_column(e16 * inv) * o16_ref[...].astype(F32))
    y = ys + d_ref[...] * u_ref[...]
    y = 0.5 * y * (1.0 + jnp.tanh(math.sqrt(2.0 / math.pi) * (y + 0.044715 * (y * y * y))))
    gate = jnp.dot(y.astype(BF16), wg_ref[...], preferred_element_type=F32) + bg_ref[...]
    ssm = y * (1.0 / (1.0 + jnp.exp(-gate)))
    mixed = (jnp.dot(att.astype(BF16), wo_ref[:ATT_WIDTH], preferred_element_type=F32)
             + jnp.dot(ssm.astype(BF16), wo_ref[ATT_WIDTH:], preferred_element_type=F32))
    out_ref[...] = x_ref[...] + _rms(mixed, g_ref[...])


def _mix(o1, o4, o16, l1, l4, l16, yg, u, x2, d, wg, bg, wo, g, tile):
    n = x2.shape[0]
    tok = lambda i: (i, 0)
    const = lambda i: (0, 0)
    half = pl.BlockSpec((tile, ATT_WIDTH), tok)
    stat = pl.BlockSpec((tile, LANES), tok)
    return pl.pallas_call(
        _mix_kernel,
        grid=(n // tile,),
        in_specs=[half] * 3 + [stat] * 3 + [
            pl.BlockSpec((N_GROUPS, tile // CHUNK, CW), lambda i: (0, i, 0)),
            half,
            pl.BlockSpec((tile, D_MODEL), tok),
            pl.BlockSpec((1, SSM_WIDTH), const),
            pl.BlockSpec((SSM_WIDTH, SSM_WIDTH), const, pipeline_mode=pl.Buffered(1)),
            pl.BlockSpec((1, SSM_WIDTH), const),
            pl.BlockSpec((D_MODEL, D_MODEL), const, pipeline_mode=pl.Buffered(1)),
            pl.BlockSpec((1, D_MODEL), const),
        ],
        out_specs=pl.BlockSpec((tile, D_MODEL), tok),
        out_shape=jax.ShapeDtypeStruct((n, D_MODEL), F32),
        scratch_shapes=[pltpu.VMEM((SSM_WIDTH // LANES, tile, LANES), F32)],
        compiler_params=_cparams(("arbitrary",)),
        name="mix",
    )(o1, o4, o16, l1, l4, l16, yg, u, x2, d, wg, bg, wo, g)


def _mlp_kernel(x_ref, gpre_ref, wu_ref, wd_ref, gpost_ref, out_ref, *, ff_chunk):
    x = x_ref[...]
    h = _rms(x, gpre_ref[...]).astype(BF16)
    acc = jnp.zeros(x.shape, F32)
    for f in range(D_FF // ff_chunk):
        sl = slice(f * ff_chunk, (f + 1) * ff_chunk)
        a = jnp.maximum(jnp.dot(h, wu_ref[:, sl], preferred_element_type=F32), 0.0)
        acc = acc + jnp.dot((a * a).astype(BF16), wd_ref[sl, :], preferred_element_type=F32)
    out_ref[...] = x + _rms(acc, gpost_ref[...])


def _mlp(x2, gpre, wu, wd, gpost, tile, ff_chunk):
    n = x2.shape[0]
    tok = lambda i: (i, 0)
    const = lambda i: (0, 0)
    return pl.pallas_call(
        functools.partial(_mlp_kernel, ff_chunk=ff_chunk),
        grid=(n // tile,),
        in_specs=[
            pl.BlockSpec((tile, D_MODEL), tok),
            pl.BlockSpec((1, D_MODEL), const),
            pl.BlockSpec((D_MODEL, D_FF), const, pipeline_mode=pl.Buffered(1)),
            pl.BlockSpec((D_FF, D_MODEL), const, pipeline_mode=pl.Buffered(1)),
            pl.BlockSpec((1, D_MODEL), const),
        ],
        out_specs=pl.BlockSpec((tile, D_MODEL), tok),
        out_shape=jax.ShapeDtypeStruct((n, D_MODEL), F32),
        compiler_params=_cparams(("arbitrary",)),
        name="mlp",
    )(x2, gpre, wu, wd, gpost)


def _rotary_tables(seq):
    half = HEAD_DIM // 2
    inv_freq = 1.0 / (ROPE_THETA ** (jnp.arange(half, dtype=F32) / half))
    ang = jnp.arange(seq, dtype=F32)[:, None] * inv_freq[None, :]
    cos, sin = jnp.cos(ang), jnp.sin(ang)
    return jnp.tile(cos, (1, 4)), jnp.tile(jnp.concatenate([-sin, sin], axis=-1), (1, 2))


def _prepare(seq, norm_mix_pre, w_in, a_re, a_im, log_step, b_re, b_im, c_re, c_im, d_skip, w_glu, b_glu,
             w_out, norm_mix_post, norm_mlp_pre, w_up, w_down, norm_mlp_post):
    row = lambda t: t.reshape(1, -1).astype(F32)
    return dict(
        tables=_rotary_tables(seq),
        s5=_s5_operators(a_re, a_im, log_step, b_re, b_im, c_re, c_im, seq // (CHUNK * N_SEG)),
        g_mix_pre=row(norm_mix_pre), w_in=w_in.astype(BF16), d_skip=row(d_skip), w_glu=w_glu.astype(BF16),
        b_glu=row(b_glu), w_out=w_out.astype(BF16), g_mix_post=row(norm_mix_post),
        g_mlp_pre=row(norm_mlp_pre), w_up=w_up.astype(BF16), w_down=w_down.astype(BF16),
        g_mlp_post=row(norm_mlp_post))


def _layer(x, p, tok_tile=1024, attn_tile=2048):
    bsz, seq, _ = x.shape
    n = bsz * seq
    x2 = x.reshape(n, D_MODEL)
    cos_t, sin_t = p['tables']
    m, bs, cs, pw, dec = p['s5']

    q1, q4, q16, k1, k4, k16, v1, v4, v16, u, ug = _inproj(
        x2, p['g_mix_pre'], p['w_in'], cos_t, sin_t, seq, tok_tile)
    (o1, l1), (o4, l4), (o16, l16) = (
        _banded_attention(q1[None], k1[None], v1[None], 1, seq, attn_tile),
        _banded_attention(q4, k4, v4, 4, seq, attn_tile),
        _banded_attention(q16, k16, v16, 16, seq, attn_tile))
    yg = _s5_scan(ug, m, bs, cs, pw, dec, bsz)
    x1 = _mix(o1, o4, o16, l1, l4, l16, yg, u, x2, p['d_skip'], p['w_glu'], p['b_glu'], p['w_out'],
              p['g_mix_post'], tok_tile)
    y = _mlp(x1, p['g_mlp_pre'], p['w_up'], p['w_down'], p['g_mlp_post'], tok_tile, 1024)
    return y.reshape(bsz, seq, D_MODEL)


def kernel(x_prompt, x_sample, norm_mix_pre, w_in, ssm_a_re, ssm_a_im, ssm_log_step, ssm_b_re, ssm_b_im,
           ssm_c_re, ssm_c_im, ssm_d, w_glu, b_glu, w_out, norm_mix_post, norm_mlp_pre, w_up, w_down,
           norm_mlp_post):
    weights = (norm_mix_pre, w_in, ssm_a_re, ssm_a_im, ssm_log_step, ssm_b_re, ssm_b_im, ssm_c_re, ssm_c_im,
               ssm_d, w_glu, b_glu, w_out, norm_mix_post, norm_mlp_pre, w_up, w_down, norm_mlp_post)
    depth = norm_mix_pre.shape[0]
    prepared = {}

    def run(x):
        seq = x.shape[1]
        if seq not in prepared:
            prepared[seq] = [_prepare(seq, *(w[l] for w in weights)) for l in range(depth)]
        for l in range(depth):
            x = _layer(x, prepared[seq][l])
        return x

    return run(x_prompt), run(x_sample)
```

```python
import functools
import math

import numpy as np
import jax
import jax.numpy as jnp
from jax import lax
from jax.experimental import pallas as pl
from jax.experimental.pallas import tpu as pltpu

F32 = jnp.float32
BF16 = jnp.bfloat16

D_MODEL = 1024
ATT_WIDTH = 512
SSM_WIDTH = 512
HEAD_DIM = 64
N_HEADS = 8
DILATIONS = ((128, 1), (512, 4), (2048, 16))
RADIUS = 64
SSM_GROUP = 16
N_GROUPS = 32
SSM_STATE = 64
D_FF = 4096
IN_WIDTH = 2048
ROPE_THETA = 10000.0
NORM_EPS = 1e-6
MASK_VALUE = -1e30

CHUNK = 16
LANES = 128
SUBLANES = 8
BF16_ROWS = 16
N_SEG = SUBLANES
SLOTS = LANES // SSM_GROUP
CW = CHUNK * SSM_GROUP
VMEM_LIMIT = 56 * 1024 * 1024
HI = lax.Precision.HIGHEST
LOG2E = 1.4426950408889634
LN2 = 0.6931471805599453


def _cparams(sem):
    return pltpu.CompilerParams(dimension_semantics=sem, vmem_limit_bytes=VMEM_LIMIT)


def _slot_masks():
    lane = lax.broadcasted_iota(jnp.int32, (1, LANES), 1)
    return [(lane // SSM_GROUP) == s for s in range(SLOTS)]


def _step_of_slot():
    gam = np.arange(SLOTS)[:, None, None]
    hh = np.arange(CHUNK // SLOTS)[None, :, None]
    l = np.arange(SLOTS)[None, None, :]
    return (SLOTS * hh + (l - gam) % SLOTS).reshape(SLOTS, CHUNK)


def _inproj_kernel(x_ref, g_ref, w_ref, cos_ref, sin_ref,
                   q1_ref, q4_ref, q16_ref, k1_ref, k4_ref, k16_ref, v1_ref, v4_ref, v16_ref,
                   u_ref, ug_ref, rs_ref, st_ref):
    x = x_ref[...]
    tile = x.shape[0]
    ms = jnp.mean(x * x, axis=-1, keepdims=True)
    h = (x * lax.rsqrt(ms + NORM_EPS)) * g_ref[...]
    proj = jnp.dot(h.astype(BF16), w_ref[...], preferred_element_type=F32)
    cos = cos_ref[...]
    sin = sin_ref[...]
    lane = lax.broadcasted_iota(jnp.int32, cos.shape, 1)
    first_half = (lane & (HEAD_DIM // 2)) == 0
    n_lt = ATT_WIDTH // LANES

    def rot(t):
        partner = jnp.where(first_half, pltpu.roll(t, LANES - HEAD_DIM // 2, 1),
                            pltpu.roll(t, HEAD_DIM // 2, 1))
        return t * cos + partner * sin

    def spread(lane_tile, nat_ref, d4_ref, d16_ref):
        for c in range(n_lt):
            rs_ref[c] = lane_tile(c)
            nat_ref[:, c * LANES:(c + 1) * LANES] = rs_ref[c].astype(BF16)
        for c in range(n_lt):
            for r4 in range(4):
                blk = rs_ref[c, pl.ds(r4, tile // 4, stride=4), :]
                d4_ref[r4, :, c * LANES:(c + 1) * LANES] = blk.astype(BF16)
                st_ref[c * 4 + r4] = blk
        for c in range(n_lt):
            for r4 in range(4):
                for r2 in range(4):
                    blk = st_ref[c * 4 + r4, pl.ds(r2, tile // 16, stride=4), :]
                    d16_ref[r4 + 4 * r2, :, c * LANES:(c + 1) * LANES] = blk.astype(BF16)

    q_scale = (HEAD_DIM ** -0.5) * LOG2E
    spread(lambda c: rot(proj[:, c * LANES:(c + 1) * LANES]) * q_scale, q1_ref, q4_ref, q16_ref)
    spread(lambda c: rot(proj[:, ATT_WIDTH + c * LANES:ATT_WIDTH + (c + 1) * LANES]), k1_ref, k4_ref, k16_ref)
    spread(lambda c: proj[:, 2 * ATT_WIDTH + c * LANES:2 * ATT_WIDTH + (c + 1) * LANES], v1_ref, v4_ref, v16_ref)
    u = proj[:, 3 * ATT_WIDTH:]
    u_ref[...] = u

    n_chunk = tile // CHUNK
    masks = _slot_masks()
    for qt in range(SSM_WIDTH // LANES):
        rs_ref[qt] = u[:, qt * LANES:(qt + 1) * LANES]
    for qt in range(SSM_WIDTH // LANES):
        for hh in range(CHUNK // SLOTS):
            rolled = []
            for t8 in range(SLOTS):
                step_rows = rs_ref[qt, pl.ds(SLOTS * hh + t8, n_chunk, stride=CHUNK), :]
                rolled.append(step_rows if t8 == 0 else pltpu.roll(step_rows, SSM_GROUP * t8, 1))
            for gam in range(SLOTS):
                res = rolled[SLOTS - 1]
                for t8 in range(SLOTS - 2, -1, -1):
                    res = jnp.where(masks[(gam + t8) % SLOTS], rolled[t8], res)
                ug_ref[SLOTS * qt + gam, :, hh * LANES:(hh + 1) * LANES] = res.astype(BF16)


def _inproj(x2, g, w_bf, cos_t, sin_t, seq, tile):
    n = x2.shape[0]
    n_pos = seq // tile
    tok = lambda i: (i, 0)
    pos = lambda i: (i % n_pos, 0)
    const = lambda i: (0, 0)
    split = lambda i: (0, i, 0)
    qkv_specs = [pl.BlockSpec((tile, ATT_WIDTH), tok), pl.BlockSpec((4, tile // 4, ATT_WIDTH), split),
                 pl.BlockSpec((16, tile // 16, ATT_WIDTH), split)]
    qkv_shapes = [jax.ShapeDtypeStruct((n, ATT_WIDTH), BF16), jax.ShapeDtypeStruct((4, n // 4, ATT_WIDTH), BF16),
                  jax.ShapeDtypeStruct((16, n // 16, ATT_WIDTH), BF16)]
    return pl.pallas_call(
        _inproj_kernel,
        grid=(n // tile,),
        in_specs=[
            pl.BlockSpec((tile, D_MODEL), tok),
            pl.BlockSpec((1, D_MODEL), const),
            pl.BlockSpec((D_MODEL, IN_WIDTH), const, pipeline_mode=pl.Buffered(1)),
            pl.BlockSpec((tile, LANES), pos),
            pl.BlockSpec((tile, LANES), pos),
        ],
        out_specs=qkv_specs * 3 + [pl.BlockSpec((tile, SSM_WIDTH), tok),
                                   pl.BlockSpec((N_GROUPS, tile // CHUNK, CW), split)],
        out_shape=qkv_shapes * 3 + [jax.ShapeDtypeStruct((n, SSM_WIDTH), F32),
                                    jax.ShapeDtypeStruct((N_GROUPS, n // CHUNK, CW), BF16)],
        scratch_shapes=[pltpu.VMEM((ATT_WIDTH // LANES, tile, LANES), F32),
                        pltpu.VMEM((4 * ATT_WIDTH // LANES, tile // 4, LANES), F32)],
        compiler_params=_cparams(("arbitrary",)),
        name="inproj",
    )(x2, g, w_bf, cos_t, sin_t)


def _attn_kernel(q_ref, kp_ref, kc_ref, kn_ref, vp_ref, vc_ref, vn_ref, o_ref, lse_ref,
                 kx_ref, vx_ref, bias_ref, s_ref, p_ref, m_ref, ost_ref, lst_ref, nat_ref, mid_ref,
                 *, dil, tm, qb, tiles_per_seq):
    tile_in_seq = pl.program_id(0) % tiles_per_seq
    n_sb = tm // qb
    if n_sb > 1:
        kx_ref[:, 0:RADIUS] = kp_ref[...]
        kx_ref[:, RADIUS:RADIUS + tm] = kc_ref[...]
        kx_ref[:, RADIUS + tm:] = kn_ref[...]
        vx_ref[:, 0:RADIUS] = vp_ref[...]
        vx_ref[:, RADIUS:RADIUS + tm] = vc_ref[...]
        vx_ref[:, RADIUS + tm:] = vn_ref[...]

    def window(prev_ref, cur_ref, next_ref, ext_ref, r, row0, sl):
        if n_sb > 1:
            return ext_ref[r, row0:row0 + win, sl]
        return jnp.concatenate([prev_ref[r, :, sl], cur_ref[r, :, sl], next_ref[r, :, sl]], axis=0)

    win = qb + 2 * RADIUS
    rows = 2 * qb
    qi = lax.broadcasted_iota(jnp.int32, (rows, win), 0) & (qb - 1)
    kcol = lax.broadcasted_iota(jnp.int32, (rows, win), 1)
    band = (kcol >= qi) & (kcol <= qi + 2 * RADIUS)
    after_start = kcol >= RADIUS
    before_end = kcol < qb + RADIUS
    neg = jnp.full((rows, win), MASK_VALUE, F32)
    zero = jnp.zeros((rows, win), F32)
    bias_ref[0] = jnp.where(band, zero, neg)
    bias_ref[1] = jnp.where(band & after_start, zero, neg)
    bias_ref[2] = jnp.where(band & before_end, zero, neg)
    bias_ref[3] = jnp.where(band & after_start & before_end, zero, neg)
    first_tile = (tile_in_seq == 0).astype(jnp.int32)
    last_tile = (tile_in_seq == tiles_per_seq - 1).astype(jnp.int32)

    lane = lax.broadcasted_iota(jnp.int32, (qb, LANES), 1)
    head_a = lane < HEAD_DIM
    ones = jnp.ones((win, LANES), BF16)
    n_pair = N_HEADS // 2

    def item(r, sb):
        row0 = sb * qb
        edge = (first_tile if sb == 0 else 0) + (2 * last_tile if sb == n_sb - 1 else 0)
        bias = bias_ref[edge]
        for p in range(n_pair):
            sl = slice(p * LANES, (p + 1) * LANES)
            q2 = q_ref[r, row0:row0 + qb, sl]
            zq = jnp.zeros_like(q2)
            qs = jnp.concatenate([jnp.where(head_a, q2, zq), jnp.where(head_a, zq, q2)], axis=0)
            kw = window(kp_ref, kc_ref, kn_ref, kx_ref, r, row0, sl)
            s_ref[p] = lax.dot_general(qs, kw, (((1,), (1,)), ((), ())), preferred_element_type=F32)
        for p in range(n_pair):
            s = s_ref[p] + bias
            m = jnp.max(s, axis=-1, keepdims=True)
            p_ref[p] = jnp.exp2(s - m).astype(BF16)
            m_ref[p] = jnp.broadcast_to(m, (rows, LANES))
        m8 = jnp.zeros((qb, LANES), F32)
        l8 = jnp.ones((qb, LANES), F32)
        for p in range(n_pair):
            sl = slice(p * LANES, (p + 1) * LANES)
            vaug = jnp.concatenate([window(vp_ref, vc_ref, vn_ref, vx_ref, r, row0, sl), ones], axis=1)
            res = jnp.dot(p_ref[p], vaug, preferred_element_type=F32)
            den = res[:, LANES:]
            mrow = m_ref[p]
            acc = jnp.where(head_a, res[:qb, :LANES], res[qb:, :LANES])
            out = acc / jnp.where(head_a, den[:qb], den[qb:])
            if dil == 1:
                o_ref[row0:row0 + qb, sl] = out.astype(o_ref.dtype)
            else:
                ost_ref[r, row0:row0 + qb, sl] = out
            m8 = jnp.where(lane == 2 * p, mrow[:qb], jnp.where(lane == 2 * p + 1, mrow[qb:], m8))
            l8 = jnp.where(lane == 2 * p, den[:qb], jnp.where(lane == 2 * p + 1, den[qb:], l8))
        lse = m8 * LN2 + jnp.log(l8)
        if dil == 1:
            lse_ref[row0:row0 + qb, :] = lse
        else:
            lst_ref[r, row0:row0 + qb, :] = lse

    for w in range(dil * n_sb):
        item(w // n_sb, w % n_sb)

    n_lt = ATT_WIDTH // LANES
    if dil > 1:
        def gather(stage_ref, lt):
            sl = slice(lt * LANES, (lt + 1) * LANES)
            if dil == 4:
                for r in range(4):
                    nat_ref[pl.ds(r, tm, stride=4), :] = stage_ref[r, :, sl]
            else:
                for r4 in range(4):
                    for r2 in range(4):
                        mid_ref[r4, pl.ds(r2, tm, stride=4), :] = stage_ref[r4 + 4 * r2, :, sl]
                for r4 in range(4):
                    nat_ref[pl.ds(r4, 4 * tm, stride=4), :] = mid_ref[r4]
            return nat_ref[...]

        for lt in range(n_lt):
            o_ref[:, lt * LANES:(lt + 1) * LANES] = gather(ost_ref, lt).astype(o_ref.dtype)
        lse_ref[...] = gather(lst_ref, 0)


def _banded_attention(q, k, v, dil, seq, nat_tile):
    _, n_rows, _ = q.shape
    n = n_rows * dil
    nat_tile = min(nat_tile, seq)
    tm = nat_tile // dil
    qb = min(2 * RADIUS, tm)
    assert dil in (1, 4, 16) and seq % nat_tile == 0 and tm % qb == 0 and qb % RADIUS == 0
    hb = tm // RADIUS
    n_halo = n_rows // RADIUS
    win = qb + 2 * RADIUS
    ext_shape = (dil, tm + 2 * RADIUS, ATT_WIDTH) if tm > qb else (1, BF16_ROWS, LANES)
    stage_rows = tm if dil > 1 else SUBLANES
    cur = pl.BlockSpec((dil, tm, ATT_WIDTH), lambda i: (0, i, 0))
    prev = pl.BlockSpec((dil, RADIUS, ATT_WIDTH), lambda i: (0, jnp.maximum(i * hb - 1, 0), 0))
    nxt = pl.BlockSpec((dil, RADIUS, ATT_WIDTH), lambda i: (0, jnp.minimum((i + 1) * hb, n_halo - 1), 0))
    return pl.pallas_call(
        functools.partial(_attn_kernel, dil=dil, tm=tm, qb=qb, tiles_per_seq=seq // nat_tile),
        grid=(n // nat_tile,),
        in_specs=[cur, prev, cur, nxt, prev, cur, nxt],
        out_specs=[pl.BlockSpec((nat_tile, ATT_WIDTH), lambda i: (i, 0)),
                   pl.BlockSpec((nat_tile, LANES), lambda i: (i, 0))],
        out_shape=[jax.ShapeDtypeStruct((n, ATT_WIDTH), BF16), jax.ShapeDtypeStruct((n, LANES), F32)],
        scratch_shapes=[
            pltpu.VMEM(ext_shape, BF16),
            pltpu.VMEM(ext_shape, BF16),
            pltpu.VMEM((4, 2 * qb, win), F32),
            pltpu.VMEM((N_HEADS // 2, 2 * qb, win), F32),
            pltpu.VMEM((N_HEADS // 2, 2 * qb, win), BF16),
            pltpu.VMEM((N_HEADS // 2, 2 * qb, LANES), F32),
            pltpu.VMEM((dil, stage_rows, ATT_WIDTH), F32),
            pltpu.VMEM((dil, stage_rows, LANES), F32),
            pltpu.VMEM((dil * stage_rows, LANES), F32),
            pltpu.VMEM((4, dil * stage_rows // 4, LANES), F32),
        ],
        compiler_params=_cparams(("arbitrary",)),
        name=f"attn_d{dil}",
    )(q, k, k, k, v, v, v)


def _s5_kernel(ug_ref, m_ref, bs_ref, cs_ref, pw_ref, dec_ref, y_ref, x_ref, h_ref, *, n_i, n_grp):
    half = LANES // 2
    for g in range(n_grp):
        y_ref[g] = jnp.dot(ug_ref[g], m_ref[g], preferred_element_type=F32)
        z = jnp.dot(ug_ref[g], bs_ref[g], preferred_element_type=F32)
        for seg in range(N_SEG):
            rows = slice(seg * n_i, (seg + 1) * n_i)
            x_ref[2 * g, pl.ds(seg, n_i, stride=N_SEG), :] = z[rows, :LANES]
            x_ref[2 * g + 1, pl.ds(seg, n_i, stride=N_SEG), :] = z[rows, LANES:]
    lane = lax.broadcasted_iota(jnp.int32, (N_SEG, LANES), 1)
    fwd = lane < half
    a_re = [jnp.broadcast_to(dec_ref[g, 0:1, :LANES], (N_SEG, LANES)) for g in range(n_grp)]
    a_im = [jnp.broadcast_to(dec_ref[g, 0:1, LANES:], (N_SEG, LANES)) for g in range(n_grp)]

    def scan_body(s, carry):
        rf = pl.ds(pl.multiple_of(s * N_SEG, N_SEG), N_SEG)
        rb = pl.ds(pl.multiple_of((n_i - 1 - s) * N_SEG, N_SEG), N_SEG)
        out = []
        for g in range(n_grp):
            hre, him = carry[2 * g], carry[2 * g + 1]
            h_ref[2 * g, rf, 0:half] = hre[:, :half]
            h_ref[2 * g, rb, half:] = hre[:, half:]
            h_ref[2 * g + 1, rf, 0:half] = him[:, :half]
            h_ref[2 * g + 1, rb, half:] = him[:, half:]
            xre = jnp.where(fwd, x_ref[2 * g, rf, :], x_ref[2 * g, rb, :])
            xim = jnp.where(fwd, x_ref[2 * g + 1, rf, :], x_ref[2 * g + 1, rb, :])
            out += [a_re[g] * hre - a_im[g] * him + xre, a_re[g] * him + a_im[g] * hre + xim]
        return tuple(out)

    zero = jnp.zeros((N_SEG, LANES), F32)
    ends = lax.fori_loop(0, n_i, scan_body, (zero,) * (2 * n_grp))

    lane1 = lax.broadcasted_iota(jnp.int32, (1, LANES), 1)
    fwd1 = lane1 < half
    entry = []
    for g in range(n_grp):
        l_re, l_im = ends[2 * g], ends[2 * g + 1]
        s_re = dec_ref[g, 1:2, :LANES]
        s_im = dec_ref[g, 1:2, LANES:]
        e_re = jnp.zeros((1, LANES), F32)
        e_im = jnp.zeros((1, LANES), F32)
        ins = []
        for s in range(N_SEG):
            ins.append((e_re, e_im))
            lre = jnp.where(fwd1, l_re[s:s + 1], l_re[N_SEG - 1 - s:N_SEG - s])
            lim = jnp.where(fwd1, l_im[s:s + 1], l_im[N_SEG - 1 - s:N_SEG - s])
            e_re, e_im = s_re * e_re - s_im * e_im + lre, s_re * e_im + s_im * e_re + lim
        entry.append((
            jnp.concatenate([jnp.where(fwd1, ins[s][0], ins[N_SEG - 1 - s][0]) for s in range(N_SEG)], axis=0),
            jnp.concatenate([jnp.where(fwd1, ins[s][1], ins[N_SEG - 1 - s][1]) for s in range(N_SEG)], axis=0)))

    def fix_body(ib, c):
        for g in range(n_grp):
            ein_re, ein_im = entry[g]
            pw = pw_ref[g, pl.ds(pl.multiple_of(ib * SUBLANES, SUBLANES), SUBLANES), :]
            for r in range(SUBLANES):
                rows = pl.ds(pl.multiple_of((ib * SUBLANES + r) * N_SEG, N_SEG), N_SEG)
                p_re = pw[r:r + 1, :LANES]
                p_im = pw[r:r + 1, LANES:]
                h_ref[2 * g, rows, :] = h_ref[2 * g, rows, :] + (p_re * ein_re - p_im * ein_im)
                h_ref[2 * g + 1, rows, :] = h_ref[2 * g + 1, rows, :] + (p_re * ein_im + p_im * ein_re)
        return c

    lax.fori_loop(0, n_i // SUBLANES, fix_body, 0)
    for g in range(n_grp):
        ungroup = lambda c: jnp.concatenate(
            [h_ref[c, pl.ds(seg, n_i, stride=N_SEG), :] for seg in range(N_SEG)], axis=0)
        hcat = jnp.concatenate([ungroup(2 * g), ungroup(2 * g + 1)], axis=1).astype(BF16)
        y_ref[g] += jnp.dot(hcat, cs_ref[g], preferred_element_type=F32)


def _s5_scan(ug, m, bs, cs, pw, dec, bsz, n_grp=8):
    n_g, total_rows, width = ug.shape
    rows = total_rows // bsz
    n_i = rows // N_SEG
    grp = lambda b, g: (g, 0, 0)
    seq_rows = pl.BlockSpec((n_grp, rows, width), lambda b, g: (g, b, 0))
    return pl.pallas_call(
        functools.partial(_s5_kernel, n_i=n_i, n_grp=n_grp),
        grid=(bsz, n_g // n_grp),
        in_specs=[
            seq_rows,
            pl.BlockSpec((n_grp, width, width), grp),
            pl.BlockSpec((n_grp, width, width), grp),
            pl.BlockSpec((n_grp, width, width), grp),
            pl.BlockSpec((n_grp, n_i, width), grp),
            pl.BlockSpec((n_grp, 2, width), grp),
        ],
        out_specs=seq_rows,
        out_shape=jax.ShapeDtypeStruct((n_g, total_rows, width), F32),
        scratch_shapes=[pltpu.VMEM((2 * n_grp, rows, LANES), F32)] * 2,
        compiler_params=_cparams(("arbitrary",) * 2),
        name="s5_scan",
    )(ug, m, bs, cs, pw, dec)


def _within_chunk_kernel(kin_ref, m_ref):
    step_of = _step_of_slot()
    masks = _slot_masks()
    for gam in range(SLOTS):
        rolled = {}

        def piece(x, slot):
            key = (x // SLOTS, (slot - x) % SLOTS)
            if key not in rolled:
                src = kin_ref[gam, :, key[0] * LANES:(key[0] + 1) * LANES]
                rolled[key] = src if key[1] == 0 else pltpu.roll(src, SSM_GROUP * key[1], 1)
            return rolled[key]

        for s_pos in range(CHUNK):
            for hh in range(CHUNK // SLOTS):
                lag = lambda l: int(step_of[gam, SLOTS * hh + l] - step_of[gam, s_pos]) + CHUNK - 1
                acc = piece(lag(SLOTS - 1), SLOTS - 1)
                for l in range(SLOTS - 2, -1, -1):
                    acc = jnp.where(masks[l], piece(lag(l), l), acc)
                m_ref[gam, s_pos * SSM_GROUP:(s_pos + 1) * SSM_GROUP, hh * LANES:(hh + 1) * LANES] = acc.astype(BF16)


def _within_chunk(kin):
    n_lag_lanes = kin.shape[-1]
    return pl.pallas_call(
        _within_chunk_kernel,
        grid=(N_GROUPS // SLOTS,),
        in_specs=[pl.BlockSpec((SLOTS, SSM_GROUP, n_lag_lanes), lambda i: (i, 0, 0))],
        out_specs=pl.BlockSpec((SLOTS, CW, CW), lambda i: (i, 0, 0)),
        out_shape=jax.ShapeDtypeStruct((N_GROUPS, CW, CW), BF16),
        compiler_params=_cparams(("arbitrary",)),
        name="s5_within_chunk",
    )(kin)


def _s5_operators(a_re, a_im, log_step, b_re, b_im, c_re, c_im, n_i):
    a_re, a_im, b_re, b_im, c_re, c_im = (t.astype(F32) for t in (a_re, a_im, b_re, b_im, c_re, c_im))
    step = jnp.exp(log_step.astype(F32))[..., None]
    zr, zi = a_re * step, a_im * step

    def power(n):
        n = jnp.asarray(n, F32).reshape(n.shape + (1, 1, 1))
        mag = jnp.exp(n * zr)
        return mag * jnp.cos(n * zi), mag * jnp.sin(n * zi)

    abr, abi = power(np.array(1))
    den = a_re * a_re + a_im * a_im
    fr = ((abr - 1.0) * a_re + abi * a_im) / den
    fi = (abi * a_re - (abr - 1.0) * a_im) / den
    bbr = fr[..., None] * b_re - fi[..., None] * b_im
    bbi = fr[..., None] * b_im + fi[..., None] * b_re

    pr, pi = power(np.arange(CHUNK + 1))
    wr = pr[:CHUNK, ..., None] * bbr - pi[:CHUNK, ..., None] * bbi
    wi = pr[:CHUNK, ..., None] * bbi + pi[:CHUNK, ..., None] * bbr
    kern = (jnp.einsum('dgcp,tdgpe->dgtce', c_re, wr, precision=HI)
            - jnp.einsum('dgcp,tdgpe->dgtce', c_im, wi, precision=HI))

    step_of = _step_of_slot()
    n_q = N_GROUPS // SLOTS
    lag = np.arange(2 * CHUNK - 1)[:, None] - (CHUNK - 1)
    tau = np.arange(CHUNK)[None, :]
    pick = jnp.asarray(np.stack([lag == tau, -lag == tau], axis=1), F32)
    kin = jnp.einsum('xdt,dgtce->gexc', pick, kern, precision=HI)
    kin = jnp.pad(kin.reshape(N_GROUPS, SSM_GROUP, (2 * CHUNK - 1) * SSM_GROUP),
                  ((0, 0), (0, 0), (0, SSM_GROUP)))
    m = _within_chunk(kin)

    onehot = lambda idx: jnp.asarray(idx[..., None] == np.arange(CHUNK + 1), F32)
    grouped = lambda t: t.reshape((t.shape[0], n_q, SLOTS) + t.shape[2:])
    take = lambda oh, t: jnp.einsum('ypn,nqyk->qypk', oh, grouped(t), precision=HI)

    def state_in(d, idx):
        ar, ai = take(onehot(idx), pr[:, d]), take(onehot(idx), pi[:, d])
        br = jnp.swapaxes(grouped(bbr[d][None])[0], -1, -2)
        bi = jnp.swapaxes(grouped(bbi[d][None])[0], -1, -2)
        re = ar[:, :, :, None, :] * br[:, :, None] - ai[:, :, :, None, :] * bi[:, :, None]
        im = ar[:, :, :, None, :] * bi[:, :, None] + ai[:, :, :, None, :] * br[:, :, None]
        return re.reshape(N_GROUPS, CW, SSM_STATE), im.reshape(N_GROUPS, CW, SSM_STATE)

    bf_re, bf_im = state_in(0, CHUNK - 1 - step_of)
    bb_re, bb_im = state_in(1, step_of)
    bs = jnp.concatenate([bf_re, bb_re, bf_im, bb_im], axis=-1).astype(BF16)

    def state_out(d, idx):
        ar, ai = take(onehot(idx), pr[:, d]), take(onehot(idx), pi[:, d])
        cr = jnp.swapaxes(grouped(c_re[d][None])[0], -1, -2)
        ci = jnp.swapaxes(grouped(c_im[d][None])[0], -1, -2)
        ar, ai = jnp.swapaxes(ar, -1, -2)[..., None], jnp.swapaxes(ai, -1, -2)[..., None]
        re = cr[:, :, :, None, :] * ar - ci[:, :, :, None, :] * ai
        im = cr[:, :, :, None, :] * ai + ci[:, :, :, None, :] * ar
        return re.reshape(N_GROUPS, SSM_STATE, CW), im.reshape(N_GROUPS, SSM_STATE, CW)

    cf_re, cf_im = state_out(0, step_of + 1)
    cb_re, cb_im = state_out(1, CHUNK - step_of)
    cs = jnp.concatenate([cf_re, cb_re, -cf_im, -cb_im], axis=1).astype(BF16)

    i_idx = np.arange(n_i)
    pf_re, pf_im = power(CHUNK * i_idx)
    pb_re, pb_im = power(CHUNK * (n_i - 1 - i_idx))
    pw = jnp.concatenate([pf_re[:, 0], pb_re[:, 1], pf_im[:, 0], pb_im[:, 1]], axis=-1)
    pw = jnp.transpose(pw, (1, 0, 2))
    dr, di = power(np.array([CHUNK, CHUNK * n_i]))
    dec = jnp.concatenate([dr[:, 0], dr[:, 1], di[:, 0], di[:, 1]], axis=-1)
    dec = jnp.transpose(dec, (1, 0, 2))
    return m, bs, cs, pw, dec


def _rms(x, g):
    return (x * lax.rsqrt(jnp.mean(x * x, axis=-1, keepdims=True) + NORM_EPS)) * g


def _mix_kernel(o1_ref, o4_ref, o16_ref, l1_ref, l4_ref, l16_ref, yg_ref, u_ref, x_ref,
                d_ref, wg_ref, bg_ref, wo_ref, g_ref, out_ref, ys_ref):
    n_chunk = x_ref.shape[0] // CHUNK
    masks = _slot_masks()
    for qt in range(SSM_WIDTH // LANES):
        for hh in range(CHUNK // SLOTS):
            src = [yg_ref[SLOTS * qt + gam, :, hh * LANES:(hh + 1) * LANES] for gam in range(SLOTS)]
            for t8 in range(SLOTS):
                pre = src[SLOTS - 1]
                for gam in range(SLOTS - 2, -1, -1):
                    pre = jnp.where(masks[(gam + t8) % SLOTS], src[gam], pre)
                nat = pre if t8 == 0 else pltpu.roll(pre, LANES - SSM_GROUP * t8, 1)
                ys_ref[qt, pl.ds(SLOTS * hh + t8, n_chunk, stride=CHUNK), :] = nat
    ys = jnp.concatenate([ys_ref[qt] for qt in range(SSM_WIDTH // LANES)], axis=1)

    l1, l4, l16 = l1_ref[...], l4_ref[...], l16_ref[...]
    mx = jnp.maximum(jnp.maximum(l1, l4), l16)
    e1, e4, e16 = jnp.exp(l1 - mx), jnp.exp(l4 - mx), jnp.exp(l16 - mx)
    inv = 1.0 / (e1 + e4 + e16)
    head_row = lax.broadcasted_iota(jnp.int32, (LANES, ATT_WIDTH), 0)
    head_col = lax.broadcasted_iota(jnp.int32, (LANES, ATT_WIDTH), 1) // HEAD_DIM
    spread = jnp.where(head_row == head_col, 1.0, 0.0).astype(BF16)

    def per_column(w):
        hi = w.astype(BF16)
        lo = (w - hi.astype(F32)).astype(BF16)
        return (jnp.dot(hi, spread, preferred_element_type=F32)
                + jnp.dot(lo, spread, preferred_element_type=F32))

    att = (per_column(e1 * inv) * o1_ref[...].astype(F32) + per_column(e4 * inv) * o4_ref[...].astype(F32)
           + per_column(e16 * inv) * o16_ref[...].astype(F32))
    y = ys + d_ref[...] * u_ref[...]
    y = 0.5 * y * (1.0 + jnp.tanh(math.sqrt(2.0 / math.pi) * (y + 0.044715 * (y * y * y))))
    gate = jnp.dot(y.astype(BF16), wg_ref[...], preferred_element_type=F32) + bg_ref[...]
    ssm = y * (1.0 / (1.0 + jnp.exp(-gate)))
    mixed = (jnp.dot(att.astype(BF16), wo_ref[:ATT_WIDTH], preferred_element_type=F32)
             + jnp.dot(ssm.astype(BF16), wo_ref[ATT_WIDTH:], preferred_element_type=F32))
    out_ref[...] = x_ref[...] + _rms(mixed, g_ref[...])


def _mix(o1, o4, o16, l1, l4, l16, yg, u, x2, d, wg, bg, wo, g, tile):
    n = x2.shape[0]
    tok = lambda i: (i, 0)
    const = lambda i: (0, 0)
    half = pl.BlockSpec((tile, ATT_WIDTH), tok)
    stat = pl.BlockSpec((tile, LANES), tok)
    return pl.pallas_call(
        _mix_kernel,
        grid=(n // tile,),
        in_specs=[half] * 3 + [stat] * 3 + [
            pl.BlockSpec((N_GROUPS, tile // CHUNK, CW), lambda i: (0, i, 0)),
            half,
            pl.BlockSpec((tile, D_MODEL), tok),
            pl.BlockSpec((1, SSM_WIDTH), const),
            pl.BlockSpec((SSM_WIDTH, SSM_WIDTH), const, pipeline_mode=pl.Buffered(1)),
            pl.BlockSpec((1, SSM_WIDTH), const),
            pl.BlockSpec((D_MODEL, D_MODEL), const, pipeline_mode=pl.Buffered(1)),
            pl.BlockSpec((1, D_MODEL), const),
        ],
        out_specs=pl.BlockSpec((tile, D_MODEL), tok),
        out_shape=jax.ShapeDtypeStruct((n, D_MODEL), F32),
        scratch_shapes=[pltpu.VMEM((SSM_WIDTH // LANES, tile, LANES), F32)],
        compiler_params=_cparams(("arbitrary",)),
        name="mix",
    )(o1, o4, o16, l1, l4, l16, yg, u, x2, d, wg, bg, wo, g)


def _mlp_kernel(x_ref, gpre_ref, wu_ref, wd_ref, gpost_ref, out_ref, *, ff_chunk):
    x = x_ref[...]
    h = _rms(x, gpre_ref[...]).astype(BF16)
    acc = jnp.zeros(x.shape, F32)
    for f in range(D_FF // ff_chunk):
        sl = slice(f * ff_chunk, (f + 1) * ff_chunk)
        a = jnp.maximum(jnp.dot(h, wu_ref[:, sl], preferred_element_type=F32), 0.0)
        acc = acc + jnp.dot((a * a).astype(BF16), wd_ref[sl, :], preferred_element_type=F32)
    out_ref[...] = x + _rms(acc, gpost_ref[...])


def _mlp(x2, gpre, wu, wd, gpost, tile, ff_chunk):
    n = x2.shape[0]
    tok = lambda i: (i, 0)
    const = lambda i: (0, 0)
    return pl.pallas_call(
        functools.partial(_mlp_kernel, ff_chunk=ff_chunk),
        grid=(n // tile,),
        in_specs=[
            pl.BlockSpec((tile, D_MODEL), tok),
            pl.BlockSpec((1, D_MODEL), const),
            pl.BlockSpec((D_MODEL, D_FF), const, pipeline_mode=pl.Buffered(1)),
            pl.BlockSpec((D_FF, D_MODEL), const, pipeline_mode=pl.Buffered(1)),
            pl.BlockSpec((1, D_MODEL), const),
        ],
        out_specs=pl.BlockSpec((tile, D_MODEL), tok),
        out_shape=jax.ShapeDtypeStruct((n, D_MODEL), F32),
        compiler_params=_cparams(("arbitrary",)),
        name="mlp",
    )(x2, gpre, wu, wd, gpost)


def _rotary_tables(seq):
    half = HEAD_DIM // 2
    inv_freq = 1.0 / (ROPE_THETA ** (jnp.arange(half, dtype=F32) / half))
    ang = jnp.arange(seq, dtype=F32)[:, None] * inv_freq[None, :]
    cos, sin = jnp.cos(ang), jnp.sin(ang)
    return jnp.tile(cos, (1, 4)), jnp.tile(jnp.concatenate([-sin, sin], axis=-1), (1, 2))


def _prepare(seq, norm_mix_pre, w_in, a_re, a_im, log_step, b_re, b_im, c_re, c_im, d_skip, w_glu, b_glu,
             w_out, norm_mix_post, norm_mlp_pre, w_up, w_down, norm_mlp_post):
    row = lambda t: t.reshape(1, -1).astype(F32)
    return dict(
        tables=_rotary_tables(seq),
        s5=_s5_operators(a_re, a_im, log_step, b_re, b_im, c_re, c_im, seq // (CHUNK * N_SEG)),
        g_mix_pre=row(norm_mix_pre), w_in=w_in.astype(BF16), d_skip=row(d_skip), w_glu=w_glu.astype(BF16),
        b_glu=row(b_glu), w_out=w_out.astype(BF16), g_mix_post=row(norm_mix_post),
        g_mlp_pre=row(norm_mlp_pre), w_up=w_up.astype(BF16), w_down=w_down.astype(BF16),
        g_mlp_post=row(norm_mlp_post))


def _layer(x, p, tok_tile=1024, attn_tile=2048):
    bsz, seq, _ = x.shape
    n = bsz * seq
    x2 = x.reshape(n, D_MODEL)
    cos_t, sin_t = p['tables']
    m, bs, cs, pw, dec = p['s5']

    q1, q4, q16, k1, k4, k16, v1, v4, v16, u, ug = _inproj(
        x2, p['g_mix_pre'], p['w_in'], cos_t, sin_t, seq, tok_tile)
    (o1, l1), (o4, l4), (o16, l16) = (
        _banded_attention(q1[None], k1[None], v1[None], 1, seq, attn_tile),
        _banded_attention(q4, k4, v4, 4, seq, attn_tile),
        _banded_attention(q16, k16, v16, 16, seq, attn_tile))
    yg = _s5_scan(ug, m, bs, cs, pw, dec, bsz)
    x1 = _mix(o1, o4, o16, l1, l4, l16, yg, u, x2, p['d_skip'], p['w_glu'], p['b_glu'], p['w_out'],
              p['g_mix_post'], tok_tile)
    y = _mlp(x1, p['g_mlp_pre'], p['w_up'], p['w_down'], p['g_mlp_post'], tok_tile, 1024)
    return y.reshape(bsz, seq, D_MODEL)


def kernel(x_prompt, x_sample, norm_mix_pre, w_in, ssm_a_re, ssm_a_im, ssm_log_step, ssm_b_re, ssm_b_im,
           ssm_c_re, ssm_c_im, ssm_d, w_glu, b_glu, w_out, norm_mix_post, norm_mlp_pre, w_up, w_down,
           norm_mlp_post):
    weights = (norm_mix_pre, w_in, ssm_a_re, ssm_a_im, ssm_log_step, ssm_b_re, ssm_b_im, ssm_c_re, ssm_c_im,
               ssm_d, w_glu, b_glu, w_out, norm_mix_post, norm_mlp_pre, w_up, w_down, norm_mlp_post)
    depth = norm_mix_pre.shape[0]
    prepared = {}

    def run(x):
        seq = x.shape[1]
        if seq not in prepared:
            prepared[seq] = [_prepare(seq, *(w[l] for w in weights)) for l in range(depth)]
        for l in range(depth):
            x = _layer(x, prepared[seq][l])
        return x

    return run(x_prompt), run(x_sample)
```

```python
import functools
import math

import numpy as np
import jax
import jax.numpy as jnp
from jax import lax
from jax.experimental import pallas as pl
from jax.experimental.pallas import tpu as pltpu

F32 = jnp.float32
BF16 = jnp.bfloat16

D_MODEL = 1024
ATT_WIDTH = 512
SSM_WIDTH = 512
HEAD_DIM = 64
N_HEADS = 8
DILATIONS = ((128, 1), (512, 4), (2048, 16))
RADIUS = 64
SSM_GROUP = 16
N_GROUPS = 32
SSM_STATE = 64
D_FF = 4096
IN_WIDTH = 2048
ROPE_THETA = 10000.0
NORM_EPS = 1e-6
MASK_VALUE = -1e30

CHUNK = 16
LANES = 128
SUBLANES = 8
BF16_ROWS = 16
N_SEG = SUBLANES
SLOTS = LANES // SSM_GROUP
CW = CHUNK * SSM_GROUP
VMEM_LIMIT = 56 * 1024 * 1024
HI = lax.Precision.HIGHEST
LOG2E = 1.4426950408889634
LN2 = 0.6931471805599453


def _cparams(sem):
    return pltpu.CompilerParams(dimension_semantics=sem, vmem_limit_bytes=VMEM_LIMIT)


def _slot_masks():
    lane = lax.broadcasted_iota(jnp.int32, (1, LANES), 1)
    return [(lane // SSM_GROUP) == s for s in range(SLOTS)]


def _step_of_slot():
    gam = np.arange(SLOTS)[:, None, None]
    hh = np.arange(CHUNK // SLOTS)[None, :, None]
    l = np.arange(SLOTS)[None, None, :]
    return (SLOTS * hh + (l - gam) % SLOTS).reshape(SLOTS, CHUNK)


def _inproj_kernel(x_ref, g_ref, w_ref, cos_ref, sin_ref,
                   q1_ref, q4_ref, q16_ref, k1_ref, k4_ref, k16_ref, v1_ref, v4_ref, v16_ref,
                   u_ref, ug_ref, rs_ref, st_ref):
    x = x_ref[...]
    tile = x.shape[0]
    ms = jnp.mean(x * x, axis=-1, keepdims=True)
    h = (x * lax.rsqrt(ms + NORM_EPS)) * g_ref[...]
    proj = jnp.dot(h.astype(BF16), w_ref[...], preferred_element_type=F32)
    cos = cos_ref[...]
    sin = sin_ref[...]
    lane = lax.broadcasted_iota(jnp.int32, cos.shape, 1)
    first_half = (lane & (HEAD_DIM // 2)) == 0
    n_lt = ATT_WIDTH // LANES

    def rot(t):
        partner = jnp.where(first_half, pltpu.roll(t, LANES - HEAD_DIM // 2, 1),
                            pltpu.roll(t, HEAD_DIM // 2, 1))
        return t * cos + partner * sin

    def spread(lane_tile, nat_ref, d4_ref, d16_ref):
        for c in range(n_lt):
            rs_ref[c] = lane_tile(c)
            nat_ref[:, c * LANES:(c + 1) * LANES] = rs_ref[c].astype(BF16)
        for c in range(n_lt):
            for r4 in range(4):
                blk = rs_ref[c, pl.ds(r4, tile // 4, stride=4), :]
                d4_ref[r4, :, c * LANES:(c + 1) * LANES] = blk.astype(BF16)
                st_ref[c * 4 + r4] = blk
        for c in range(n_lt):
            for r4 in range(4):
                for r2 in range(4):
                    blk = st_ref[c * 4 + r4, pl.ds(r2, tile // 16, stride=4), :]
                    d16_ref[r4 + 4 * r2, :, c * LANES:(c + 1) * LANES] = blk.astype(BF16)

    q_scale = (HEAD_DIM ** -0.5) * LOG2E
    spread(lambda c: rot(proj[:, c * LANES:(c + 1) * LANES]) * q_scale, q1_ref, q4_ref, q16_ref)
    spread(lambda c: rot(proj[:, ATT_WIDTH + c * LANES:ATT_WIDTH + (c + 1) * LANES]), k1_ref, k4_ref, k16_ref)
    spread(lambda c: proj[:, 2 * ATT_WIDTH + c * LANES:2 * ATT_WIDTH + (c + 1) * LANES], v1_ref, v4_ref, v16_ref)
    u = proj[:, 3 * ATT_WIDTH:]
    u_ref[...] = u

    n_chunk = tile // CHUNK
    masks = _slot_masks()
    for qt in range(SSM_WIDTH // LANES):
        rs_ref[qt] = u[:, qt * LANES:(qt + 1) * LANES]
    for qt in range(SSM_WIDTH // LANES):
        for hh in range(CHUNK // SLOTS):
            rolled = []
            for t8 in range(SLOTS):
                step_rows = rs_ref[qt, pl.ds(SLOTS * hh + t8, n_chunk, stride=CHUNK), :]
                rolled.append(step_rows if t8 == 0 else pltpu.roll(step_rows, SSM_GROUP * t8, 1))
            for gam in range(SLOTS):
                res = rolled[SLOTS - 1]
                for t8 in range(SLOTS - 2, -1, -1):
                    res = jnp.where(masks[(gam + t8) % SLOTS], rolled[t8], res)
                ug_ref[SLOTS * qt + gam, :, hh * LANES:(hh + 1) * LANES] = res.astype(BF16)


def _inproj(x2, g, w_bf, cos_t, sin_t, seq, tile):
    n = x2.shape[0]
    n_pos = seq // tile
    tok = lambda i: (i, 0)
    pos = lambda i: (i % n_pos, 0)
    const = lambda i: (0, 0)
    split = lambda i: (0, i, 0)
    qkv_specs = [pl.BlockSpec((tile, ATT_WIDTH), tok), pl.BlockSpec((4, tile // 4, ATT_WIDTH), split),
                 pl.BlockSpec((16, tile // 16, ATT_WIDTH), split)]
    qkv_shapes = [jax.ShapeDtypeStruct((n, ATT_WIDTH), BF16), jax.ShapeDtypeStruct((4, n // 4, ATT_WIDTH), BF16),
                  jax.ShapeDtypeStruct((16, n // 16, ATT_WIDTH), BF16)]
    return pl.pallas_call(
        _inproj_kernel,
        grid=(n // tile,),
        in_specs=[
            pl.BlockSpec((tile, D_MODEL), tok),
            pl.BlockSpec((1, D_MODEL), const),
            pl.BlockSpec((D_MODEL, IN_WIDTH), const, pipeline_mode=pl.Buffered(1)),
            pl.BlockSpec((tile, LANES), pos),
            pl.BlockSpec((tile, LANES), pos),
        ],
        out_specs=qkv_specs * 3 + [pl.BlockSpec((tile, SSM_WIDTH), tok),
                                   pl.BlockSpec((N_GROUPS, tile // CHUNK, CW), split)],
        out_shape=qkv_shapes * 3 + [jax.ShapeDtypeStruct((n, SSM_WIDTH), F32),
                                    jax.ShapeDtypeStruct((N_GROUPS, n // CHUNK, CW), BF16)],
        scratch_shapes=[pltpu.VMEM((ATT_WIDTH // LANES, tile, LANES), F32),
                        pltpu.VMEM((4 * ATT_WIDTH // LANES, tile // 4, LANES), F32)],
        compiler_params=_cparams(("arbitrary",)),
        name="inproj",
    )(x2, g, w_bf, cos_t, sin_t)


def _attn_kernel(q_ref, kp_ref, kc_ref, kn_ref, vp_ref, vc_ref, vn_ref, o_ref, lse_ref,
                 kx_ref, vx_ref, bias_ref, s_ref, p_ref, m_ref, ost_ref, lst_ref, nat_ref, mid_ref,
                 *, dil, tm, qb, tiles_per_seq):
    tile_in_seq = pl.program_id(0) % tiles_per_seq
    n_sb = tm // qb
    if n_sb > 1:
        kx_ref[:, 0:RADIUS] = kp_ref[...]
        kx_ref[:, RADIUS:RADIUS + tm] = kc_ref[...]
        kx_ref[:, RADIUS + tm:] = kn_ref[...]
        vx_ref[:, 0:RADIUS] = vp_ref[...]
        vx_ref[:, RADIUS:RADIUS + tm] = vc_ref[...]
        vx_ref[:, RADIUS + tm:] = vn_ref[...]

    def window(prev_ref, cur_ref, next_ref, ext_ref, r, row0, sl):
        if n_sb > 1:
            return ext_ref[r, row0:row0 + win, sl]
        return jnp.concatenate([prev_ref[r, :, sl], cur_ref[r, :, sl], next_ref[r, :, sl]], axis=0)

    win = qb + 2 * RADIUS
    rows = 2 * qb
    qi = lax.broadcasted_iota(jnp.int32, (rows, win), 0) & (qb - 1)
    kcol = lax.broadcasted_iota(jnp.int32, (rows, win), 1)
    band = (kcol >= qi) & (kcol <= qi + 2 * RADIUS)
    after_start = kcol >= RADIUS
    before_end = kcol < qb + RADIUS
    neg = jnp.full((rows, win), MASK_VALUE, F32)
    zero = jnp.zeros((rows, win), F32)
    bias_ref[0] = jnp.where(band, zero, neg)
    bias_ref[1] = jnp.where(band & after_start, zero, neg)
    bias_ref[2] = jnp.where(band & before_end, zero, neg)
    bias_ref[3] = jnp.where(band & after_start & before_end, zero, neg)
    first_tile = (tile_in_seq == 0).astype(jnp.int32)
    last_tile = (tile_in_seq == tiles_per_seq - 1).astype(jnp.int32)

    lane = lax.broadcasted_iota(jnp.int32, (qb, LANES), 1)
    head_a = lane < HEAD_DIM
    ones = jnp.ones((win, LANES), BF16)
    n_pair = N_HEADS // 2

    def item(r, sb):
        row0 = sb * qb
        edge = (first_tile if sb == 0 else 0) + (2 * last_tile if sb == n_sb - 1 else 0)
        bias = bias_ref[edge]
        for p in range(n_pair):
            sl = slice(p * LANES, (p + 1) * LANES)
            q2 = q_ref[r, row0:row0 + qb, sl]
            zq = jnp.zeros_like(q2)
            qs = jnp.concatenate([jnp.where(head_a, q2, zq), jnp.where(head_a, zq, q2)], axis=0)
            kw = window(kp_ref, kc_ref, kn_ref, kx_ref, r, row0, sl)
            s_ref[p] = lax.dot_general(qs, kw, (((1,), (1,)), ((), ())), preferred_element_type=F32)
        for p in range(n_pair):
            s = s_ref[p] + bias
            m = jnp.max(s, axis=-1, keepdims=True)
            p_ref[p] = jnp.exp2(s - m).astype(BF16)
            m_ref[p] = jnp.broadcast_to(m, (rows, LANES))
        m8 = jnp.zeros((qb, LANES), F32)
        l8 = jnp.ones((qb, LANES), F32)
        for p in range(n_pair):
            sl = slice(p * LANES, (p + 1) * LANES)
            vaug = jnp.concatenate([window(vp_ref, vc_ref, vn_ref, vx_ref, r, row0, sl), ones], axis=1)
            res = jnp.dot(p_ref[p], vaug, preferred_element_type=F32)
            den = res[:, LANES:]
            mrow = m_ref[p]
            acc = jnp.where(head_a, res[:qb, :LANES], res[qb:, :LANES])
            out = acc / jnp.where(head_a, den[:qb], den[qb:])
            if dil == 1:
                o_ref[row0:row0 + qb, sl] = out.astype(o_ref.dtype)
            else:
                ost_ref[r, row0:row0 + qb, sl] = out
            m8 = jnp.where(lane == 2 * p, mrow[:qb], jnp.where(lane == 2 * p + 1, mrow[qb:], m8))
            l8 = jnp.where(lane == 2 * p, den[:qb], jnp.where(lane == 2 * p + 1, den[qb:], l8))
        lse = m8 * LN2 + jnp.log(l8)
        if dil == 1:
            lse_ref[row0:row0 + qb, :] = lse
        else:
            lst_ref[r, row0:row0 + qb, :] = lse

    for w in range(dil * n_sb):
        item(w // n_sb, w % n_sb)

    n_lt = ATT_WIDTH // LANES
    if dil > 1:
        def gather(stage_ref, lt):
            sl = slice(lt * LANES, (lt + 1) * LANES)
            if dil == 4:
                for r in range(4):
                    nat_ref[pl.ds(r, tm, stride=4), :] = stage_ref[r, :, sl]
            else:
                for r4 in range(4):
                    for r2 in range(4):
                        mid_ref[r4, pl.ds(r2, tm, stride=4), :] = stage_ref[r4 + 4 * r2, :, sl]
                for r4 in range(4):
                    nat_ref[pl.ds(r4, 4 * tm, stride=4), :] = mid_ref[r4]
            return nat_ref[...]

        for lt in range(n_lt):
            o_ref[:, lt * LANES:(lt + 1) * LANES] = gather(ost_ref, lt).astype(o_ref.dtype)
        lse_ref[...] = gather(lst_ref, 0)


def _banded_attention(q, k, v, dil, seq, nat_tile):
    _, n_rows, _ = q.shape
    n = n_rows * dil
    nat_tile = min(nat_tile, seq)
    tm = nat_tile // dil
    qb = min(2 * RADIUS, tm)
    assert dil in (1, 4, 16) and seq % nat_tile == 0 and tm % qb == 0 and qb % RADIUS == 0
    hb = tm // RADIUS
    n_halo = n_rows // RADIUS
    win = qb + 2 * RADIUS
    ext_shape = (dil, tm + 2 * RADIUS, ATT_WIDTH) if tm > qb else (1, BF16_ROWS, LANES)
    stage_rows = tm if dil > 1 else SUBLANES
    cur = pl.BlockSpec((dil, tm, ATT_WIDTH), lambda i: (0, i, 0))
    prev = pl.BlockSpec((dil, RADIUS, ATT_WIDTH), lambda i: (0, jnp.maximum(i * hb - 1, 0), 0))
    nxt = pl.BlockSpec((dil, RADIUS, ATT_WIDTH), lambda i: (0, jnp.minimum((i + 1) * hb, n_halo - 1), 0))
    return pl.pallas_call(
        functools.partial(_attn_kernel, dil=dil, tm=tm, qb=qb, tiles_per_seq=seq // nat_tile),
        grid=(n // nat_tile,),
        in_specs=[cur, prev, cur, nxt, prev, cur, nxt],
        out_specs=[pl.BlockSpec((nat_tile, ATT_WIDTH), lambda i: (i, 0)),
                   pl.BlockSpec((nat_tile, LANES), lambda i: (i, 0))],
        out_shape=[jax.ShapeDtypeStruct((n, ATT_WIDTH), BF16), jax.ShapeDtypeStruct((n, LANES), F32)],
        scratch_shapes=[
            pltpu.VMEM(ext_shape, BF16),
            pltpu.VMEM(ext_shape, BF16),
            pltpu.VMEM((4, 2 * qb, win), F32),
            pltpu.VMEM((N_HEADS // 2, 2 * qb, win), F32),
            pltpu.VMEM((N_HEADS // 2, 2 * qb, win), BF16),
            pltpu.VMEM((N_HEADS // 2, 2 * qb, LANES), F32),
            pltpu.VMEM((dil, stage_rows, ATT_WIDTH), F32),
            pltpu.VMEM((dil, stage_rows, LANES), F32),
            pltpu.VMEM((dil * stage_rows, LANES), F32),
            pltpu.VMEM((4, dil * stage_rows // 4, LANES), F32),
        ],
        compiler_params=_cparams(("arbitrary",)),
        name=f"attn_d{dil}",
    )(q, k, k, k, v, v, v)


def _s5_kernel(ug_ref, bs_ref, mc_ref, pw_ref, dec_ref, y_ref, x_ref, h_ref, *, n_i, n_grp):
    half = LANES // 2
    for g in range(n_grp):
        z = jnp.dot(ug_ref[g], bs_ref[g], preferred_element_type=F32)
        for seg in range(N_SEG):
            rows = slice(seg * n_i, (seg + 1) * n_i)
            x_ref[2 * g, pl.ds(seg, n_i, stride=N_SEG), :] = z[rows, :LANES]
            x_ref[2 * g + 1, pl.ds(seg, n_i, stride=N_SEG), :] = z[rows, LANES:]
    lane = lax.broadcasted_iota(jnp.int32, (N_SEG, LANES), 1)
    fwd = lane < half
    a_re = [jnp.broadcast_to(dec_ref[g, 0:1, :LANES], (N_SEG, LANES)) for g in range(n_grp)]
    a_im = [jnp.broadcast_to(dec_ref[g, 0:1, LANES:], (N_SEG, LANES)) for g in range(n_grp)]

    def scan_body(s, carry):
        rf = pl.ds(pl.multiple_of(s * N_SEG, N_SEG), N_SEG)
        rb = pl.ds(pl.multiple_of((n_i - 1 - s) * N_SEG, N_SEG), N_SEG)
        out = []
        for g in range(n_grp):
            hre, him = carry[2 * g], carry[2 * g + 1]
            h_ref[2 * g, rf, 0:half] = hre[:, :half]
            h_ref[2 * g, rb, half:] = hre[:, half:]
            h_ref[2 * g + 1, rf, 0:half] = him[:, :half]
            h_ref[2 * g + 1, rb, half:] = him[:, half:]
            xre = jnp.where(fwd, x_ref[2 * g, rf, :], x_ref[2 * g, rb, :])
            xim = jnp.where(fwd, x_ref[2 * g + 1, rf, :], x_ref[2 * g + 1, rb, :])
            out += [a_re[g] * hre - a_im[g] * him + xre, a_re[g] * him + a_im[g] * hre + xim]
        return tuple(out)

    zero = jnp.zeros((N_SEG, LANES), F32)
    ends = lax.fori_loop(0, n_i, scan_body, (zero,) * (2 * n_grp))

    lane1 = lax.broadcasted_iota(jnp.int32, (1, LANES), 1)
    fwd1 = lane1 < half
    entry = []
    for g in range(n_grp):
        l_re, l_im = ends[2 * g], ends[2 * g + 1]
        s_re = dec_ref[g, 1:2, :LANES]
        s_im = dec_ref[g, 1:2, LANES:]
        e_re = jnp.zeros((1, LANES), F32)
        e_im = jnp.zeros((1, LANES), F32)
        ins = []
        for s in range(N_SEG):
            ins.append((e_re, e_im))
            lre = jnp.where(fwd1, l_re[s:s + 1], l_re[N_SEG - 1 - s:N_SEG - s])
            lim = jnp.where(fwd1, l_im[s:s + 1], l_im[N_SEG - 1 - s:N_SEG - s])
            e_re, e_im = s_re * e_re - s_im * e_im + lre, s_re * e_im + s_im * e_re + lim
        entry.append((
            jnp.concatenate([jnp.where(fwd1, ins[s][0], ins[N_SEG - 1 - s][0]) for s in range(N_SEG)], axis=0),
            jnp.concatenate([jnp.where(fwd1, ins[s][1], ins[N_SEG - 1 - s][1]) for s in range(N_SEG)], axis=0)))

    def fix_body(ib, c):
        for g in range(n_grp):
            ein_re, ein_im = entry[g]
            pw = pw_ref[g, pl.ds(pl.multiple_of(ib * SUBLANES, SUBLANES), SUBLANES), :]
            for r in range(SUBLANES):
                rows = pl.ds(pl.multiple_of((ib * SUBLANES + r) * N_SEG, N_SEG), N_SEG)
                p_re = pw[r:r + 1, :LANES]
                p_im = pw[r:r + 1, LANES:]
                h_ref[2 * g, rows, :] = h_ref[2 * g, rows, :] + (p_re * ein_re - p_im * ein_im)
                h_ref[2 * g + 1, rows, :] = h_ref[2 * g + 1, rows, :] + (p_re * ein_im + p_im * ein_re)
        return c

    lax.fori_loop(0, n_i // SUBLANES, fix_body, 0)
    for g in range(n_grp):
        ungroup = lambda c: jnp.concatenate(
            [h_ref[c, pl.ds(seg, n_i, stride=N_SEG), :] for seg in range(N_SEG)], axis=0)
        hcat = jnp.concatenate([ungroup(2 * g), ungroup(2 * g + 1)], axis=1).astype(BF16)
        y_ref[g] = jnp.dot(jnp.concatenate([ug_ref[g], hcat], axis=1), mc_ref[g], preferred_element_type=F32)


def _s5_scan(ug, bs, mc, pw, dec, bsz, n_grp=8):
    n_g, total_rows, width = ug.shape
    rows = total_rows // bsz
    n_i = rows // N_SEG
    grp = lambda b, g: (g, 0, 0)
    seq_rows = pl.BlockSpec((n_grp, rows, width), lambda b, g: (g, b, 0))
    return pl.pallas_call(
        functools.partial(_s5_kernel, n_i=n_i, n_grp=n_grp),
        grid=(bsz, n_g // n_grp),
        in_specs=[
            seq_rows,
            pl.BlockSpec((n_grp, width, width), grp),
            pl.BlockSpec((n_grp, 2 * width, width), grp),
            pl.BlockSpec((n_grp, n_i, width), grp),
            pl.BlockSpec((n_grp, 2, width), grp),
        ],
        out_specs=seq_rows,
        out_shape=jax.ShapeDtypeStruct((n_g, total_rows, width), F32),
        scratch_shapes=[pltpu.VMEM((2 * n_grp, rows, LANES), F32)] * 2,
        compiler_params=_cparams(("arbitrary",) * 2),
        name="s5_scan",
    )(ug, bs, mc, pw, dec)


def _within_chunk_kernel(kin_ref, m_ref):
    step_of = _step_of_slot()
    masks = _slot_masks()
    for gam in range(SLOTS):
        rolled = {}

        def piece(x, slot):
            key = (x // SLOTS, (slot - x) % SLOTS)
            if key not in rolled:
                src = kin_ref[gam, :, key[0] * LANES:(key[0] + 1) * LANES]
                rolled[key] = src if key[1] == 0 else pltpu.roll(src, SSM_GROUP * key[1], 1)
            return rolled[key]

        for s_pos in range(CHUNK):
            for hh in range(CHUNK // SLOTS):
                lag = lambda l: int(step_of[gam, SLOTS * hh + l] - step_of[gam, s_pos]) + CHUNK - 1
                acc = piece(lag(SLOTS - 1), SLOTS - 1)
                for l in range(SLOTS - 2, -1, -1):
                    acc = jnp.where(masks[l], piece(lag(l), l), acc)
                m_ref[gam, s_pos * SSM_GROUP:(s_pos + 1) * SSM_GROUP, hh * LANES:(hh + 1) * LANES] = acc.astype(BF16)


def _within_chunk(kin):
    n_lag_lanes = kin.shape[-1]
    return pl.pallas_call(
        _within_chunk_kernel,
        grid=(N_GROUPS // SLOTS,),
        in_specs=[pl.BlockSpec((SLOTS, SSM_GROUP, n_lag_lanes), lambda i: (i, 0, 0))],
        out_specs=pl.BlockSpec((SLOTS, CW, CW), lambda i: (i, 0, 0)),
        out_shape=jax.ShapeDtypeStruct((N_GROUPS, CW, CW), BF16),
        compiler_params=_cparams(("arbitrary",)),
        name="s5_within_chunk",
    )(kin)


def _s5_operators(a_re, a_im, log_step, b_re, b_im, c_re, c_im, n_i):
    a_re, a_im, b_re, b_im, c_re, c_im = (t.astype(F32) for t in (a_re, a_im, b_re, b_im, c_re, c_im))
    step = jnp.exp(log_step.astype(F32))[..., None]
    zr, zi = a_re * step, a_im * step

    def power(n):
        n = jnp.asarray(n, F32).reshape(n.shape + (1, 1, 1))
        mag = jnp.exp(n * zr)
        return mag * jnp.cos(n * zi), mag * jnp.sin(n * zi)

    abr, abi = power(np.array(1))
    den = a_re * a_re + a_im * a_im
    fr = ((abr - 1.0) * a_re + abi * a_im) / den
    fi = (abi * a_re - (abr - 1.0) * a_im) / den
    bbr = fr[..., None] * b_re - fi[..., None] * b_im
    bbi = fr[..., None] * b_im + fi[..., None] * b_re

    pr, pi = power(np.arange(CHUNK + 1))
    wr = pr[:CHUNK, ..., None] * bbr - pi[:CHUNK, ..., None] * bbi
    wi = pr[:CHUNK, ..., None] * bbi + pi[:CHUNK, ..., None] * bbr
    kern = (jnp.einsum('dgcp,tdgpe->dgtce', c_re, wr, precision=HI)
            - jnp.einsum('dgcp,tdgpe->dgtce', c_im, wi, precision=HI))

    step_of = _step_of_slot()
    n_q = N_GROUPS // SLOTS
    lag = np.arange(2 * CHUNK - 1)[:, None] - (CHUNK - 1)
    tau = np.arange(CHUNK)[None, :]
    pick = jnp.asarray(np.stack([lag == tau, -lag == tau], axis=1), F32)
    kin = jnp.einsum('xdt,dgtce->gexc', pick, kern, precision=HI)
    kin = jnp.pad(kin.reshape(N_GROUPS, SSM_GROUP, (2 * CHUNK - 1) * SSM_GROUP),
                  ((0, 0), (0, 0), (0, SSM_GROUP)))
    m = _within_chunk(kin)

    onehot = lambda idx: jnp.asarray(idx[..., None] == np.arange(CHUNK + 1), F32)
    grouped = lambda t: t.reshape((t.shape[0], n_q, SLOTS) + t.shape[2:])
    take = lambda oh, t: jnp.einsum('ypn,nqyk->qypk', oh, grouped(t), precision=HI)

    def state_in(d, idx):
        ar, ai = take(onehot(idx), pr[:, d]), take(onehot(idx), pi[:, d])
        br = jnp.swapaxes(grouped(bbr[d][None])[0], -1, -2)
        bi = jnp.swapaxes(grouped(bbi[d][None])[0], -1, -2)
        re = ar[:, :, :, None, :] * br[:, :, None] - ai[:, :, :, None, :] * bi[:, :, None]
        im = ar[:, :, :, None, :] * bi[:, :, None] + ai[:, :, :, None, :] * br[:, :, None]
        return re.reshape(N_GROUPS, CW, SSM_STATE), im.reshape(N_GROUPS, CW, SSM_STATE)

    bf_re, bf_im = state_in(0, CHUNK - 1 - step_of)
    bb_re, bb_im = state_in(1, step_of)
    bs = jnp.concatenate([bf_re, bb_re, bf_im, bb_im], axis=-1).astype(BF16)

    def state_out(d, idx):
        ar, ai = take(onehot(idx), pr[:, d]), take(onehot(idx), pi[:, d])
        cr = jnp.swapaxes(grouped(c_re[d][None])[0], -1, -2)
        ci = jnp.swapaxes(grouped(c_im[d][None])[0], -1, -2)
        ar, ai = jnp.swapaxes(ar, -1, -2)[..., None], jnp.swapaxes(ai, -1, -2)[..., None]
        re = cr[:, :, :, None, :] * ar - ci[:, :, :, None, :] * ai
        im = cr[:, :, :, None, :] * ai + ci[:, :, :, None, :] * ar
        return re.reshape(N_GROUPS, SSM_STATE, CW), im.reshape(N_GROUPS, SSM_STATE, CW)

    cf_re, cf_im = state_out(0, step_of + 1)
    cb_re, cb_im = state_out(1, CHUNK - step_of)
    cs = jnp.concatenate([cf_re, cb_re, -cf_im, -cb_im], axis=1).astype(BF16)

    i_idx = np.arange(n_i)
    pf_re, pf_im = power(CHUNK * i_idx)
    pb_re, pb_im = power(CHUNK * (n_i - 1 - i_idx))
    pw = jnp.concatenate([pf_re[:, 0], pb_re[:, 1], pf_im[:, 0], pb_im[:, 1]], axis=-1)
    pw = jnp.transpose(pw, (1, 0, 2))
    dr, di = power(np.array([CHUNK, CHUNK * n_i]))
    dec = jnp.concatenate([dr[:, 0], dr[:, 1], di[:, 0], di[:, 1]], axis=-1)
    dec = jnp.transpose(dec, (1, 0, 2))
    return bs, jnp.concatenate([m, cs], axis=1), pw, dec


def _rms(x, g):
    return (x * lax.rsqrt(jnp.mean(x * x, axis=-1, keepdims=True) + NORM_EPS)) * g


def _mix_kernel(o1_ref, o4_ref, o16_ref, l1_ref, l4_ref, l16_ref, yg_ref, u_ref, x_ref,
                d_ref, wg_ref, bg_ref, wo_ref, g_ref, out_ref, ys_ref):
    n_chunk = x_ref.shape[0] // CHUNK
    masks = _slot_masks()
    for qt in range(SSM_WIDTH // LANES):
        for hh in range(CHUNK // SLOTS):
            src = [yg_ref[SLOTS * qt + gam, :, hh * LANES:(hh + 1) * LANES] for gam in range(SLOTS)]
            for t8 in range(SLOTS):
                pre = src[SLOTS - 1]
                for gam in range(SLOTS - 2, -1, -1):
                    pre = jnp.where(masks[(gam + t8) % SLOTS], src[gam], pre)
                nat = pre if t8 == 0 else pltpu.roll(pre, LANES - SSM_GROUP * t8, 1)
                ys_ref[qt, pl.ds(SLOTS * hh + t8, n_chunk, stride=CHUNK), :] = nat
    ys = jnp.concatenate([ys_ref[qt] for qt in range(SSM_WIDTH // LANES)], axis=1)

    l1, l4, l16 = l1_ref[...], l4_ref[...], l16_ref[...]
    mx = jnp.maximum(jnp.maximum(l1, l4), l16)
    e1, e4, e16 = jnp.exp(l1 - mx), jnp.exp(l4 - mx), jnp.exp(l16 - mx)
    inv = 1.0 / (e1 + e4 + e16)
    head_row = lax.broadcasted_iota(jnp.int32, (2 * LANES, ATT_WIDTH), 0) % LANES
    head_col = lax.broadcasted_iota(jnp.int32, (2 * LANES, ATT_WIDTH), 1) // HEAD_DIM
    spread = jnp.where(head_row == head_col, 1.0, 0.0).astype(BF16)

    def per_column(w):
        hi = w.astype(BF16)
        lo = (w - hi.astype(F32)).astype(BF16)
        return jnp.dot(jnp.concatenate([hi, lo], axis=1), spread, preferred_element_type=F32)

    att = (per_column(e1 * inv) * o1_ref[...].astype(F32) + per_column(e4 * inv) * o4_ref[...].astype(F32)
           + per_column(e16 * inv) * o16_ref[...].astype(F32))
    y = ys + d_ref[...] * u_ref[...]
    y = 0.5 * y * (1.0 + jnp.tanh(math.sqrt(2.0 / math.pi) * (y + 0.044715 * (y * y * y))))
    gate = jnp.dot(y.astype(BF16), wg_ref[...], preferred_element_type=F32) + bg_ref[...]
    ssm = y * (1.0 / (1.0 + jnp.exp(-gate)))
    mixed = jnp.dot(jnp.concatenate([att.astype(BF16), ssm.astype(BF16)], axis=1), wo_ref[...],
                    preferred_element_type=F32)
    out_ref[...] = x_ref[...] + _rms(mixed, g_ref[...])


def _mix(o1, o4, o16, l1, l4, l16, yg, u, x2, d, wg, bg, wo, g, tile):
    n = x2.shape[0]
    tok = lambda i: (i, 0)
    const = lambda i: (0, 0)
    half = pl.BlockSpec((tile, ATT_WIDTH), tok)
    stat = pl.BlockSpec((tile, LANES), tok)
    return pl.pallas_call(
        _mix_kernel,
        grid=(n // tile,),
        in_specs=[half] * 3 + [stat] * 3 + [
            pl.BlockSpec((N_GROUPS, tile // CHUNK, CW), lambda i: (0, i, 0)),
            half,
            pl.BlockSpec((tile, D_MODEL), tok),
            pl.BlockSpec((1, SSM_WIDTH), const),
            pl.BlockSpec((SSM_WIDTH, SSM_WIDTH), const, pipeline_mode=pl.Buffered(1)),
            pl.BlockSpec((1, SSM_WIDTH), const),
            pl.BlockSpec((D_MODEL, D_MODEL), const, pipeline_mode=pl.Buffered(1)),
            pl.BlockSpec((1, D_MODEL), const),
        ],
        out_specs=pl.BlockSpec((tile, D_MODEL), tok),
        out_shape=jax.ShapeDtypeStruct((n, D_MODEL), F32),
        scratch_shapes=[pltpu.VMEM((SSM_WIDTH // LANES, tile, LANES), F32)],
        compiler_params=_cparams(("arbitrary",)),
        name="mix",
    )(o1, o4, o16, l1, l4, l16, yg, u, x2, d, wg, bg, wo, g)


def _mlp_kernel(x_ref, gpre_ref, wu_ref, wd_ref, gpost_ref, out_ref, *, ff_chunk):
    x = x_ref[...]
    h = _rms(x, gpre_ref[...]).astype(BF16)
    acc = jnp.zeros(x.shape, F32)
    for f in range(D_FF // ff_chunk):
        sl = slice(f * ff_chunk, (f + 1) * ff_chunk)
        a = jnp.maximum(jnp.dot(h, wu_ref[:, sl], preferred_element_type=F32), 0.0)
        acc = acc + jnp.dot((a * a).astype(BF16), wd_ref[sl, :], preferred_element_type=F32)
    out_ref[...] = x + _rms(acc, gpost_ref[...])


def _mlp(x2, gpre, wu, wd, gpost, tile, ff_chunk):
    n = x2.shape[0]
    tok = lambda i: (i, 0)
    const = lambda i: (0, 0)
    return pl.pallas_call(
        functools.partial(_mlp_kernel, ff_chunk=ff_chunk),
        grid=(n // tile,),
        in_specs=[
            pl.BlockSpec((tile, D_MODEL), tok),
            pl.BlockSpec((1, D_MODEL), const),
            pl.BlockSpec((D_MODEL, D_FF), const, pipeline_mode=pl.Buffered(1)),
            pl.BlockSpec((D_FF, D_MODEL), const, pipeline_mode=pl.Buffered(1)),
            pl.BlockSpec((1, D_MODEL), const),
        ],
        out_specs=pl.BlockSpec((tile, D_MODEL), tok),
        out_shape=jax.ShapeDtypeStruct((n, D_MODEL), F32),
        compiler_params=_cparams(("arbitrary",)),
        name="mlp",
    )(x2, gpre, wu, wd, gpost)


def _rotary_tables(seq):
    half = HEAD_DIM // 2
    inv_freq = 1.0 / (ROPE_THETA ** (jnp.arange(half, dtype=F32) / half))
    ang = jnp.arange(seq, dtype=F32)[:, None] * inv_freq[None, :]
    cos, sin = jnp.cos(ang), jnp.sin(ang)
    return jnp.tile(cos, (1, 4)), jnp.tile(jnp.concatenate([-sin, sin], axis=-1), (1, 2))


def _prepare(seq, norm_mix_pre, w_in, a_re, a_im, log_step, b_re, b_im, c_re, c_im, d_skip, w_glu, b_glu,
             w_out, norm_mix_post, norm_mlp_pre, w_up, w_down, norm_mlp_post):
    row = lambda t: t.reshape(1, -1).astype(F32)
    return dict(
        tables=_rotary_tables(seq),
        s5=_s5_operators(a_re, a_im, log_step, b_re, b_im, c_re, c_im, seq // (CHUNK * N_SEG)),
        g_mix_pre=row(norm_mix_pre), w_in=w_in.astype(BF16), d_skip=row(d_skip), w_glu=w_glu.astype(BF16),
        b_glu=row(b_glu), w_out=w_out.astype(BF16), g_mix_post=row(norm_mix_post),
        g_mlp_pre=row(norm_mlp_pre), w_up=w_up.astype(BF16), w_down=w_down.astype(BF16),
        g_mlp_post=row(norm_mlp_post))


def _layer(x, p, tok_tile=1024, attn_tile=2048):
    bsz, seq, _ = x.shape
    n = bsz * seq
    x2 = x.reshape(n, D_MODEL)
    cos_t, sin_t = p['tables']
    bs, mc, pw, dec = p['s5']

    q1, q4, q16, k1, k4, k16, v1, v4, v16, u, ug = _inproj(
        x2, p['g_mix_pre'], p['w_in'], cos_t, sin_t, seq, tok_tile)
    (o1, l1), (o4, l4), (o16, l16) = (
        _banded_attention(q1[None], k1[None], v1[None], 1, seq, attn_tile),
        _banded_attention(q4, k4, v4, 4, seq, attn_tile),
        _banded_attention(q16, k16, v16, 16, seq, attn_tile))
    yg = _s5_scan(ug, bs, mc, pw, dec, bsz)
    x1 = _mix(o1, o4, o16, l1, l4, l16, yg, u, x2, p['d_skip'], p['w_glu'], p['b_glu'], p['w_out'],
              p['g_mix_post'], tok_tile)
    y = _mlp(x1, p['g_mlp_pre'], p['w_up'], p['w_down'], p['g_mlp_post'], tok_tile, 1024)
    return y.reshape(bsz, seq, D_MODEL)


def kernel(x_prompt, x_sample, norm_mix_pre, w_in, ssm_a_re, ssm_a_im, ssm_log_step, ssm_b_re, ssm_b_im,
           ssm_c_re, ssm_c_im, ssm_d, w_glu, b_glu, w_out, norm_mix_post, norm_mlp_pre, w_up, w_down,
           norm_mlp_post):
    weights = (norm_mix_pre, w_in, ssm_a_re, ssm_a_im, ssm_log_step, ssm_b_re, ssm_b_im, ssm_c_re, ssm_c_im,
               ssm_d, w_glu, b_glu, w_out, norm_mix_post, norm_mlp_pre, w_up, w_down, norm_mlp_post)
    depth = norm_mix_pre.shape[0]
    prepared = {}

    def run(x):
        seq = x.shape[1]
        if seq not in prepared:
            prepared[seq] = [_prepare(seq, *(w[l] for w in weights)) for l in range(depth)]
        for l in range(depth):
            x = _layer(x, prepared[seq][l])
        return x

    return run(x_prompt), run(x_sample)
```

```python
import functools
import math

import numpy as np
import jax
import jax.numpy as jnp
from jax import lax
from jax.experimental import pallas as pl
from jax.experimental.pallas import tpu as pltpu

F32 = jnp.float32
BF16 = jnp.bfloat16

D_MODEL = 1024
ATT_WIDTH = 512
SSM_WIDTH = 512
HEAD_DIM = 64
N_HEADS = 8
DILATIONS = ((128, 1), (512, 4), (2048, 16))
RADIUS = 64
SSM_GROUP = 16
N_GROUPS = 32
SSM_STATE = 64
D_FF = 4096
IN_WIDTH = 2048
ROPE_THETA = 10000.0
NORM_EPS = 1e-6
MASK_VALUE = -1e30

CHUNK = 16
LANES = 128
SUBLANES = 8
BF16_ROWS = 16
N_SEG = SUBLANES
SLOTS = LANES // SSM_GROUP
CW = CHUNK * SSM_GROUP
VMEM_LIMIT = 56 * 1024 * 1024
HI = lax.Precision.HIGHEST
LOG2E = 1.4426950408889634
LN2 = 0.6931471805599453


def _cparams(sem):
    return pltpu.CompilerParams(dimension_semantics=sem, vmem_limit_bytes=VMEM_LIMIT)


def _slot_masks():
    lane = lax.broadcasted_iota(jnp.int32, (1, LANES), 1)
    return [(lane // SSM_GROUP) == s for s in range(SLOTS)]


def _step_of_slot():
    gam = np.arange(SLOTS)[:, None, None]
    hh = np.arange(CHUNK // SLOTS)[None, :, None]
    l = np.arange(SLOTS)[None, None, :]
    return (SLOTS * hh + (l - gam) % SLOTS).reshape(SLOTS, CHUNK)


def _inproj_kernel(x_ref, g_ref, w_ref, cos_ref, sin_ref,
                   q1_ref, q4_ref, q16_ref, k1_ref, k4_ref, k16_ref, v1_ref, v4_ref, v16_ref,
                   u_ref, ug_ref, rs_ref, st_ref):
    x = x_ref[...]
    tile = x.shape[0]
    ms = jnp.mean(x * x, axis=-1, keepdims=True)
    h = (x * lax.rsqrt(ms + NORM_EPS)) * g_ref[...]
    proj = jnp.dot(h.astype(BF16), w_ref[...], preferred_element_type=F32)
    cos = cos_ref[...]
    sin = sin_ref[...]
    lane = lax.broadcasted_iota(jnp.int32, cos.shape, 1)
    first_half = (lane & (HEAD_DIM // 2)) == 0
    n_lt = ATT_WIDTH // LANES

    def rot(t):
        partner = jnp.where(first_half, pltpu.roll(t, LANES - HEAD_DIM // 2, 1),
                            pltpu.roll(t, HEAD_DIM // 2, 1))
        return t * cos + partner * sin

    def spread(lane_tile, nat_ref, d4_ref, d16_ref):
        for c in range(n_lt):
            rs_ref[c] = lane_tile(c)
            nat_ref[:, c * LANES:(c + 1) * LANES] = rs_ref[c].astype(BF16)
        for c in range(n_lt):
            for r4 in range(4):
                blk = rs_ref[c, pl.ds(r4, tile // 4, stride=4), :]
                d4_ref[r4, :, c * LANES:(c + 1) * LANES] = blk.astype(BF16)
                st_ref[c * 4 + r4] = blk
        for c in range(n_lt):
            for r4 in range(4):
                for r2 in range(4):
                    blk = st_ref[c * 4 + r4, pl.ds(r2, tile // 16, stride=4), :]
                    d16_ref[r4 + 4 * r2, :, c * LANES:(c + 1) * LANES] = blk.astype(BF16)

    q_scale = (HEAD_DIM ** -0.5) * LOG2E
    spread(lambda c: rot(proj[:, c * LANES:(c + 1) * LANES]) * q_scale, q1_ref, q4_ref, q16_ref)
    spread(lambda c: rot(proj[:, ATT_WIDTH + c * LANES:ATT_WIDTH + (c + 1) * LANES]), k1_ref, k4_ref, k16_ref)
    spread(lambda c: proj[:, 2 * ATT_WIDTH + c * LANES:2 * ATT_WIDTH + (c + 1) * LANES], v1_ref, v4_ref, v16_ref)
    u = proj[:, 3 * ATT_WIDTH:]
    u_ref[...] = u

    n_chunk = tile // CHUNK
    masks = _slot_masks()
    for qt in range(SSM_WIDTH // LANES):
        rs_ref[qt] = u[:, qt * LANES:(qt + 1) * LANES]
    for qt in range(SSM_WIDTH // LANES):
        for b in range(4):
            st_ref[b] = rs_ref[qt, pl.ds(b, tile // 4, stride=4), :]
        for hh in range(CHUNK // SLOTS):
            rolled = []
            for t8 in range(SLOTS):
                t = SLOTS * hh + t8
                step_rows = st_ref[t % 4, pl.ds(t // 4, n_chunk, stride=4), :]
                rolled.append(step_rows if t8 == 0 else pltpu.roll(step_rows, SSM_GROUP * t8, 1))
            for gam in range(SLOTS):
                res = rolled[SLOTS - 1]
                for t8 in range(SLOTS - 2, -1, -1):
                    res = jnp.where(masks[(gam + t8) % SLOTS], rolled[t8], res)
                ug_ref[SLOTS * qt + gam, :, hh * LANES:(hh + 1) * LANES] = res.astype(BF16)


def _inproj(x2, g, w_bf, cos_t, sin_t, seq, tile):
    n = x2.shape[0]
    n_pos = seq // tile
    tok = lambda i: (i, 0)
    pos = lambda i: (i % n_pos, 0)
    const = lambda i: (0, 0)
    split = lambda i: (0, i, 0)
    qkv_specs = [pl.BlockSpec((tile, ATT_WIDTH), tok), pl.BlockSpec((4, tile // 4, ATT_WIDTH), split),
                 pl.BlockSpec((16, tile // 16, ATT_WIDTH), split)]
    qkv_shapes = [jax.ShapeDtypeStruct((n, ATT_WIDTH), BF16), jax.ShapeDtypeStruct((4, n // 4, ATT_WIDTH), BF16),
                  jax.ShapeDtypeStruct((16, n // 16, ATT_WIDTH), BF16)]
    return pl.pallas_call(
        _inproj_kernel,
        grid=(n // tile,),
        in_specs=[
            pl.BlockSpec((tile, D_MODEL), tok),
            pl.BlockSpec((1, D_MODEL), const),
            pl.BlockSpec((D_MODEL, IN_WIDTH), const, pipeline_mode=pl.Buffered(1)),
            pl.BlockSpec((tile, LANES), pos),
            pl.BlockSpec((tile, LANES), pos),
        ],
        out_specs=qkv_specs * 3 + [pl.BlockSpec((tile, SSM_WIDTH), tok),
                                   pl.BlockSpec((N_GROUPS, tile // CHUNK, CW), split)],
        out_shape=qkv_shapes * 3 + [jax.ShapeDtypeStruct((n, SSM_WIDTH), F32),
                                    jax.ShapeDtypeStruct((N_GROUPS, n // CHUNK, CW), BF16)],
        scratch_shapes=[pltpu.VMEM((ATT_WIDTH // LANES, tile, LANES), F32),
                        pltpu.VMEM((4 * ATT_WIDTH // LANES, tile // 4, LANES), F32)],
        compiler_params=_cparams(("arbitrary",)),
        name="inproj",
    )(x2, g, w_bf, cos_t, sin_t)


def _attn_kernel(q_ref, kp_ref, kc_ref, kn_ref, vp_ref, vc_ref, vn_ref, o_ref, lse_ref,
                 kx_ref, vx_ref, bias_ref, s_ref, p_ref, m_ref, ost_ref, lst_ref, nat_ref, mid_ref,
                 *, dil, tm, qb, tiles_per_seq):
    tile_in_seq = pl.program_id(0) % tiles_per_seq
    n_sb = tm // qb
    if n_sb > 1:
        kx_ref[:, 0:RADIUS] = kp_ref[...]
        kx_ref[:, RADIUS:RADIUS + tm] = kc_ref[...]
        kx_ref[:, RADIUS + tm:] = kn_ref[...]
        vx_ref[:, 0:RADIUS] = vp_ref[...]
        vx_ref[:, RADIUS:RADIUS + tm] = vc_ref[...]
        vx_ref[:, RADIUS + tm:] = vn_ref[...]

    def window(prev_ref, cur_ref, next_ref, ext_ref, r, row0, sl):
        if n_sb > 1:
            return ext_ref[r, row0:row0 + win, sl]
        return jnp.concatenate([prev_ref[r, :, sl], cur_ref[r, :, sl], next_ref[r, :, sl]], axis=0)

    win = qb + 2 * RADIUS
    rows = 2 * qb
    qi = lax.broadcasted_iota(jnp.int32, (rows, win), 0) & (qb - 1)
    kcol = lax.broadcasted_iota(jnp.int32, (rows, win), 1)
    band = (kcol >= qi) & (kcol <= qi + 2 * RADIUS)
    after_start = kcol >= RADIUS
    before_end = kcol < qb + RADIUS
    neg = jnp.full((rows, win), MASK_VALUE, F32)
    zero = jnp.zeros((rows, win), F32)
    bias_ref[0] = jnp.where(band, zero, neg)
    bias_ref[1] = jnp.where(band & after_start, zero, neg)
    bias_ref[2] = jnp.where(band & before_end, zero, neg)
    bias_ref[3] = jnp.where(band & after_start & before_end, zero, neg)
    first_tile = (tile_in_seq == 0).astype(jnp.int32)
    last_tile = (tile_in_seq == tiles_per_seq - 1).astype(jnp.int32)

    lane = lax.broadcasted_iota(jnp.int32, (qb, LANES), 1)
    head_a = lane < HEAD_DIM
    ones = jnp.ones((win, LANES), BF16)
    n_pair = N_HEADS // 2

    def item(r, sb):
        row0 = sb * qb
        edge = (first_tile if sb == 0 else 0) + (2 * last_tile if sb == n_sb - 1 else 0)
        bias = bias_ref[edge]
        for p in range(n_pair):
            sl = slice(p * LANES, (p + 1) * LANES)
            q2 = q_ref[r, row0:row0 + qb, sl]
            zq = jnp.zeros_like(q2)
            qs = jnp.concatenate([jnp.where(head_a, q2, zq), jnp.where(head_a, zq, q2)], axis=0)
            kw = window(kp_ref, kc_ref, kn_ref, kx_ref, r, row0, sl)
            s_ref[p] = lax.dot_general(qs, kw, (((1,), (1,)), ((), ())), preferred_element_type=F32)
        for p in range(n_pair):
            s = s_ref[p] + bias
            m = jnp.max(s, axis=-1, keepdims=True)
            p_ref[p] = jnp.exp2(s - m).astype(BF16)
            m_ref[p] = jnp.broadcast_to(m, (rows, LANES))
        m8 = jnp.zeros((qb, LANES), F32)
        l8 = jnp.ones((qb, LANES), F32)
        for p in range(n_pair):
            sl = slice(p * LANES, (p + 1) * LANES)
            vaug = jnp.concatenate([window(vp_ref, vc_ref, vn_ref, vx_ref, r, row0, sl), ones], axis=1)
            res = jnp.dot(p_ref[p], vaug, preferred_element_type=F32)
            den = res[:, LANES:]
            mrow = m_ref[p]
            acc = jnp.where(head_a, res[:qb, :LANES], res[qb:, :LANES])
            out = acc / jnp.where(head_a, den[:qb], den[qb:])
            if dil == 1:
                o_ref[row0:row0 + qb, sl] = out.astype(o_ref.dtype)
            else:
                ost_ref[r, row0:row0 + qb, sl] = out
            m8 = jnp.where(lane == 2 * p, mrow[:qb], jnp.where(lane == 2 * p + 1, mrow[qb:], m8))
            l8 = jnp.where(lane == 2 * p, den[:qb], jnp.where(lane == 2 * p + 1, den[qb:], l8))
        lse = m8 * LN2 + jnp.log(l8)
        if dil == 1:
            lse_ref[row0:row0 + qb, :] = lse
        else:
            lst_ref[r, row0:row0 + qb, :] = lse

    for w in range(dil * n_sb):
        item(w // n_sb, w % n_sb)

    n_lt = ATT_WIDTH // LANES
    if dil > 1:
        def gather(stage_ref, lt):
            sl = slice(lt * LANES, (lt + 1) * LANES)
            if dil == 4:
                for r in range(4):
                    nat_ref[pl.ds(r, tm, stride=4), :] = stage_ref[r, :, sl]
            else:
                for r4 in range(4):
                    for r2 in range(4):
                        mid_ref[r4, pl.ds(r2, tm, stride=4), :] = stage_ref[r4 + 4 * r2, :, sl]
                for r4 in range(4):
                    nat_ref[pl.ds(r4, 4 * tm, stride=4), :] = mid_ref[r4]
            return nat_ref[...]

        for lt in range(n_lt):
            o_ref[:, lt * LANES:(lt + 1) * LANES] = gather(ost_ref, lt).astype(o_ref.dtype)
        lse_ref[...] = gather(lst_ref, 0)


def _banded_attention(q, k, v, dil, seq, nat_tile):
    _, n_rows, _ = q.shape
    n = n_rows * dil
    nat_tile = min(nat_tile, seq)
    tm = nat_tile // dil
    qb = min(2 * RADIUS, tm)
    assert dil in (1, 4, 16) and seq % nat_tile == 0 and tm % qb == 0 and qb % RADIUS == 0
    hb = tm // RADIUS
    n_halo = n_rows // RADIUS
    win = qb + 2 * RADIUS
    ext_shape = (dil, tm + 2 * RADIUS, ATT_WIDTH) if tm > qb else (1, BF16_ROWS, LANES)
    stage_rows = tm if dil > 1 else SUBLANES
    cur = pl.BlockSpec((dil, tm, ATT_WIDTH), lambda i: (0, i, 0))
    prev = pl.BlockSpec((dil, RADIUS, ATT_WIDTH), lambda i: (0, jnp.maximum(i * hb - 1, 0), 0))
    nxt = pl.BlockSpec((dil, RADIUS, ATT_WIDTH), lambda i: (0, jnp.minimum((i + 1) * hb, n_halo - 1), 0))
    return pl.pallas_call(
        functools.partial(_attn_kernel, dil=dil, tm=tm, qb=qb, tiles_per_seq=seq // nat_tile),
        grid=(n // nat_tile,),
        in_specs=[cur, prev, cur, nxt, prev, cur, nxt],
        out_specs=[pl.BlockSpec((nat_tile, ATT_WIDTH), lambda i: (i, 0)),
                   pl.BlockSpec((nat_tile, LANES), lambda i: (i, 0))],
        out_shape=[jax.ShapeDtypeStruct((n, ATT_WIDTH), BF16), jax.ShapeDtypeStruct((n, LANES), F32)],
        scratch_shapes=[
            pltpu.VMEM(ext_shape, BF16),
            pltpu.VMEM(ext_shape, BF16),
            pltpu.VMEM((4, 2 * qb, win), F32),
            pltpu.VMEM((N_HEADS // 2, 2 * qb, win), F32),
            pltpu.VMEM((N_HEADS // 2, 2 * qb, win), BF16),
            pltpu.VMEM((N_HEADS // 2, 2 * qb, LANES), F32),
            pltpu.VMEM((dil, stage_rows, ATT_WIDTH), F32),
            pltpu.VMEM((dil, stage_rows, LANES), F32),
            pltpu.VMEM((dil * stage_rows, LANES), F32),
            pltpu.VMEM((4, dil * stage_rows // 4, LANES), F32),
        ],
        compiler_params=_cparams(("arbitrary",)),
        name=f"attn_d{dil}",
    )(q, k, k, k, v, v, v)


def _s5_kernel(ug_ref, bs_ref, mc_ref, pw_ref, dec_ref, y_ref, x_ref, h_ref, *, n_i, n_grp):
    half = LANES // 2
    for g in range(n_grp):
        z = jnp.dot(ug_ref[g], bs_ref[g], preferred_element_type=F32)
        for seg in range(N_SEG):
            rows = slice(seg * n_i, (seg + 1) * n_i)
            x_ref[2 * g, pl.ds(seg, n_i, stride=N_SEG), :] = z[rows, :LANES]
            x_ref[2 * g + 1, pl.ds(seg, n_i, stride=N_SEG), :] = z[rows, LANES:]
    lane = lax.broadcasted_iota(jnp.int32, (N_SEG, LANES), 1)
    fwd = lane < half
    a_re = [jnp.broadcast_to(dec_ref[g, 0:1, :LANES], (N_SEG, LANES)) for g in range(n_grp)]
    a_im = [jnp.broadcast_to(dec_ref[g, 0:1, LANES:], (N_SEG, LANES)) for g in range(n_grp)]

    def scan_body(s, carry):
        rf = pl.ds(pl.multiple_of(s * N_SEG, N_SEG), N_SEG)
        rb = pl.ds(pl.multiple_of((n_i - 1 - s) * N_SEG, N_SEG), N_SEG)
        out = []
        for g in range(n_grp):
            hre, him = carry[2 * g], carry[2 * g + 1]
            h_ref[2 * g, rf, 0:half] = hre[:, :half]
            h_ref[2 * g, rb, half:] = hre[:, half:]
            h_ref[2 * g + 1, rf, 0:half] = him[:, :half]
            h_ref[2 * g + 1, rb, half:] = him[:, half:]
            xre = jnp.where(fwd, x_ref[2 * g, rf, :], x_ref[2 * g, rb, :])
            xim = jnp.where(fwd, x_ref[2 * g + 1, rf, :], x_ref[2 * g + 1, rb, :])
            out += [a_re[g] * hre - a_im[g] * him + xre, a_re[g] * him + a_im[g] * hre + xim]
        return tuple(out)

    zero = jnp.zeros((N_SEG, LANES), F32)
    ends = lax.fori_loop(0, n_i, scan_body, (zero,) * (2 * n_grp))

    lane1 = lax.broadcasted_iota(jnp.int32, (1, LANES), 1)
    fwd1 = lane1 < half
    entry = []
    for g in range(n_grp):
        l_re, l_im = ends[2 * g], ends[2 * g + 1]
        s_re = dec_ref[g, 1:2, :LANES]
        s_im = dec_ref[g, 1:2, LANES:]
        e_re = jnp.zeros((1, LANES), F32)
        e_im = jnp.zeros((1, LANES), F32)
        ins = []
        for s in range(N_SEG):
            ins.append((e_re, e_im))
            lre = jnp.where(fwd1, l_re[s:s + 1], l_re[N_SEG - 1 - s:N_SEG - s])
            lim = jnp.where(fwd1, l_im[s:s + 1], l_im[N_SEG - 1 - s:N_SEG - s])
            e_re, e_im = s_re * e_re - s_im * e_im + lre, s_re * e_im + s_im * e_re + lim
        entry.append((
            jnp.concatenate([jnp.where(fwd1, ins[s][0], ins[N_SEG - 1 - s][0]) for s in range(N_SEG)], axis=0),
            jnp.concatenate([jnp.where(fwd1, ins[s][1], ins[N_SEG - 1 - s][1]) for s in range(N_SEG)], axis=0)))

    def fix_body(ib, c):
        for g in range(n_grp):
            ein_re, ein_im = entry[g]
            pw = pw_ref[g, pl.ds(pl.multiple_of(ib * SUBLANES, SUBLANES), SUBLANES), :]
            for r in range(SUBLANES):
                rows = pl.ds(pl.multiple_of((ib * SUBLANES + r) * N_SEG, N_SEG), N_SEG)
                p_re = pw[r:r + 1, :LANES]
                p_im = pw[r:r + 1, LANES:]
                h_ref[2 * g, rows, :] = h_ref[2 * g, rows, :] + (p_re * ein_re - p_im * ein_im)
                h_ref[2 * g + 1, rows, :] = h_ref[2 * g + 1, rows, :] + (p_re * ein_im + p_im * ein_re)
        return c

    lax.fori_loop(0, n_i // SUBLANES, fix_body, 0)
    for g in range(n_grp):
        ungroup = lambda c: jnp.concatenate(
            [h_ref[c, pl.ds(seg, n_i, stride=N_SEG), :] for seg in range(N_SEG)], axis=0)
        hcat = jnp.concatenate([ungroup(2 * g), ungroup(2 * g + 1)], axis=1).astype(BF16)
        y_ref[g] = jnp.dot(jnp.concatenate([ug_ref[g], hcat], axis=1), mc_ref[g], preferred_element_type=F32)


def _s5_scan(ug, bs, mc, pw, dec, bsz, n_grp=8):
    n_g, total_rows, width = ug.shape
    rows = total_rows // bsz
    n_i = rows // N_SEG
    grp = lambda b, g: (g, 0, 0)
    seq_rows = pl.BlockSpec((n_grp, rows, width), lambda b, g: (g, b, 0))
    return pl.pallas_call(
        functools.partial(_s5_kernel, n_i=n_i, n_grp=n_grp),
        grid=(bsz, n_g // n_grp),
        in_specs=[
            seq_rows,
            pl.BlockSpec((n_grp, width, width), grp),
            pl.BlockSpec((n_grp, 2 * width, width), grp),
            pl.BlockSpec((n_grp, n_i, width), grp),
            pl.BlockSpec((n_grp, 2, width), grp),
        ],
        out_specs=seq_rows,
        out_shape=jax.ShapeDtypeStruct((n_g, total_rows, width), F32),
        scratch_shapes=[pltpu.VMEM((2 * n_grp, rows, LANES), F32)] * 2,
        compiler_params=_cparams(("arbitrary",) * 2),
        name="s5_scan",
    )(ug, bs, mc, pw, dec)


def _within_chunk_kernel(kin_ref, m_ref):
    step_of = _step_of_slot()
    masks = _slot_masks()
    for gam in range(SLOTS):
        rolled = {}

        def piece(x, slot):
            key = (x // SLOTS, (slot - x) % SLOTS)
            if key not in rolled:
                src = kin_ref[gam, :, key[0] * LANES:(key[0] + 1) * LANES]
                rolled[key] = src if key[1] == 0 else pltpu.roll(src, SSM_GROUP * key[1], 1)
            return rolled[key]

        for s_pos in range(CHUNK):
            for hh in range(CHUNK // SLOTS):
                lag = lambda l: int(step_of[gam, SLOTS * hh + l] - step_of[gam, s_pos]) + CHUNK - 1
                acc = piece(lag(SLOTS - 1), SLOTS - 1)
                for l in range(SLOTS - 2, -1, -1):
                    acc = jnp.where(masks[l], piece(lag(l), l), acc)
                m_ref[gam, s_pos * SSM_GROUP:(s_pos + 1) * SSM_GROUP, hh * LANES:(hh + 1) * LANES] = acc.astype(BF16)


def _within_chunk(kin):
    n_lag_lanes = kin.shape[-1]
    return pl.pallas_call(
        _within_chunk_kernel,
        grid=(N_GROUPS // SLOTS,),
        in_specs=[pl.BlockSpec((SLOTS, SSM_GROUP, n_lag_lanes), lambda i: (i, 0, 0))],
        out_specs=pl.BlockSpec((SLOTS, CW, CW), lambda i: (i, 0, 0)),
        out_shape=jax.ShapeDtypeStruct((N_GROUPS, CW, CW), BF16),
        compiler_params=_cparams(("arbitrary",)),
        name="s5_within_chunk",
    )(kin)


def _s5_operators(a_re, a_im, log_step, b_re, b_im, c_re, c_im, n_i):
    a_re, a_im, b_re, b_im, c_re, c_im = (t.astype(F32) for t in (a_re, a_im, b_re, b_im, c_re, c_im))
    step = jnp.exp(log_step.astype(F32))[..., None]
    zr, zi = a_re * step, a_im * step

    def power(n):
        n = jnp.asarray(n, F32).reshape(n.shape + (1, 1, 1))
        mag = jnp.exp(n * zr)
        return mag * jnp.cos(n * zi), mag * jnp.sin(n * zi)

    abr, abi = power(np.array(1))
    den = a_re * a_re + a_im * a_im
    fr = ((abr - 1.0) * a_re + abi * a_im) / den
    fi = (abi * a_re - (abr - 1.0) * a_im) / den
    bbr = fr[..., None] * b_re - fi[..., None] * b_im
    bbi = fr[..., None] * b_im + fi[..., None] * b_re

    pr, pi = power(np.arange(CHUNK + 1))
    wr = pr[:CHUNK, ..., None] * bbr - pi[:CHUNK, ..., None] * bbi
    wi = pr[:CHUNK, ..., None] * bbi + pi[:CHUNK, ..., None] * bbr
    kern = (jnp.einsum('dgcp,tdgpe->dgtce', c_re, wr, precision=HI)
            - jnp.einsum('dgcp,tdgpe->dgtce', c_im, wi, precision=HI))

    step_of = _step_of_slot()
    n_q = N_GROUPS // SLOTS
    lag = np.arange(2 * CHUNK - 1)[:, None] - (CHUNK - 1)
    tau = np.arange(CHUNK)[None, :]
    pick = jnp.asarray(np.stack([lag == tau, -lag == tau], axis=1), F32)
    kin = jnp.einsum('xdt,dgtce->gexc', pick, kern, precision=HI)
    kin = jnp.pad(kin.reshape(N_GROUPS, SSM_GROUP, (2 * CHUNK - 1) * SSM_GROUP),
                  ((0, 0), (0, 0), (0, SSM_GROUP)))
    m = _within_chunk(kin)

    onehot = lambda idx: jnp.asarray(idx[..., None] == np.arange(CHUNK + 1), F32)
    grouped = lambda t: t.reshape((t.shape[0], n_q, SLOTS) + t.shape[2:])
    take = lambda oh, t: jnp.einsum('ypn,nqyk->qypk', oh, grouped(t), precision=HI)

    def state_in(d, idx):
        ar, ai = take(onehot(idx), pr[:, d]), take(onehot(idx), pi[:, d])
        br = jnp.swapaxes(grouped(bbr[d][None])[0], -1, -2)
        bi = jnp.swapaxes(grouped(bbi[d][None])[0], -1, -2)
        re = ar[:, :, :, None, :] * br[:, :, None] - ai[:, :, :, None, :] * bi[:, :, None]
        im = ar[:, :, :, None, :] * bi[:, :, None] + ai[:, :, :, None, :] * br[:, :, None]
        return re.reshape(N_GROUPS, CW, SSM_STATE), im.reshape(N_GROUPS, CW, SSM_STATE)

    bf_re, bf_im = state_in(0, CHUNK - 1 - step_of)
    bb_re, bb_im = state_in(1, step_of)
    bs = jnp.concatenate([bf_re, bb_re, bf_im, bb_im], axis=-1).astype(BF16)

    def state_out(d, idx):
        ar, ai = take(onehot(idx), pr[:, d]), take(onehot(idx), pi[:, d])
        cr = jnp.swapaxes(grouped(c_re[d][None])[0], -1, -2)
        ci = jnp.swapaxes(grouped(c_im[d][None])[0], -1, -2)
        ar, ai = jnp.swapaxes(ar, -1, -2)[..., None], jnp.swapaxes(ai, -1, -2)[..., None]
        re = cr[:, :, :, None, :] * ar - ci[:, :, :, None, :] * ai
        im = cr[:, :, :, None, :] * ai + ci[:, :, :, None, :] * ar
        return re.reshape(N_GROUPS, SSM_STATE, CW), im.reshape(N_GROUPS, SSM_STATE, CW)

    cf_re, cf_im = state_out(0, step_of + 1)
    cb_re, cb_im = state_out(1, CHUNK - step_of)
    cs = jnp.concatenate([cf_re, cb_re, -cf_im, -cb_im], axis=1).astype(BF16)

    i_idx = np.arange(n_i)
    pf_re, pf_im = power(CHUNK * i_idx)
    pb_re, pb_im = power(CHUNK * (n_i - 1 - i_idx))
    pw = jnp.concatenate([pf_re[:, 0], pb_re[:, 1], pf_im[:, 0], pb_im[:, 1]], axis=-1)
    pw = jnp.transpose(pw, (1, 0, 2))
    dr, di = power(np.array([CHUNK, CHUNK * n_i]))
    dec = jnp.concatenate([dr[:, 0], dr[:, 1], di[:, 0], di[:, 1]], axis=-1)
    dec = jnp.transpose(dec, (1, 0, 2))
    return bs, jnp.concatenate([m, cs], axis=1), pw, dec


def _rms(x, g):
    return (x * lax.rsqrt(jnp.mean(x * x, axis=-1, keepdims=True) + NORM_EPS)) * g


def _mix_kernel(o1_ref, o4_ref, o16_ref, l1_ref, l4_ref, l16_ref, yg_ref, u_ref, x_ref,
                d_ref, wg_ref, bg_ref, wo_ref, g_ref, out_ref, ys_ref, yq_ref):
    n_chunk = x_ref.shape[0] // CHUNK
    masks = _slot_masks()
    for qt in range(SSM_WIDTH // LANES):
        for hh in range(CHUNK // SLOTS):
            src = [yg_ref[SLOTS * qt + gam, :, hh * LANES:(hh + 1) * LANES] for gam in range(SLOTS)]
            for t8 in range(SLOTS):
                pre = src[SLOTS - 1]
                for gam in range(SLOTS - 2, -1, -1):
                    pre = jnp.where(masks[(gam + t8) % SLOTS], src[gam], pre)
                nat = pre if t8 == 0 else pltpu.roll(pre, LANES - SSM_GROUP * t8, 1)
                t = SLOTS * hh + t8
                yq_ref[t % 4, pl.ds(t // 4, n_chunk, stride=4), :] = nat
        for b in range(4):
            ys_ref[qt, pl.ds(b, 4 * n_chunk, stride=4), :] = yq_ref[b]
    ys = jnp.concatenate([ys_ref[qt] for qt in range(SSM_WIDTH // LANES)], axis=1)

    l1, l4, l16 = l1_ref[...], l4_ref[...], l16_ref[...]
    mx = jnp.maximum(jnp.maximum(l1, l4), l16)
    e1, e4, e16 = jnp.exp(l1 - mx), jnp.exp(l4 - mx), jnp.exp(l16 - mx)
    inv = 1.0 / (e1 + e4 + e16)
    head_row = lax.broadcasted_iota(jnp.int32, (2 * LANES, ATT_WIDTH), 0) % LANES
    head_col = lax.broadcasted_iota(jnp.int32, (2 * LANES, ATT_WIDTH), 1) // HEAD_DIM
    spread = jnp.where(head_row == head_col, 1.0, 0.0).astype(BF16)

    def per_column(w):
        hi = w.astype(BF16)
        lo = (w - hi.astype(F32)).astype(BF16)
        return jnp.dot(jnp.concatenate([hi, lo], axis=1), spread, preferred_element_type=F32)

    att = (per_column(e1 * inv) * o1_ref[...].astype(F32) + per_column(e4 * inv) * o4_ref[...].astype(F32)
           + per_column(e16 * inv) * o16_ref[...].astype(F32))
    y = ys + d_ref[...] * u_ref[...]
    y = 0.5 * y * (1.0 + jnp.tanh(math.sqrt(2.0 / math.pi) * (y + 0.044715 * (y * y * y))))
    gate = jnp.dot(y.astype(BF16), wg_ref[...], preferred_element_type=F32) + bg_ref[...]
    ssm = y * (1.0 / (1.0 + jnp.exp(-gate)))
    mixed = jnp.dot(jnp.concatenate([att.astype(BF16), ssm.astype(BF16)], axis=1), wo_ref[...],
                    preferred_element_type=F32)
    out_ref[...] = x_ref[...] + _rms(mixed, g_ref[...])


def _mix(o1, o4, o16, l1, l4, l16, yg, u, x2, d, wg, bg, wo, g, tile):
    n = x2.shape[0]
    tok = lambda i: (i, 0)
    const = lambda i: (0, 0)
    half = pl.BlockSpec((tile, ATT_WIDTH), tok)
    stat = pl.BlockSpec((tile, LANES), tok)
    return pl.pallas_call(
        _mix_kernel,
        grid=(n // tile,),
        in_specs=[half] * 3 + [stat] * 3 + [
            pl.BlockSpec((N_GROUPS, tile // CHUNK, CW), lambda i: (0, i, 0)),
            half,
            pl.BlockSpec((tile, D_MODEL), tok),
            pl.BlockSpec((1, SSM_WIDTH), const),
            pl.BlockSpec((SSM_WIDTH, SSM_WIDTH), const, pipeline_mode=pl.Buffered(1)),
            pl.BlockSpec((1, SSM_WIDTH), const),
            pl.BlockSpec((D_MODEL, D_MODEL), const, pipeline_mode=pl.Buffered(1)),
            pl.BlockSpec((1, D_MODEL), const),
        ],
        out_specs=pl.BlockSpec((tile, D_MODEL), tok),
        out_shape=jax.ShapeDtypeStruct((n, D_MODEL), F32),
        scratch_shapes=[pltpu.VMEM((SSM_WIDTH // LANES, tile, LANES), F32),
                        pltpu.VMEM((4, tile // 4, LANES), F32)],
        compiler_params=_cparams(("arbitrary",)),
        name="mix",
    )(o1, o4, o16, l1, l4, l16, yg, u, x2, d, wg, bg, wo, g)


def _mlp_kernel(x_ref, gpre_ref, wu_ref, wd_ref, gpost_ref, out_ref, *, ff_chunk):
    x = x_ref[...]
    h = _rms(x, gpre_ref[...]).astype(BF16)
    acc = jnp.zeros(x.shape, F32)
    for f in range(D_FF // ff_chunk):
        sl = slice(f * ff_chunk, (f + 1) * ff_chunk)
        a = jnp.maximum(jnp.dot(h, wu_ref[:, sl], preferred_element_type=F32), 0.0)
        acc = acc + jnp.dot((a * a).astype(BF16), wd_ref[sl, :], preferred_element_type=F32)
    out_ref[...] = x + _rms(acc, gpost_ref[...])


def _mlp(x2, gpre, wu, wd, gpost, tile, ff_chunk):
    n = x2.shape[0]
    tok = lambda i: (i, 0)
    const = lambda i: (0, 0)
    return pl.pallas_call(
        functools.partial(_mlp_kernel, ff_chunk=ff_chunk),
        grid=(n // tile,),
        in_specs=[
            pl.BlockSpec((tile, D_MODEL), tok),
            pl.BlockSpec((1, D_MODEL), const),
            pl.BlockSpec((D_MODEL, D_FF), const, pipeline_mode=pl.Buffered(1)),
            pl.BlockSpec((D_FF, D_MODEL), const, pipeline_mode=pl.Buffered(1)),
            pl.BlockSpec((1, D_MODEL), const),
        ],
        out_specs=pl.BlockSpec((tile, D_MODEL), tok),
        out_shape=jax.ShapeDtypeStruct((n, D_MODEL), F32),
        compiler_params=_cparams(("arbitrary",)),
        name="mlp",
    )(x2, gpre, wu, wd, gpost)


def _rotary_tables(seq):
    half = HEAD_DIM // 2
    inv_freq = 1.0 / (ROPE_THETA ** (jnp.arange(half, dtype=F32) / half))
    ang = jnp.arange(seq, dtype=F32)[:, None] * inv_freq[None, :]
    cos, sin = jnp.cos(ang), jnp.sin(ang)
    return jnp.tile(cos, (1, 4)), jnp.tile(jnp.concatenate([-sin, sin], axis=-1), (1, 2))


def _prepare(seq, norm_mix_pre, w_in, a_re, a_im, log_step, b_re, b_im, c_re, c_im, d_skip, w_glu, b_glu,
             w_out, norm_mix_post, norm_mlp_pre, w_up, w_down, norm_mlp_post):
    row = lambda t: t.reshape(1, -1).astype(F32)
    return dict(
        tables=_rotary_tables(seq),
        s5=_s5_operators(a_re, a_im, log_step, b_re, b_im, c_re, c_im, seq // (CHUNK * N_SEG)),
        g_mix_pre=row(norm_mix_pre), w_in=w_in.astype(BF16), d_skip=row(d_skip), w_glu=w_glu.astype(BF16),
        b_glu=row(b_glu), w_out=w_out.astype(BF16), g_mix_post=row(norm_mix_post),
        g_mlp_pre=row(norm_mlp_pre), w_up=w_up.astype(BF16), w_down=w_down.astype(BF16),
        g_mlp_post=row(norm_mlp_post))


def _layer(x, p, tok_tile=1024, attn_tile=2048):
    bsz, seq, _ = x.shape
    n = bsz * seq
    x2 = x.reshape(n, D_MODEL)
    cos_t, sin_t = p['tables']
    bs, mc, pw, dec = p['s5']

    q1, q4, q16, k1, k4, k16, v1, v4, v16, u, ug = _inproj(
        x2, p['g_mix_pre'], p['w_in'], cos_t, sin_t, seq, tok_tile)
    (o1, l1), (o4, l4), (o16, l16) = (
        _banded_attention(q1[None], k1[None], v1[None], 1, seq, attn_tile),
        _banded_attention(q4, k4, v4, 4, seq, attn_tile),
        _banded_attention(q16, k16, v16, 16, seq, attn_tile))
    yg = _s5_scan(ug, bs, mc, pw, dec, bsz)
    x1 = _mix(o1, o4, o16, l1, l4, l16, yg, u, x2, p['d_skip'], p['w_glu'], p['b_glu'], p['w_out'],
              p['g_mix_post'], tok_tile)
    y = _mlp(x1, p['g_mlp_pre'], p['w_up'], p['w_down'], p['g_mlp_post'], tok_tile, 1024)
    return y.reshape(bsz, seq, D_MODEL)


def kernel(x_prompt, x_sample, norm_mix_pre, w_in, ssm_a_re, ssm_a_im, ssm_log_step, ssm_b_re, ssm_b_im,
           ssm_c_re, ssm_c_im, ssm_d, w_glu, b_glu, w_out, norm_mix_post, norm_mlp_pre, w_up, w_down,
           norm_mlp_post):
    weights = (norm_mix_pre, w_in, ssm_a_re, ssm_a_im, ssm_log_step, ssm_b_re, ssm_b_im, ssm_c_re, ssm_c_im,
               ssm_d, w_glu, b_glu, w_out, norm_mix_post, norm_mlp_pre, w_up, w_down, norm_mlp_post)
    depth = norm_mix_pre.shape[0]
    prepared = {}

    def run(x):
        seq = x.shape[1]
        if seq not in prepared:
            prepared[seq] = [_prepare(seq, *(w[l] for w in weights)) for l in range(depth)]
        for l in range(depth):
            x = _layer(x, prepared[seq][l])
        return x

    return run(x_prompt), run(x_sample)
```

```python
import functools
import math

import numpy as np
import jax
import jax.numpy as jnp
from jax import lax
from jax.experimental import pallas as pl
from jax.experimental.pallas import tpu as pltpu

F32 = jnp.float32
BF16 = jnp.bfloat16

D_MODEL = 1024
ATT_WIDTH = 512
SSM_WIDTH = 512
HEAD_DIM = 64
N_HEADS = 8
DILATIONS = ((128, 1), (512, 4), (2048, 16))
RADIUS = 64
SSM_GROUP = 16
N_GROUPS = 32
SSM_STATE = 64
D_FF = 4096
IN_WIDTH = 2048
ROPE_THETA = 10000.0
NORM_EPS = 1e-6
MASK_VALUE = -1e30

CHUNK = 16
LANES = 128
SUBLANES = 8
BF16_ROWS = 16
N_SEG = SUBLANES
SLOTS = LANES // SSM_GROUP
CW = CHUNK * SSM_GROUP
VMEM_LIMIT = 56 * 1024 * 1024
HI = lax.Precision.HIGHEST
LOG2E = 1.4426950408889634
LN2 = 0.6931471805599453


def _cparams(sem):
    return pltpu.CompilerParams(dimension_semantics=sem, vmem_limit_bytes=VMEM_LIMIT)


def _slot_masks():
    lane = lax.broadcasted_iota(jnp.int32, (1, LANES), 1)
    return [(lane // SSM_GROUP) == s for s in range(SLOTS)]


def _step_of_slot():
    gam = np.arange(SLOTS)[:, None, None]
    hh = np.arange(CHUNK // SLOTS)[None, :, None]
    l = np.arange(SLOTS)[None, None, :]
    return (SLOTS * hh + (l - gam) % SLOTS).reshape(SLOTS, CHUNK)


def _inproj_kernel(x_ref, g_ref, w_ref, cos_ref, sin_ref,
                   q1_ref, q4_ref, q16_ref, k1_ref, k4_ref, k16_ref, v1_ref, v4_ref, v16_ref,
                   ug_ref, rs_ref, st_ref):
    x = x_ref[...]
    tile = x.shape[0]
    ms = jnp.mean(x * x, axis=-1, keepdims=True)
    h = (x * lax.rsqrt(ms + NORM_EPS)) * g_ref[...]
    proj = jnp.dot(h.astype(BF16), w_ref[...], preferred_element_type=F32)
    cos = cos_ref[...]
    sin = sin_ref[...]
    lane = lax.broadcasted_iota(jnp.int32, cos.shape, 1)
    first_half = (lane & (HEAD_DIM // 2)) == 0
    n_lt = ATT_WIDTH // LANES

    def rot(t):
        partner = jnp.where(first_half, pltpu.roll(t, LANES - HEAD_DIM // 2, 1),
                            pltpu.roll(t, HEAD_DIM // 2, 1))
        return t * cos + partner * sin

    def spread(lane_tile, nat_ref, d4_ref, d16_ref):
        for c in range(n_lt):
            rs_ref[c] = lane_tile(c)
            nat_ref[:, c * LANES:(c + 1) * LANES] = rs_ref[c].astype(BF16)
        for c in range(n_lt):
            for r4 in range(4):
                blk = rs_ref[c, pl.ds(r4, tile // 4, stride=4), :]
                d4_ref[r4, :, c * LANES:(c + 1) * LANES] = blk.astype(BF16)
                st_ref[c * 4 + r4] = blk
        for c in range(n_lt):
            for r4 in range(4):
                for r2 in range(4):
                    blk = st_ref[c * 4 + r4, pl.ds(r2, tile // 16, stride=4), :]
                    d16_ref[r4 + 4 * r2, :, c * LANES:(c + 1) * LANES] = blk.astype(BF16)

    q_scale = (HEAD_DIM ** -0.5) * LOG2E
    spread(lambda c: rot(proj[:, c * LANES:(c + 1) * LANES]) * q_scale, q1_ref, q4_ref, q16_ref)
    spread(lambda c: rot(proj[:, ATT_WIDTH + c * LANES:ATT_WIDTH + (c + 1) * LANES]), k1_ref, k4_ref, k16_ref)
    spread(lambda c: proj[:, 2 * ATT_WIDTH + c * LANES:2 * ATT_WIDTH + (c + 1) * LANES], v1_ref, v4_ref, v16_ref)
    u = proj[:, 3 * ATT_WIDTH:]

    n_chunk = tile // CHUNK
    masks = _slot_masks()
    for qt in range(SSM_WIDTH // LANES):
        rs_ref[qt] = u[:, qt * LANES:(qt + 1) * LANES]
    for qt in range(SSM_WIDTH // LANES):
        for b in range(4):
            st_ref[b] = rs_ref[qt, pl.ds(b, tile // 4, stride=4), :]
        for hh in range(CHUNK // SLOTS):
            rolled = []
            for t8 in range(SLOTS):
                t = SLOTS * hh + t8
                step_rows = st_ref[t % 4, pl.ds(t // 4, n_chunk, stride=4), :]
                rolled.append(step_rows if t8 == 0 else pltpu.roll(step_rows, SSM_GROUP * t8, 1))
            for gam in range(SLOTS):
                res = rolled[SLOTS - 1]
                for t8 in range(SLOTS - 2, -1, -1):
                    res = jnp.where(masks[(gam + t8) % SLOTS], rolled[t8], res)
                ug_ref[SLOTS * qt + gam, :, hh * LANES:(hh + 1) * LANES] = res.astype(BF16)


def _inproj(x2, g, w_bf, cos_t, sin_t, seq, tile):
    n = x2.shape[0]
    n_pos = seq // tile
    tok = lambda i: (i, 0)
    pos = lambda i: (i % n_pos, 0)
    const = lambda i: (0, 0)
    split = lambda i: (0, i, 0)
    qkv_specs = [pl.BlockSpec((tile, ATT_WIDTH), tok), pl.BlockSpec((4, tile // 4, ATT_WIDTH), split),
                 pl.BlockSpec((16, tile // 16, ATT_WIDTH), split)]
    qkv_shapes = [jax.ShapeDtypeStruct((n, ATT_WIDTH), BF16), jax.ShapeDtypeStruct((4, n // 4, ATT_WIDTH), BF16),
                  jax.ShapeDtypeStruct((16, n // 16, ATT_WIDTH), BF16)]
    return pl.pallas_call(
        _inproj_kernel,
        grid=(n // tile,),
        in_specs=[
            pl.BlockSpec((tile, D_MODEL), tok),
            pl.BlockSpec((1, D_MODEL), const),
            pl.BlockSpec((D_MODEL, IN_WIDTH), const, pipeline_mode=pl.Buffered(1)),
            pl.BlockSpec((tile, LANES), pos),
            pl.BlockSpec((tile, LANES), pos),
        ],
        out_specs=qkv_specs * 3 + [pl.BlockSpec((N_GROUPS, tile // CHUNK, CW), split)],
        out_shape=qkv_shapes * 3 + [jax.ShapeDtypeStruct((N_GROUPS, n // CHUNK, CW), BF16)],
        scratch_shapes=[pltpu.VMEM((ATT_WIDTH // LANES, tile, LANES), F32),
                        pltpu.VMEM((4 * ATT_WIDTH // LANES, tile // 4, LANES), F32)],
        compiler_params=_cparams(("arbitrary",)),
        name="inproj",
    )(x2, g, w_bf, cos_t, sin_t)


def _attn_kernel(q_ref, kp_ref, kc_ref, kn_ref, vp_ref, vc_ref, vn_ref, o_ref, lse_ref,
                 kx_ref, vx_ref, bias_ref, s_ref, p_ref, m_ref, ost_ref, lst_ref, nat_ref, mid_ref,
                 *, dil, tm, qb, tiles_per_seq):
    tile_in_seq = pl.program_id(0) % tiles_per_seq
    n_sb = tm // qb
    if n_sb > 1:
        kx_ref[:, 0:RADIUS] = kp_ref[...]
        kx_ref[:, RADIUS:RADIUS + tm] = kc_ref[...]
        kx_ref[:, RADIUS + tm:] = kn_ref[...]
        vx_ref[:, 0:RADIUS] = vp_ref[...]
        vx_ref[:, RADIUS:RADIUS + tm] = vc_ref[...]
        vx_ref[:, RADIUS + tm:] = vn_ref[...]

    def window(prev_ref, cur_ref, next_ref, ext_ref, r, row0, sl):
        if n_sb > 1:
            return ext_ref[r, row0:row0 + win, sl]
        return jnp.concatenate([prev_ref[r, :, sl], cur_ref[r, :, sl], next_ref[r, :, sl]], axis=0)

    win = qb + 2 * RADIUS
    rows = 2 * qb
    qi = lax.broadcasted_iota(jnp.int32, (rows, win), 0) & (qb - 1)
    kcol = lax.broadcasted_iota(jnp.int32, (rows, win), 1)
    band = (kcol >= qi) & (kcol <= qi + 2 * RADIUS)
    after_start = kcol >= RADIUS
    before_end = kcol < qb + RADIUS
    neg = jnp.full((rows, win), MASK_VALUE, F32)
    zero = jnp.zeros((rows, win), F32)
    bias_ref[0] = jnp.where(band, zero, neg)
    bias_ref[1] = jnp.where(band & after_start, zero, neg)
    bias_ref[2] = jnp.where(band & before_end, zero, neg)
    bias_ref[3] = jnp.where(band & after_start & before_end, zero, neg)
    first_tile = (tile_in_seq == 0).astype(jnp.int32)
    last_tile = (tile_in_seq == tiles_per_seq - 1).astype(jnp.int32)

    lane = lax.broadcasted_iota(jnp.int32, (qb, LANES), 1)
    head_a = lane < HEAD_DIM
    ones = jnp.ones((win, LANES), BF16)
    n_pair = N_HEADS // 2

    def item(r, sb):
        row0 = sb * qb
        edge = (first_tile if sb == 0 else 0) + (2 * last_tile if sb == n_sb - 1 else 0)
        bias = bias_ref[edge]
        for p in range(n_pair):
            sl = slice(p * LANES, (p + 1) * LANES)
            q2 = q_ref[r, row0:row0 + qb, sl]
            zq = jnp.zeros_like(q2)
            qs = jnp.concatenate([jnp.where(head_a, q2, zq), jnp.where(head_a, zq, q2)], axis=0)
            kw = window(kp_ref, kc_ref, kn_ref, kx_ref, r, row0, sl)
            s_ref[p] = lax.dot_general(qs, kw, (((1,), (1,)), ((), ())), preferred_element_type=F32)
        for p in range(n_pair):
            s = s_ref[p] + bias
            m = jnp.max(s, axis=-1, keepdims=True)
            p_ref[p] = jnp.exp2(s - m).astype(BF16)
            m_ref[p] = jnp.broadcast_to(m, (rows, LANES))
        m8 = jnp.zeros((qb, LANES), F32)
        l8 = jnp.ones((qb, LANES), F32)
        for p in range(n_pair):
            sl = slice(p * LANES, (p + 1) * LANES)
            vaug = jnp.concatenate([window(vp_ref, vc_ref, vn_ref, vx_ref, r, row0, sl), ones], axis=1)
            res = jnp.dot(p_ref[p], vaug, preferred_element_type=F32)
            den = res[:, LANES:]
            mrow = m_ref[p]
            acc = jnp.where(head_a, res[:qb, :LANES], res[qb:, :LANES])
            out = acc / jnp.where(head_a, den[:qb], den[qb:])
            if dil == 1:
                o_ref[row0:row0 + qb, sl] = out.astype(o_ref.dtype)
            else:
                ost_ref[r, row0:row0 + qb, sl] = out
            m8 = jnp.where(lane == 2 * p, mrow[:qb], jnp.where(lane == 2 * p + 1, mrow[qb:], m8))
            l8 = jnp.where(lane == 2 * p, den[:qb], jnp.where(lane == 2 * p + 1, den[qb:], l8))
        lse = m8 * LN2 + jnp.log(l8)
        if dil == 1:
            lse_ref[row0:row0 + qb, :] = lse
        else:
            lst_ref[r, row0:row0 + qb, :] = lse

    for w in range(dil * n_sb):
        item(w // n_sb, w % n_sb)

    n_lt = ATT_WIDTH // LANES
    if dil > 1:
        def gather(stage_ref, lt):
            sl = slice(lt * LANES, (lt + 1) * LANES)
            if dil == 4:
                for r in range(4):
                    nat_ref[pl.ds(r, tm, stride=4), :] = stage_ref[r, :, sl]
            else:
                for r4 in range(4):
                    for r2 in range(4):
                        mid_ref[r4, pl.ds(r2, tm, stride=4), :] = stage_ref[r4 + 4 * r2, :, sl]
                for r4 in range(4):
                    nat_ref[pl.ds(r4, 4 * tm, stride=4), :] = mid_ref[r4]
            return nat_ref[...]

        for lt in range(n_lt):
            o_ref[:, lt * LANES:(lt + 1) * LANES] = gather(ost_ref, lt).astype(o_ref.dtype)
        lse_ref[...] = gather(lst_ref, 0)


def _banded_attention(q, k, v, dil, seq, nat_tile):
    _, n_rows, _ = q.shape
    n = n_rows * dil
    nat_tile = min(nat_tile, seq)
    tm = nat_tile // dil
    qb = min(2 * RADIUS, tm)
    assert dil in (1, 4, 16) and seq % nat_tile == 0 and tm % qb == 0 and qb % RADIUS == 0
    hb = tm // RADIUS
    n_halo = n_rows // RADIUS
    win = qb + 2 * RADIUS
    ext_shape = (dil, tm + 2 * RADIUS, ATT_WIDTH) if tm > qb else (1, BF16_ROWS, LANES)
    stage_rows = tm if dil > 1 else SUBLANES
    cur = pl.BlockSpec((dil, tm, ATT_WIDTH), lambda i: (0, i, 0))
    prev = pl.BlockSpec((dil, RADIUS, ATT_WIDTH), lambda i: (0, jnp.maximum(i * hb - 1, 0), 0))
    nxt = pl.BlockSpec((dil, RADIUS, ATT_WIDTH), lambda i: (0, jnp.minimum((i + 1) * hb, n_halo - 1), 0))
    return pl.pallas_call(
        functools.partial(_attn_kernel, dil=dil, tm=tm, qb=qb, tiles_per_seq=seq // nat_tile),
        grid=(n // nat_tile,),
        in_specs=[cur, prev, cur, nxt, prev, cur, nxt],
        out_specs=[pl.BlockSpec((nat_tile, ATT_WIDTH), lambda i: (i, 0)),
                   pl.BlockSpec((nat_tile, LANES), lambda i: (i, 0))],
        out_shape=[jax.ShapeDtypeStruct((n, ATT_WIDTH), BF16), jax.ShapeDtypeStruct((n, LANES), F32)],
        scratch_shapes=[
            pltpu.VMEM(ext_shape, BF16),
            pltpu.VMEM(ext_shape, BF16),
            pltpu.VMEM((4, 2 * qb, win), F32),
            pltpu.VMEM((N_HEADS // 2, 2 * qb, win), F32),
            pltpu.VMEM((N_HEADS // 2, 2 * qb, win), BF16),
            pltpu.VMEM((N_HEADS // 2, 2 * qb, LANES), F32),
            pltpu.VMEM((dil, stage_rows, ATT_WIDTH), F32),
            pltpu.VMEM((dil, stage_rows, LANES), F32),
            pltpu.VMEM((dil * stage_rows, LANES), F32),
            pltpu.VMEM((4, dil * stage_rows // 4, LANES), F32),
        ],
        compiler_params=_cparams(("arbitrary",)),
        name=f"attn_d{dil}",
    )(q, k, k, k, v, v, v)


def _s5_kernel(ug_ref, bs_ref, mc_ref, pw_ref, dec_ref, y_ref, x_ref, h_ref, *, n_i, n_grp):
    half = LANES // 2
    for g in range(n_grp):
        z = jnp.dot(ug_ref[g], bs_ref[g], preferred_element_type=F32)
        for seg in range(N_SEG):
            rows = slice(seg * n_i, (seg + 1) * n_i)
            x_ref[2 * g, pl.ds(seg, n_i, stride=N_SEG), :] = z[rows, :LANES]
            x_ref[2 * g + 1, pl.ds(seg, n_i, stride=N_SEG), :] = z[rows, LANES:]
    lane = lax.broadcasted_iota(jnp.int32, (N_SEG, LANES), 1)
    fwd = lane < half
    a_re = [jnp.broadcast_to(dec_ref[g, 0:1, :LANES], (N_SEG, LANES)) for g in range(n_grp)]
    a_im = [jnp.broadcast_to(dec_ref[g, 0:1, LANES:], (N_SEG, LANES)) for g in range(n_grp)]

    def scan_body(s, carry):
        rf = pl.ds(pl.multiple_of(s * N_SEG, N_SEG), N_SEG)
        rb = pl.ds(pl.multiple_of((n_i - 1 - s) * N_SEG, N_SEG), N_SEG)
        out = []
        for g in range(n_grp):
            hre, him = carry[2 * g], carry[2 * g + 1]
            h_ref[2 * g, rf, 0:half] = hre[:, :half]
            h_ref[2 * g, rb, half:] = hre[:, half:]
            h_ref[2 * g + 1, rf, 0:half] = him[:, :half]
            h_ref[2 * g + 1, rb, half:] = him[:, half:]
            xre = jnp.where(fwd, x_ref[2 * g, rf, :], x_ref[2 * g, rb, :])
            xim = jnp.where(fwd, x_ref[2 * g + 1, rf, :], x_ref[2 * g + 1, rb, :])
            out += [a_re[g] * hre - a_im[g] * him + xre, a_re[g] * him + a_im[g] * hre + xim]
        return tuple(out)

    zero = jnp.zeros((N_SEG, LANES), F32)
    ends = lax.fori_loop(0, n_i, scan_body, (zero,) * (2 * n_grp))

    lane1 = lax.broadcasted_iota(jnp.int32, (1, LANES), 1)
    fwd1 = lane1 < half
    entry = []
    for g in range(n_grp):
        l_re, l_im = ends[2 * g], ends[2 * g + 1]
        s_re = dec_ref[g, 1:2, :LANES]
        s_im = dec_ref[g, 1:2, LANES:]
        e_re = jnp.zeros((1, LANES), F32)
        e_im = jnp.zeros((1, LANES), F32)
        ins = []
        for s in range(N_SEG):
            ins.append((e_re, e_im))
            lre = jnp.where(fwd1, l_re[s:s + 1], l_re[N_SEG - 1 - s:N_SEG - s])
            lim = jnp.where(fwd1, l_im[s:s + 1], l_im[N_SEG - 1 - s:N_SEG - s])
            e_re, e_im = s_re * e_re - s_im * e_im + lre, s_re * e_im + s_im * e_re + lim
        entry.append((
            jnp.concatenate([jnp.where(fwd1, ins[s][0], ins[N_SEG - 1 - s][0]) for s in range(N_SEG)], axis=0),
            jnp.concatenate([jnp.where(fwd1, ins[s][1], ins[N_SEG - 1 - s][1]) for s in range(N_SEG)], axis=0)))

    def fix_body(ib, c):
        for g in range(n_grp):
            ein_re, ein_im = entry[g]
            pw = pw_ref[g, pl.ds(pl.multiple_of(ib * SUBLANES, SUBLANES), SUBLANES), :]
            for r in range(SUBLANES):
                rows = pl.ds(pl.multiple_of((ib * SUBLANES + r) * N_SEG, N_SEG), N_SEG)
                p_re = pw[r:r + 1, :LANES]
                p_im = pw[r:r + 1, LANES:]
                h_ref[2 * g, rows, :] = h_ref[2 * g, rows, :] + (p_re * ein_re - p_im * ein_im)
                h_ref[2 * g + 1, rows, :] = h_ref[2 * g + 1, rows, :] + (p_re * ein_im + p_im * ein_re)
        return c

    lax.fori_loop(0, n_i // SUBLANES, fix_body, 0)
    for g in range(n_grp):
        ungroup = lambda c: jnp.concatenate(
            [h_ref[c, pl.ds(seg, n_i, stride=N_SEG), :] for seg in range(N_SEG)], axis=0)
        hcat = jnp.concatenate([ungroup(2 * g), ungroup(2 * g + 1)], axis=1).astype(BF16)
        y_ref[g] = jnp.dot(jnp.concatenate([ug_ref[g], hcat], axis=1), mc_ref[g], preferred_element_type=F32)


def _s5_scan(ug, bs, mc, pw, dec, bsz, n_grp=8):
    n_g, total_rows, width = ug.shape
    rows = total_rows // bsz
    n_i = rows // N_SEG
    grp = lambda b, g: (g, 0, 0)
    seq_rows = pl.BlockSpec((n_grp, rows, width), lambda b, g: (g, b, 0))
    return pl.pallas_call(
        functools.partial(_s5_kernel, n_i=n_i, n_grp=n_grp),
        grid=(bsz, n_g // n_grp),
        in_specs=[
            seq_rows,
            pl.BlockSpec((n_grp, width, width), grp),
            pl.BlockSpec((n_grp, 2 * width, width), grp),
            pl.BlockSpec((n_grp, n_i, width), grp),
            pl.BlockSpec((n_grp, 2, width), grp),
        ],
        out_specs=seq_rows,
        out_shape=jax.ShapeDtypeStruct((n_g, total_rows, width), F32),
        scratch_shapes=[pltpu.VMEM((2 * n_grp, rows, LANES), F32)] * 2,
        compiler_params=_cparams(("arbitrary",) * 2),
        name="s5_scan",
    )(ug, bs, mc, pw, dec)


def _within_chunk_kernel(kin_ref, m_ref):
    step_of = _step_of_slot()
    masks = _slot_masks()
    for gam in range(SLOTS):
        rolled = {}

        def piece(x, slot):
            key = (x // SLOTS, (slot - x) % SLOTS)
            if key not in rolled:
                src = kin_ref[gam, :, key[0] * LANES:(key[0] + 1) * LANES]
                rolled[key] = src if key[1] == 0 else pltpu.roll(src, SSM_GROUP * key[1], 1)
            return rolled[key]

        for s_pos in range(CHUNK):
            for hh in range(CHUNK // SLOTS):
                lag = lambda l: int(step_of[gam, SLOTS * hh + l] - step_of[gam, s_pos]) + CHUNK - 1
                acc = piece(lag(SLOTS - 1), SLOTS - 1)
                for l in range(SLOTS - 2, -1, -1):
                    acc = jnp.where(masks[l], piece(lag(l), l), acc)
                m_ref[gam, s_pos * SSM_GROUP:(s_pos + 1) * SSM_GROUP, hh * LANES:(hh + 1) * LANES] = acc.astype(BF16)


def _within_chunk(kin):
    n_lag_lanes = kin.shape[-1]
    return pl.pallas_call(
        _within_chunk_kernel,
        grid=(N_GROUPS // SLOTS,),
        in_specs=[pl.BlockSpec((SLOTS, SSM_GROUP, n_lag_lanes), lambda i: (i, 0, 0))],
        out_specs=pl.BlockSpec((SLOTS, CW, CW), lambda i: (i, 0, 0)),
        out_shape=jax.ShapeDtypeStruct((N_GROUPS, CW, CW), BF16),
        compiler_params=_cparams(("arbitrary",)),
        name="s5_within_chunk",
    )(kin)


def _s5_operators(a_re, a_im, log_step, b_re, b_im, c_re, c_im, d_skip, n_i):
    a_re, a_im, b_re, b_im, c_re, c_im = (t.astype(F32) for t in (a_re, a_im, b_re, b_im, c_re, c_im))
    step = jnp.exp(log_step.astype(F32))[..., None]
    zr, zi = a_re * step, a_im * step

    def power(n):
        n = jnp.asarray(n, F32).reshape(n.shape + (1, 1, 1))
        mag = jnp.exp(n * zr)
        return mag * jnp.cos(n * zi), mag * jnp.sin(n * zi)

    abr, abi = power(np.array(1))
    den = a_re * a_re + a_im * a_im
    fr = ((abr - 1.0) * a_re + abi * a_im) / den
    fi = (abi * a_re - (abr - 1.0) * a_im) / den
    bbr = fr[..., None] * b_re - fi[..., None] * b_im
    bbi = fr[..., None] * b_im + fi[..., None] * b_re

    pr, pi = power(np.arange(CHUNK + 1))
    wr = pr[:CHUNK, ..., None] * bbr - pi[:CHUNK, ..., None] * bbi
    wi = pr[:CHUNK, ..., None] * bbi + pi[:CHUNK, ..., None] * bbr
    kern = (jnp.einsum('dgcp,tdgpe->dgtce', c_re, wr, precision=HI)
            - jnp.einsum('dgcp,tdgpe->dgtce', c_im, wi, precision=HI))

    step_of = _step_of_slot()
    n_q = N_GROUPS // SLOTS
    lag = np.arange(2 * CHUNK - 1)[:, None] - (CHUNK - 1)
    tau = np.arange(CHUNK)[None, :]
    pick = jnp.asarray(np.stack([lag == tau, -lag == tau], axis=1), F32)
    kin = jnp.einsum('xdt,dgtce->gexc', pick, kern, precision=HI)
    lag0 = jnp.asarray(np.arange(2 * CHUNK - 1) == CHUNK - 1, F32)
    kin = kin + (d_skip.astype(F32).reshape(N_GROUPS, SSM_GROUP, 1, 1) * lag0[None, None, :, None]
                 * jnp.eye(SSM_GROUP, dtype=F32)[None, :, None, :])
    kin = jnp.pad(kin.reshape(N_GROUPS, SSM_GROUP, (2 * CHUNK - 1) * SSM_GROUP),
                  ((0, 0), (0, 0), (0, SSM_GROUP)))
    m = _within_chunk(kin)

    onehot = lambda idx: jnp.asarray(idx[..., None] == np.arange(CHUNK + 1), F32)
    grouped = lambda t: t.reshape((t.shape[0], n_q, SLOTS) + t.shape[2:])
    take = lambda oh, t: jnp.einsum('ypn,nqyk->qypk', oh, grouped(t), precision=HI)

    def state_in(d, idx):
        ar, ai = take(onehot(idx), pr[:, d]), take(onehot(idx), pi[:, d])
        br = jnp.swapaxes(grouped(bbr[d][None])[0], -1, -2)
        bi = jnp.swapaxes(grouped(bbi[d][None])[0], -1, -2)
        re = ar[:, :, :, None, :] * br[:, :, None] - ai[:, :, :, None, :] * bi[:, :, None]
        im = ar[:, :, :, None, :] * bi[:, :, None] + ai[:, :, :, None, :] * br[:, :, None]
        return re.reshape(N_GROUPS, CW, SSM_STATE), im.reshape(N_GROUPS, CW, SSM_STATE)

    bf_re, bf_im = state_in(0, CHUNK - 1 - step_of)
    bb_re, bb_im = state_in(1, step_of)
    bs = jnp.concatenate([bf_re, bb_re, bf_im, bb_im], axis=-1).astype(BF16)

    def state_out(d, idx):
        ar, ai = take(onehot(idx), pr[:, d]), take(onehot(idx), pi[:, d])
        cr = jnp.swapaxes(grouped(c_re[d][None])[0], -1, -2)
        ci = jnp.swapaxes(grouped(c_im[d][None])[0], -1, -2)
        ar, ai = jnp.swapaxes(ar, -1, -2)[..., None], jnp.swapaxes(ai, -1, -2)[..., None]
        re = cr[:, :, :, None, :] * ar - ci[:, :, :, None, :] * ai
        im = cr[:, :, :, None, :] * ai + ci[:, :, :, None, :] * ar
        return re.reshape(N_GROUPS, SSM_STATE, CW), im.reshape(N_GROUPS, SSM_STATE, CW)

    cf_re, cf_im = state_out(0, step_of + 1)
    cb_re, cb_im = state_out(1, CHUNK - step_of)
    cs = jnp.concatenate([cf_re, cb_re, -cf_im, -cb_im], axis=1).astype(BF16)

    i_idx = np.arange(n_i)
    pf_re, pf_im = power(CHUNK * i_idx)
    pb_re, pb_im = power(CHUNK * (n_i - 1 - i_idx))
    pw = jnp.concatenate([pf_re[:, 0], pb_re[:, 1], pf_im[:, 0], pb_im[:, 1]], axis=-1)
    pw = jnp.transpose(pw, (1, 0, 2))
    dr, di = power(np.array([CHUNK, CHUNK * n_i]))
    dec = jnp.concatenate([dr[:, 0], dr[:, 1], di[:, 0], di[:, 1]], axis=-1)
    dec = jnp.transpose(dec, (1, 0, 2))
    return bs, jnp.concatenate([m, cs], axis=1), pw, dec


def _rms(x, g):
    return (x * lax.rsqrt(jnp.mean(x * x, axis=-1, keepdims=True) + NORM_EPS)) * g


def _mix_kernel(o1_ref, o4_ref, o16_ref, l1_ref, l4_ref, l16_ref, yg_ref, x_ref,
                wg_ref, bg_ref, wo_ref, g_ref, out_ref, ys_ref, yq_ref):
    n_chunk = x_ref.shape[0] // CHUNK
    masks = _slot_masks()
    for qt in range(SSM_WIDTH // LANES):
        for hh in range(CHUNK // SLOTS):
            src = [yg_ref[SLOTS * qt + gam, :, hh * LANES:(hh + 1) * LANES] for gam in range(SLOTS)]
            for t8 in range(SLOTS):
                pre = src[SLOTS - 1]
                for gam in range(SLOTS - 2, -1, -1):
                    pre = jnp.where(masks[(gam + t8) % SLOTS], src[gam], pre)
                nat = pre if t8 == 0 else pltpu.roll(pre, LANES - SSM_GROUP * t8, 1)
                t = SLOTS * hh + t8
                yq_ref[t % 4, pl.ds(t // 4, n_chunk, stride=4), :] = nat
        for b in range(4):
            ys_ref[qt, pl.ds(b, 4 * n_chunk, stride=4), :] = yq_ref[b]
    ys = jnp.concatenate([ys_ref[qt] for qt in range(SSM_WIDTH // LANES)], axis=1)

    l1, l4, l16 = l1_ref[...], l4_ref[...], l16_ref[...]
    mx = jnp.maximum(jnp.maximum(l1, l4), l16)
    e1, e4, e16 = jnp.exp(l1 - mx), jnp.exp(l4 - mx), jnp.exp(l16 - mx)
    inv = 1.0 / (e1 + e4 + e16)
    head_row = lax.broadcasted_iota(jnp.int32, (2 * LANES, ATT_WIDTH), 0) % LANES
    head_col = lax.broadcasted_iota(jnp.int32, (2 * LANES, ATT_WIDTH), 1) // HEAD_DIM
    spread = jnp.where(head_row == head_col, 1.0, 0.0).astype(BF16)

    def per_column(w):
        hi = w.astype(BF16)
        lo = (w - hi.astype(F32)).astype(BF16)
        return jnp.dot(jnp.concatenate([hi, lo], axis=1), spread, preferred_element_type=F32)

    att = (per_column(e1 * inv) * o1_ref[...].astype(F32) + per_column(e4 * inv) * o4_ref[...].astype(F32)
           + per_column(e16 * inv) * o16_ref[...].astype(F32))
    y = ys
    y = 0.5 * y * (1.0 + jnp.tanh(math.sqrt(2.0 / math.pi) * (y + 0.044715 * (y * y * y))))
    gate = jnp.dot(y.astype(BF16), wg_ref[...], preferred_element_type=F32) + bg_ref[...]
    ssm = y * (1.0 / (1.0 + jnp.exp(-gate)))
    mixed = jnp.dot(jnp.concatenate([att.astype(BF16), ssm.astype(BF16)], axis=1), wo_ref[...],
                    preferred_element_type=F32)
    out_ref[...] = x_ref[...] + _rms(mixed, g_ref[...])


def _mix(o1, o4, o16, l1, l4, l16, yg, x2, wg, bg, wo, g, tile):
    n = x2.shape[0]
    tok = lambda i: (i, 0)
    const = lambda i: (0, 0)
    half = pl.BlockSpec((tile, ATT_WIDTH), tok)
    stat = pl.BlockSpec((tile, LANES), tok)
    return pl.pallas_call(
        _mix_kernel,
        grid=(n // tile,),
        in_specs=[half] * 3 + [stat] * 3 + [
            pl.BlockSpec((N_GROUPS, tile // CHUNK, CW), lambda i: (0, i, 0)),
            pl.BlockSpec((tile, D_MODEL), tok),
            pl.BlockSpec((SSM_WIDTH, SSM_WIDTH), const, pipeline_mode=pl.Buffered(1)),
            pl.BlockSpec((1, SSM_WIDTH), const),
            pl.BlockSpec((D_MODEL, D_MODEL), const, pipeline_mode=pl.Buffered(1)),
            pl.BlockSpec((1, D_MODEL), const),
        ],
        out_specs=pl.BlockSpec((tile, D_MODEL), tok),
        out_shape=jax.ShapeDtypeStruct((n, D_MODEL), F32),
        scratch_shapes=[pltpu.VMEM((SSM_WIDTH // LANES, tile, LANES), F32),
                        pltpu.VMEM((4, tile // 4, LANES), F32)],
        compiler_params=_cparams(("arbitrary",)),
        name="mix",
    )(o1, o4, o16, l1, l4, l16, yg, x2, wg, bg, wo, g)


def _mlp_kernel(x_ref, gpre_ref, wu_ref, wd_ref, gpost_ref, out_ref, *, ff_chunk):
    x = x_ref[...]
    h = _rms(x, gpre_ref[...]).astype(BF16)
    acc = jnp.zeros(x.shape, F32)
    for f in range(D_FF // ff_chunk):
        sl = slice(f * ff_chunk, (f + 1) * ff_chunk)
        a = jnp.maximum(jnp.dot(h, wu_ref[:, sl], preferred_element_type=F32), 0.0)
        acc = acc + jnp.dot((a * a).astype(BF16), wd_ref[sl, :], preferred_element_type=F32)
    out_ref[...] = x + _rms(acc, gpost_ref[...])


def _mlp(x2, gpre, wu, wd, gpost, tile, ff_chunk):
    n = x2.shape[0]
    tok = lambda i: (i, 0)
    const = lambda i: (0, 0)
    return pl.pallas_call(
        functools.partial(_mlp_kernel, ff_chunk=ff_chunk),
        grid=(n // tile,),
        in_specs=[
            pl.BlockSpec((tile, D_MODEL), tok),
            pl.BlockSpec((1, D_MODEL), const),
            pl.BlockSpec((D_MODEL, D_FF), const, pipeline_mode=pl.Buffered(1)),
            pl.BlockSpec((D_FF, D_MODEL), const, pipeline_mode=pl.Buffered(1)),
            pl.BlockSpec((1, D_MODEL), const),
        ],
        out_specs=pl.BlockSpec((tile, D_MODEL), tok),
        out_shape=jax.ShapeDtypeStruct((n, D_MODEL), F32),
        compiler_params=_cparams(("arbitrary",)),
        name="mlp",
    )(x2, gpre, wu, wd, gpost)


def _rotary_tables(seq):
    half = HEAD_DIM // 2
    inv_freq = 1.0 / (ROPE_THETA ** (jnp.arange(half, dtype=F32) / half))
    ang = jnp.arange(seq, dtype=F32)[:, None] * inv_freq[None, :]
    cos, sin = jnp.cos(ang), jnp.sin(ang)
    return jnp.tile(cos, (1, 4)), jnp.tile(jnp.concatenate([-sin, sin], axis=-1), (1, 2))


def _prepare(seq, norm_mix_pre, w_in, a_re, a_im, log_step, b_re, b_im, c_re, c_im, d_skip, w_glu, b_glu,
             w_out, norm_mix_post, norm_mlp_pre, w_up, w_down, norm_mlp_post):
    row = lambda t: t.reshape(1, -1).astype(F32)
    return dict(
        tables=_rotary_tables(seq),
        s5=_s5_operators(a_re, a_im, log_step, b_re, b_im, c_re, c_im, d_skip, seq // (CHUNK * N_SEG)),
        g_mix_pre=row(norm_mix_pre), w_in=w_in.astype(BF16), w_glu=w_glu.astype(BF16),
        b_glu=row(b_glu), w_out=w_out.astype(BF16), g_mix_post=row(norm_mix_post),
        g_mlp_pre=row(norm_mlp_pre), w_up=w_up.astype(BF16), w_down=w_down.astype(BF16),
        g_mlp_post=row(norm_mlp_post))


def _layer(x, p, tok_tile=1024, attn_tile=2048):
    bsz, seq, _ = x.shape
    n = bsz * seq
    x2 = x.reshape(n, D_MODEL)
    cos_t, sin_t = p['tables']
    bs, mc, pw, dec = p['s5']

    q1, q4, q16, k1, k4, k16, v1, v4, v16, ug = _inproj(
        x2, p['g_mix_pre'], p['w_in'], cos_t, sin_t, seq, tok_tile)
    (o1, l1), (o4, l4), (o16, l16) = (
        _banded_attention(q1[None], k1[None], v1[None], 1, seq, attn_tile),
        _banded_attention(q4, k4, v4, 4, seq, attn_tile),
        _banded_attention(q16, k16, v16, 16, seq, attn_tile))
    yg = _s5_scan(ug, bs, mc, pw, dec, bsz)
    x1 = _mix(o1, o4, o16, l1, l4, l16, yg, x2, p['w_glu'], p['b_glu'], p['w_out'], p['g_mix_post'], tok_tile)
    y = _mlp(x1, p['g_mlp_pre'], p['w_up'], p['w_down'], p['g_mlp_post'], tok_tile, 1024)
    return y.reshape(bsz, seq, D_MODEL)


def kernel(x_prompt, x_sample, norm_mix_pre, w_in, ssm_a_re, ssm_a_im, ssm_log_step, ssm_b_re, ssm_b_im,
           ssm_c_re, ssm_c_im, ssm_d, w_glu, b_glu, w_out, norm_mix_post, norm_mlp_pre, w_up, w_down,
           norm_mlp_post):
    weights = (norm_mix_pre, w_in, ssm_a_re, ssm_a_im, ssm_log_step, ssm_b_re, ssm_b_im, ssm_c_re, ssm_c_im,
               ssm_d, w_glu, b_glu, w_out, norm_mix_post, norm_mlp_pre, w_up, w_down, norm_mlp_post)
    depth = norm_mix_pre.shape[0]
    prepared = {}

    def run(x):
        seq = x.shape[1]
        if seq not in prepared:
            prepared[seq] = [_prepare(seq, *(w[l] for w in weights)) for l in range(depth)]
        for l in range(depth):
            x = _layer(x, prepared[seq][l])
        return x

    return run(x_prompt), run(x_sample)
```
